```python
import jax, jax.numpy as jnp
from jax import lax
import numpy as np

D_MODEL = 2048
BATCH = 2
SEQ = 16384
DEPTH = 4

GRID_W = 64
CTX_LEN = 256
HEAD_DIM = 128
AXIS_DIM = HEAD_DIM // 2
ROPE_THETA = 10000.0
EPS = 1e-6

POOL_WINDOWS = (2, 4, 8, 16)
POOL_GROUPS = 4
POOL_CH = 128
POOL_WIDTH = POOL_GROUPS * POOL_CH

NA_HEADS = 6
NA_WIDTH = NA_HEADS * HEAD_DIM
NA_ROWS = 8
NA_COLS = 16

GQA_Q_HEADS = 6
GQA_KV_HEADS = 2
GQA_GROUP = GQA_Q_HEADS // GQA_KV_HEADS
GQA_Q_WIDTH = GQA_Q_HEADS * HEAD_DIM
GQA_KV_WIDTH = GQA_KV_HEADS * HEAD_DIM
Q_BLOCK = 128

N_BRANCH = 3
MIX_WIDTH = POOL_WIDTH + NA_WIDTH + GQA_Q_WIDTH

OFF_POOL = N_BRANCH * D_MODEL
OFF_NA_Q = OFF_POOL + POOL_WIDTH
OFF_GQA_Q = OFF_NA_Q + NA_WIDTH
OFF_KV = OFF_GQA_Q + GQA_Q_WIDTH
IN_COLS = OFF_KV + 2 * NA_WIDTH + 2 * GQA_KV_WIDTH

N_EXPERTS = 16
N_GROUPS = 4
EXPERTS_PER_GROUP = N_EXPERTS // N_GROUPS
TOP_K = 2
D_FF = 1024

N_MOD = 6

kernel_name = "hybrid_pool_na_gqa_moe_dit"


def rmsnorm(t, g):
    tf = t.astype(jnp.float32)
    y = tf * lax.rsqrt(jnp.mean(tf * tf, axis=-1, keepdims=True) + EPS)
    return (y * g.astype(jnp.float32)).astype(t.dtype)


def split_heads(t, n_heads):
    b, n, _ = t.shape
    return t.reshape(b, n, n_heads, HEAD_DIM).transpose(0, 2, 1, 3)


def merge_heads(t):
    b, h, n, d = t.shape
    return t.transpose(0, 2, 1, 3).reshape(b, n, h * d)


def axial_rope(n):
    t = jnp.arange(n, dtype=jnp.int32)
    row = (t // GRID_W).astype(jnp.float32)
    col = (t % GRID_W).astype(jnp.float32)
    inv = ROPE_THETA ** (-jnp.arange(0, AXIS_DIM, 2, dtype=jnp.float32) / AXIS_DIM)
    ang = jnp.concatenate([row[:, None] * inv, col[:, None] * inv], axis=-1)
    return jnp.cos(ang), jnp.sin(ang)


def apply_rope(t, cos, sin):
    half = HEAD_DIM // 2
    t1, t2 = t[..., :half], t[..., half:]
    cos = cos.astype(t.dtype)
    sin = sin.astype(t.dtype)
    return jnp.concatenate([t1 * cos - t2 * sin, t2 * cos + t1 * sin], axis=-1)


def q_heads(p, gains):
    na_q = rmsnorm(split_heads(p[..., OFF_NA_Q:OFF_GQA_Q], NA_HEADS), gains[0])
    g_q = rmsnorm(split_heads(p[..., OFF_GQA_Q:OFF_KV], GQA_Q_HEADS), gains[2])
    return na_q, g_q


def kv_heads(p_kv, gains):
    o1 = NA_WIDTH
    o2 = 2 * NA_WIDTH
    o3 = o2 + GQA_KV_WIDTH
    na_k = rmsnorm(split_heads(p_kv[..., :o1], NA_HEADS), gains[1])
    na_v = split_heads(p_kv[..., o1:o2], NA_HEADS)
    g_k = rmsnorm(split_heads(p_kv[..., o2:o3], GQA_KV_HEADS), gains[3])
    g_v = split_heads(p_kv[..., o3:], GQA_KV_HEADS)
    return na_k, na_v, g_k, g_v


def sdpa(q, k, v):
    s = jnp.einsum('bhgqd,bhkd->bhgqk', q, k, preferred_element_type=jnp.float32) * (HEAD_DIM ** -0.5)
    p = jax.nn.softmax(s, axis=-1)
    return jnp.einsum('bhgqk,bhkd->bhgqd', p.astype(v.dtype), v)


def pool_mixer(u, w_grp, scale):
    b, n, _ = u.shape
    uf = u.astype(jnp.float32)
    cs = jnp.concatenate([jnp.zeros((b, 1, POOL_WIDTH), jnp.float32), jnp.cumsum(uf, axis=1)], axis=1)
    t = jnp.arange(n)
    diffs = []
    for g, w in enumerate(POOL_WINDOWS):
        lo = jnp.clip(t - w // 2, 0, n)
        hi = jnp.clip(t + w // 2, 0, n)
        sl = slice(g * POOL_CH, (g + 1) * POOL_CH)
        csg = cs[..., sl]
        mean = (csg[:, hi] - csg[:, lo]) / (hi - lo).astype(jnp.float32)[:, None]
        diffs.append(mean - uf[..., sl])
    d = jnp.stack(diffs, axis=2).astype(u.dtype)
    y = jnp.einsum('bngc,gce->bnge', d, w_grp).reshape(b, n, POOL_WIDTH)
    return y * scale


def na_latent(q, k, v, k_ctx, v_ctx, rpb):
    b, h, n, hd = q.shape
    rows = n // GRID_W
    kr = min(NA_ROWS, rows)
    n_loc = kr * NA_COLS
    kg = k.reshape(b, h, rows, GRID_W, hd)
    vg = v.reshape(b, h, rows, GRID_W, hd)
    qg = q.reshape(b, h, rows, GRID_W, hd).transpose(2, 0, 1, 3, 4)
    col = jnp.arange(GRID_W)
    col_start = jnp.clip(col - NA_COLS // 2, 0, GRID_W - NA_COLS)
    col_idx = col_start[:, None] + jnp.arange(NA_COLS)[None, :]
    dc = col_idx - col[:, None] + (NA_COLS - 1)
    scale = HEAD_DIM ** -0.5

    def one_row(args):
        r, q_row = args
        rs = jnp.clip(r - kr // 2, 0, rows - kr)
        k_rows = lax.dynamic_slice_in_dim(kg, rs, kr, axis=2)
        v_rows = lax.dynamic_slice_in_dim(vg, rs, kr, axis=2)
        k_win = k_rows[:, :, :, col_idx].transpose(0, 1, 3, 2, 4, 5).reshape(b, h, GRID_W, n_loc, hd)
        v_win = v_rows[:, :, :, col_idx].transpose(0, 1, 3, 2, 4, 5).reshape(b, h, GRID_W, n_loc, hd)
        dr = rs + jnp.arange(kr) - r + (NA_ROWS - 1)
        bias = rpb[:, dr[:, None, None], dc[None, :, :]]
        bias = bias.transpose(0, 2, 1, 3).reshape(h, GRID_W, n_loc).astype(jnp.float32)
        s_loc = jnp.einsum('bhqd,bhqkd->bhqk', q_row, k_win, preferred_element_type=jnp.float32) * scale + bias
        s_ctx = jnp.einsum('bhqd,bhkd->bhqk', q_row, k_ctx, preferred_element_type=jnp.float32) * scale
        p = jax.nn.softmax(jnp.concatenate([s_loc, s_ctx], axis=-1), axis=-1).astype(v.dtype)
        return (jnp.einsum('bhqk,bhqkd->bhqd', p[..., :n_loc], v_win)
                + jnp.einsum('bhqk,bhkd->bhqd', p[..., n_loc:], v_ctx))

    o = lax.map(one_row, (jnp.arange(rows), qg))
    return o.transpose(1, 2, 0, 3, 4).reshape(b, h, n, hd)


def gqa_latent(q, k, v, k_ctx, v_ctx):
    b, hk, g, n, hd = q.shape
    k_all = jnp.concatenate([k_ctx, k], axis=2)
    v_all = jnp.concatenate([v_ctx, v], axis=2)
    nb = n // Q_BLOCK
    qb = q.reshape(b, hk, g, nb, Q_BLOCK, hd).transpose(3, 0, 1, 2, 4, 5)
    o = lax.map(lambda qi: sdpa(qi, k_all, v_all), qb)
    return o.transpose(1, 2, 3, 0, 4, 5).reshape(b, hk * g, n, hd)


def merge_branches(p, y_pool, y_na, y_gqa, w_br_l, w_out_l):
    gates = jax.nn.sigmoid(p[..., :OFF_POOL].astype(jnp.float32)).astype(p.dtype)
    g_a, g_b, g_c = jnp.split(gates, N_BRANCH, axis=-1)
    r1 = POOL_WIDTH
    r2 = POOL_WIDTH + NA_WIDTH
    z = (g_a * (y_pool @ w_br_l[:r1]) + g_b * (y_na @ w_br_l[r1:r2]) + g_c * (y_gqa @ w_br_l[r2:]))
    return z @ w_out_l


def moe(h, w_router, b_router, w_gu_l, w_down_l):
    shp = h.shape
    t = h.reshape(-1, D_MODEL)
    n_tok = t.shape[0]
    s = jax.nn.sigmoid(jnp.matmul(t, w_router, preferred_element_type=jnp.float32))
    sel = s + b_router.astype(jnp.float32)
    grp = sel.reshape(n_tok, N_GROUPS, EXPERTS_PER_GROUP)
    grp_score = lax.top_k(grp, TOP_K)[0].sum(-1)
    g_best = jnp.argmax(grp_score, axis=-1)
    in_group = jnp.take_along_axis(grp, g_best[:, None, None], axis=1)[:, 0]
    _, local = lax.top_k(in_group, TOP_K)
    idx = g_best[:, None] * EXPERTS_PER_GROUP + local
    w = jnp.take_along_axis(s, idx, axis=-1)
    w = w / jnp.sum(w, axis=-1, keepdims=True)
    dense_w = jnp.sum(jax.nn.one_hot(idx, N_EXPERTS, dtype=jnp.float32) * w[..., None], axis=1).astype(t.dtype)
    out = jnp.zeros_like(t)
    for e in range(N_EXPERTS):
        gu = t @ w_gu_l[e]
        a = jax.nn.silu(gu[:, :D_FF]) * gu[:, D_FF:]
        out = out + dense_w[:, e:e + 1] * (a @ w_down_l[e])
    return out.reshape(shp)


def setup_inputs(seed: int = 0) -> dict:
    key = jax.random.key(seed)
    ks = jax.random.split(key, 20)
    f32 = jnp.float32
    nrm = lambda k, shape, s: jax.random.normal(k, shape, f32) * s
    return {
        "x": nrm(ks[0], (BATCH, SEQ, D_MODEL), 1.0),
        "c": nrm(ks[1], (BATCH, D_MODEL), 1.0),
        "ctx": nrm(ks[2], (BATCH, CTX_LEN, D_MODEL), 1.0),
        "c_ctx": nrm(ks[3], (D_MODEL,), 1.0),
        "w_mod": nrm(ks[4], (DEPTH, D_MODEL, N_MOD * D_MODEL), 0.5 * D_MODEL ** -0.5),
        "b_mod": nrm(ks[5], (DEPTH, N_MOD * D_MODEL), 0.02),
        "norm1": 1.0 + nrm(ks[6], (DEPTH, D_MODEL), 0.02),
        "norm2": 1.0 + nrm(ks[7], (DEPTH, D_MODEL), 0.02),
        "w_in": nrm(ks[8], (DEPTH, D_MODEL, IN_COLS), D_MODEL ** -0.5),
        "qk_gain": 1.0 + nrm(ks[9], (DEPTH, 4, HEAD_DIM), 0.02),
        "pool_w": nrm(ks[10], (DEPTH, POOL_GROUPS, POOL_CH, POOL_CH), POOL_CH ** -0.5),
        "pool_scale": 1.0 + nrm(ks[11], (DEPTH, POOL_WIDTH), 0.1),
        "na_rpb": nrm(ks[12], (DEPTH, NA_HEADS, 2 * NA_ROWS - 1, 2 * NA_COLS - 1), 0.02),
        "w_br": nrm(ks[13], (DEPTH, MIX_WIDTH, D_MODEL), NA_WIDTH ** -0.5),
        "w_out": nrm(ks[14], (DEPTH, D_MODEL, D_MODEL), D_MODEL ** -0.5),
        "w_router": nrm(ks[15], (D_MODEL, N_EXPERTS), D_MODEL ** -0.5),
        "b_router": nrm(ks[16], (N_EXPERTS,), 0.01),
        "w_gu": nrm(ks[17], (DEPTH, N_EXPERTS, D_MODEL, 2 * D_FF), D_MODEL ** -0.5),
        "w_down": nrm(ks[18], (DEPTH, N_EXPERTS, D_FF, D_MODEL), D_FF ** -0.5),
    }


def reference(x, c, ctx, c_ctx, w_mod, b_mod, norm1, norm2, w_in, qk_gain, pool_w, pool_scale,
              na_rpb, w_br, w_out, w_router, b_router, w_gu, w_down):
    b, n, _ = x.shape
    cos, sin = axial_rope(n)
    s_c = jax.nn.silu(c)
    s_cc = jax.nn.silu(c_ctx)
    for l in range(DEPTH):
        last = l == DEPTH - 1
        gains = qk_gain[l]
        mx = (s_c @ w_mod[l] + b_mod[l])[:, None, :]
        mc = s_cc @ w_mod[l] + b_mod[l]
        x_sh1, x_sc1, x_g1, x_sh2, x_sc2, x_g2 = jnp.split(mx, N_MOD, axis=-1)
        c_sh1, c_sc1, c_g1, c_sh2, c_sc2, c_g2 = jnp.split(mc, N_MOD, axis=-1)

        hx = rmsnorm(x, norm1[l]) * (1 + x_sc1) + x_sh1
        hc = rmsnorm(ctx, norm1[l]) * (1 + c_sc1) + c_sh1
        na_k_c, na_v_c, g_k_c, g_v_c = kv_heads(hc @ w_in[l, :, OFF_KV:], gains)

        px = hx @ w_in[l]
        na_k, na_v, g_k, g_v = kv_heads(px[..., OFF_KV:], gains)
        na_q, g_q = q_heads(px, gains)
        g_q = apply_rope(g_q, cos, sin)
        g_k = apply_rope(g_k, cos, sin)
        y_pool = pool_mixer(px[..., OFF_POOL:OFF_NA_Q], pool_w[l], pool_scale[l])
        y_na = merge_heads(na_latent(na_q, na_k, na_v, na_k_c, na_v_c, na_rpb[l]))
        y_gqa = merge_heads(gqa_latent(g_q.reshape(b, GQA_KV_HEADS, GQA_GROUP, n, HEAD_DIM),
                                       g_k, g_v, g_k_c, g_v_c))
        x = x + x_g1 * merge_branches(px, y_pool, y_na, y_gqa, w_br[l], w_out[l])

        if not last:
            pc = hc @ w_in[l, :, :OFF_KV]
            nc = pc.shape[1]
            na_q_c, g_q_c = q_heads(pc, gains)
            yc_pool = pool_mixer(pc[..., OFF_POOL:OFF_NA_Q], pool_w[l], pool_scale[l])
            yc_na = merge_heads(sdpa(na_q_c[:, :, None], na_k_c, na_v_c)[:, :, 0])
            yc_gqa = merge_heads(sdpa(g_q_c.reshape(b, GQA_KV_HEADS, GQA_GROUP, nc, HEAD_DIM),
                                      g_k_c, g_v_c).reshape(b, GQA_Q_HEADS, nc, HEAD_DIM))
            ctx = ctx + c_g1 * merge_branches(pc, yc_pool, yc_na, yc_gqa, w_br[l], w_out[l])

            h2x = rmsnorm(x, norm2[l]) * (1 + x_sc2) + x_sh2
            h2c = rmsnorm(ctx, norm2[l]) * (1 + c_sc2) + c_sh2
            f = moe(jnp.concatenate([h2c, h2x], axis=1), w_router, b_router, w_gu[l], w_down[l])
            ctx = ctx + c_g2 * f[:, :nc]
            x = x + x_g2 * f[:, nc:]
        else:
            h2x = rmsnorm(x, norm2[l]) * (1 + x_sc2) + x_sh2
            x = x + x_g2 * moe(h2x, w_router, b_router, w_gu[l], w_down[l])
    return x
```

```python
import functools

import jax
import jax.numpy as jnp
from jax import lax
from jax.experimental import pallas as pl
from jax.experimental.pallas import tpu as pltpu

F32 = jnp.float32
BF16 = jnp.bfloat16

GRID_W = 64
HEAD_DIM = 128
ROPE_THETA = 10000.0
EPS = 1e-6
POOL_WINDOWS = (2, 4, 8, 16)
POOL_CH = 128
POOL_WIDTH = len(POOL_WINDOWS) * POOL_CH
NA_HEADS = 6
NA_WIDTH = NA_HEADS * HEAD_DIM
NA_ROWS = 8
NA_COLS = 16
GQA_Q_HEADS = 6
GQA_KV_HEADS = 2
GQA_GROUP = GQA_Q_HEADS // GQA_KV_HEADS
GQA_Q_WIDTH = GQA_Q_HEADS * HEAD_DIM
N_BRANCH = 3
N_EXPERTS = 16
N_GROUPS = 4
EXPERTS_PER_GROUP = N_EXPERTS // N_GROUPS
N_MOD = 6
ATTN_SCALE = HEAD_DIM ** -0.5

COL_BLOCK = 2048
CB_POOL = 3 * COL_BLOCK // POOL_WIDTH
HB_NA_Q = (3 * COL_BLOCK + POOL_WIDTH) // HEAD_DIM
HB_GQA_Q = HB_NA_Q + NA_HEADS
HB_NA_K = HB_GQA_Q + GQA_Q_HEADS
HB_NA_V = HB_NA_K + NA_HEADS
HB_GQA_K = HB_NA_V + NA_HEADS
HB_GQA_V = HB_GQA_K + GQA_KV_HEADS
HB_KV0 = HB_NA_K

V7X_VMEM_LIMIT = 56 * 1024 * 1024
NEG_BIG = -1e30


def _cparams(sem):
    return pltpu.CompilerParams(dimension_semantics=sem, vmem_limit_bytes=V7X_VMEM_LIMIT)


def _dot(a, b):
    return jnp.dot(a, b, preferred_element_type=F32)


def _dot_nt(a, b):
    return lax.dot_general(a, b, (((1,), (1,)), ((), ())), preferred_element_type=F32)


def _mod_kernel(ct_ref, w_ref, b_ref, o_ref, *, n_rows):
    ct = ct_ref[...]
    a = ct * jax.nn.sigmoid(ct)
    w = w_ref[0]
    rows = [jnp.sum(w * a[:, r:r + 1], axis=0, keepdims=True) for r in range(n_rows)]
    rows += [jnp.zeros_like(rows[0])] * (8 - n_rows)
    o_ref[0] = jnp.concatenate(rows, axis=0) + b_ref[0]


def _modulation(c_rows, w_mod, b_mod):
    depth, d, nm = w_mod.shape
    n_rows = c_rows.shape[0]
    ct = jnp.zeros((d, 8), F32).at[:, :n_rows].set(c_rows.T)
    tn = 1024
    return pl.pallas_call(
        functools.partial(_mod_kernel, n_rows=n_rows),
        grid=(depth, nm // tn),
        in_specs=[
            pl.BlockSpec((d, 8), lambda l, j: (0, 0)),
            pl.BlockSpec((1, d, tn), lambda l, j: (l, 0, j)),
            pl.BlockSpec((1, 1, tn), lambda l, j: (l, 0, j)),
        ],
        out_specs=pl.BlockSpec((1, 8, tn), lambda l, j: (l, 0, j)),
        out_shape=jax.ShapeDtypeStruct((depth, 8, nm), F32),
        compiler_params=_cparams(("arbitrary", "arbitrary")),
        name="adaln_mod",
    )(ct, w_mod, b_mod.reshape(depth, 1, nm))


def _head_norm(a, gain):
    return a * lax.rsqrt(jnp.mean(a * a, axis=-1, keepdims=True) + EPS) * gain


def _rope(y, cos, sin):
    return y * cos + pltpu.roll(y, HEAD_DIM // 2, 1) * sin


def _inproj_kernel(*refs, rope, j_off):
    if rope:
        x_ref, g_ref, sc_ref, sh_ref, w_ref, qg_ref, cos_ref, sin_ref, o_ref = refs
    else:
        x_ref, g_ref, sc_ref, sh_ref, w_ref, qg_ref, o_ref = refs
    j = pl.program_id(0) + j_off
    x = x_ref[0]
    h = x * lax.rsqrt(jnp.mean(x * x, axis=-1, keepdims=True) + EPS) * g_ref[...]
    h = h * (1.0 + sc_ref[0]) + sh_ref[0]
    acc = _dot(h.astype(BF16), w_ref[...])

    def head(c):
        return acc[:, c * HEAD_DIM:(c + 1) * HEAD_DIM]

    def put(c, y):
        o_ref[0, :, c * HEAD_DIM:(c + 1) * HEAD_DIM] = y.astype(BF16)

    @pl.when(j < 3)
    def _():
        o_ref[0] = jax.nn.sigmoid(acc).astype(BF16)

    @pl.when(j == 3)
    def _():
        npool = POOL_WIDTH // HEAD_DIM
        o_ref[0, :, :POOL_WIDTH] = acc[:, :POOL_WIDTH].astype(BF16)
        for c in range(npool, npool + NA_HEADS):
            put(c, _head_norm(head(c), qg_ref[0:1, :]) * ATTN_SCALE)
        for c in range(npool + NA_HEADS, npool + NA_HEADS + GQA_Q_HEADS):
            y = _head_norm(head(c), qg_ref[2:3, :])
            if rope:
                y = _rope(y, cos_ref[...], sin_ref[...])
            put(c, y * ATTN_SCALE)

    @pl.when(j == 4)
    def _():
        for c in range(NA_HEADS):
            put(c, _head_norm(head(c), qg_ref[1:2, :]))
        c0 = 2 * NA_HEADS
        o_ref[0, :, NA_WIDTH:c0 * HEAD_DIM] = acc[:, NA_WIDTH:c0 * HEAD_DIM].astype(BF16)
        for c in range(c0, c0 + GQA_KV_HEADS):
            y = _head_norm(head(c), qg_ref[3:4, :])
            if rope:
                y = _rope(y, cos_ref[...], sin_ref[...])
            put(c, y)
        c1 = c0 + GQA_KV_HEADS
        o_ref[0, :, c1 * HEAD_DIM:] = acc[:, c1 * HEAD_DIM:].astype(BF16)


def _inproj(x, gain, sc, sh, w, qgain, cos, sin, *, kv_only=False):
    b, n, d = x.shape
    rope = cos is not None
    tm = min(512, n)
    ncb = w.shape[1] // COL_BLOCK
    j_off = ncb - 1 if kv_only else 0
    nj = ncb - j_off
    in_specs = [
        pl.BlockSpec((1, tm, d), lambda j, bb, i: (bb, i, 0)),
        pl.BlockSpec((1, d), lambda j, bb, i: (0, 0)),
        pl.BlockSpec((1, 1, d), lambda j, bb, i: (bb, 0, 0)),
        pl.BlockSpec((1, 1, d), lambda j, bb, i: (bb, 0, 0)),
        pl.BlockSpec((d, COL_BLOCK), lambda j, bb, i: (0, j + j_off)),
        pl.BlockSpec((4, HEAD_DIM), lambda j, bb, i: (0, 0)),
    ]
    args = [x, gain.reshape(1, d), sc, sh, w, qgain]
    if rope:
        in_specs += [pl.BlockSpec((tm, HEAD_DIM), lambda j, bb, i: (i, 0))] * 2
        args += [cos, sin]
    return pl.pallas_call(
        functools.partial(_inproj_kernel, rope=rope, j_off=j_off),
        grid=(nj, b, n // tm),
        in_specs=in_specs,
        out_specs=pl.BlockSpec((1, tm, COL_BLOCK), lambda j, bb, i: (bb, i, j)),
        out_shape=jax.ShapeDtypeStruct((b, n, nj * COL_BLOCK), BF16),
        compiler_params=_cparams(("arbitrary", "arbitrary", "arbitrary")),
        name="inproj_rope" if rope else ("inproj_kv" if kv_only else "inproj_ctx"),
    )(*args)


POOL_HALO = 16


def _pool_kernel(prev_ref, cur_ref, next_ref, w_ref, s_ref, o_ref, buf_ref, *, tm, n):
    i = pl.program_id(1)
    nt = pl.num_programs(1)
    hl = POOL_HALO
    buf_ref[pl.ds(hl, tm), :] = cur_ref[0].astype(F32)
    buf_ref[pl.ds(0, hl), :] = jnp.where(i > 0, prev_ref[0].astype(F32), 0.0)
    buf_ref[pl.ds(hl + tm, hl), :] = jnp.where(i < nt - 1, next_ref[0].astype(F32), 0.0)
    t = i * tm + lax.broadcasted_iota(jnp.int32, (tm, 1), 0)
    for g, w in enumerate(POOL_WINDOWS):
        sl = slice(g * POOL_CH, (g + 1) * POOL_CH)
        acc = buf_ref[pl.ds(hl - w // 2, tm), sl]
        for off in range(-w // 2 + 1, w // 2):
            acc = acc + buf_ref[pl.ds(hl + off, tm), sl]
        cnt = (jnp.minimum(t + w // 2, n) - jnp.maximum(t - w // 2, 0)).astype(F32)
        dlt = acc / cnt - buf_ref[pl.ds(hl, tm), sl]
        y = _dot(dlt.astype(BF16), w_ref[g]) * s_ref[:, sl]
        o_ref[0, :, sl] = y.astype(BF16)


def _pool(p, pool_w, pool_scale):
    b, n, _ = p.shape
    tm = min(512, n)
    hl = POOL_HALO
    hb = tm // hl
    last = n // hl - 1
    return pl.pallas_call(
        functools.partial(_pool_kernel, tm=tm, n=n),
        grid=(b, n // tm),
        in_specs=[
            pl.BlockSpec((1, hl, POOL_WIDTH), lambda bb, i: (bb, jnp.maximum(i * hb - 1, 0), CB_POOL)),
            pl.BlockSpec((1, tm, POOL_WIDTH), lambda bb, i: (bb, i, CB_POOL)),
            pl.BlockSpec((1, hl, POOL_WIDTH), lambda bb, i: (bb, jnp.minimum((i + 1) * hb, last), CB_POOL)),
            pl.BlockSpec((len(POOL_WINDOWS), POOL_CH, POOL_CH), lambda bb, i: (0, 0, 0)),
            pl.BlockSpec((1, POOL_WIDTH), lambda bb, i: (0, 0)),
        ],
        out_specs=pl.BlockSpec((1, tm, POOL_WIDTH), lambda bb, i: (bb, i, 0)),
        out_shape=jax.ShapeDtypeStruct((b, n, POOL_WIDTH), BF16),
        scratch_shapes=[pltpu.VMEM((tm + 2 * hl, POOL_WIDTH), F32)],
        compiler_params=_cparams(("arbitrary", "arbitrary")),
        name="pool_mixer",
    )(p, p, p, pool_w, pool_scale.reshape(1, POOL_WIDTH))


def _na_bias(rpb, kr):
    col = jnp.arange(GRID_W)
    col_start = jnp.clip(col - NA_COLS // 2, 0, GRID_W - NA_COLS)
    dcol = col[None, :] - col[:, None] + (NA_COLS - 1)
    ok = (col[None, :] >= col_start[:, None]) & (col[None, :] < col_start[:, None] + NA_COLS)
    dcol = jnp.clip(dcol, 0, 2 * NA_COLS - 2)
    d = jnp.arange(kr)
    i = jnp.arange(kr)
    drow = i[None, :] - d[:, None] + (NA_ROWS - 1)
    bias = rpb[:, drow[:, :, None, None], dcol[None, None, :, :]]
    bias = jnp.where(ok[None, None, None], bias.astype(F32), NEG_BIG)
    h = rpb.shape[0]
    return bias.transpose(1, 0, 3, 2, 4).reshape(kr, h, GRID_W, kr * GRID_W)


def _na_kernel(q_ref, k_ref, v_ref, kc_ref, vc_ref, bias_ref, o_ref, *, rb, rows, kr):
    i = pl.program_id(2)
    kc = kc_ref[0]
    vc = vc_ref[0]
    w = GRID_W

    def body(rr, carry):
        r = i * rb + rr
        rs = jnp.clip(r - kr // 2, 0, rows - kr)
        q = q_ref[0, pl.ds(pl.multiple_of(rr * w, w), w), :]
        kw = k_ref[0, pl.ds(pl.multiple_of(rs * w, w), kr * w), :]
        vw = v_ref[0, pl.ds(pl.multiple_of(rs * w, w), kr * w), :]
        s_loc = _dot_nt(q, kw) + bias_ref[r - rs, 0]
        s_ctx = _dot_nt(q, kc)
        m = jnp.maximum(jnp.max(s_loc, axis=-1, keepdims=True), jnp.max(s_ctx, axis=-1, keepdims=True))
        p_loc = jnp.exp(s_loc - m)
        p_ctx = jnp.exp(s_ctx - m)
        den = jnp.sum(p_loc, axis=-1, keepdims=True) + jnp.sum(p_ctx, axis=-1, keepdims=True)
        o = _dot(p_loc.astype(BF16), vw) + _dot(p_ctx.astype(BF16), vc)
        o_ref[0, pl.ds(pl.multiple_of(rr * w, w), w), :] = (o / den).astype(BF16)
        return carry

    lax.fori_loop(0, rb, body, 0)


def _na(p, pc, kvc0, bias):
    b, n, _ = p.shape
    nc = pc.shape[1]
    rows = n // GRID_W
    kr = bias.shape[0]
    rb = min(16, rows)
    return pl.pallas_call(
        functools.partial(_na_kernel, rb=rb, rows=rows, kr=kr),
        grid=(b, NA_HEADS, rows // rb),
        in_specs=[
            pl.BlockSpec((1, rb * GRID_W, HEAD_DIM), lambda bb, h, i: (bb, i, HB_NA_Q + h)),
            pl.BlockSpec((1, n, HEAD_DIM), lambda bb, h, i: (bb, 0, HB_NA_K + h)),
            pl.BlockSpec((1, n, HEAD_DIM), lambda bb, h, i: (bb, 0, HB_NA_V + h)),
            pl.BlockSpec((1, nc, HEAD_DIM), lambda bb, h, i: (bb, 0, kvc0 + h)),
            pl.BlockSpec((1, nc, HEAD_DIM), lambda bb, h, i: (bb, 0, kvc0 + NA_HEADS + h)),
            pl.BlockSpec((kr, 1, GRID_W, kr * GRID_W), lambda bb, h, i: (0, h, 0, 0)),
        ],
        out_specs=pl.BlockSpec((1, rb * GRID_W, HEAD_DIM), lambda bb, h, i: (bb, i, h)),
        out_shape=jax.ShapeDtypeStruct((b, n, NA_WIDTH), BF16),
        compiler_params=_cparams(("arbitrary", "arbitrary", "arbitrary")),
        name="na_attn",
    )(p, p, p, pc, pc, bias)


def _gqa_kernel(q0_ref, q1_ref, q2_ref, k_ref, v_ref, kc_ref, vc_ref, o_ref, *, tk, n):
    q = jnp.concatenate([q0_ref[0], q1_ref[0], q2_ref[0]], axis=0)
    tq = q0_ref.shape[1]

    s = _dot_nt(q, kc_ref[0])
    m = jnp.max(s, axis=-1, keepdims=True)
    p = jnp.exp(s - m)
    den = jnp.sum(p, axis=-1, keepdims=True)
    acc = _dot(p.astype(BF16), vc_ref[0])

    def body(c, carry):
        m, den, acc = carry
        off = pl.multiple_of(c * tk, tk)
        s = _dot_nt(q, k_ref[0, pl.ds(off, tk), :])
        m_new = jnp.maximum(m, jnp.max(s, axis=-1, keepdims=True))
        alpha = jnp.exp(m - m_new)
        p = jnp.exp(s - m_new)
        den = alpha * den + jnp.sum(p, axis=-1, keepdims=True)
        acc = alpha * acc + _dot(p.astype(BF16), v_ref[0, pl.ds(off, tk), :])
        return m_new, den, acc

    m, den, acc = lax.fori_loop(0, n // tk, body, (m, den, acc))
    o = acc / den
    for g in range(GQA_GROUP):
        o_ref[0, :, g * HEAD_DIM:(g + 1) * HEAD_DIM] = o[g * tq:(g + 1) * tq].astype(BF16)


def _gqa(p, pc, kvc0):
    b, n, _ = p.shape
    nc = pc.shape[1]
    tq = min(256, n)
    tk = min(512, n)
    kc0 = kvc0 + 2 * NA_HEADS

    def qspec(g):
        return pl.BlockSpec((1, tq, HEAD_DIM), lambda bb, h, i: (bb, i, HB_GQA_Q + h * GQA_GROUP + g))

    return pl.pallas_call(
        functools.partial(_gqa_kernel, tk=tk, n=n),
        grid=(b, GQA_KV_HEADS, n // tq),
        in_specs=[
            qspec(0), qspec(1), qspec(2),
            pl.BlockSpec((1, n, HEAD_DIM), lambda bb, h, i: (bb, 0, HB_GQA_K + h)),
            pl.BlockSpec((1, n, HEAD_DIM), lambda bb, h, i: (bb, 0, HB_GQA_V + h)),
            pl.BlockSpec((1, nc, HEAD_DIM), lambda bb, h, i: (bb, 0, kc0 + h)),
            pl.BlockSpec((1, nc, HEAD_DIM), lambda bb, h, i: (bb, 0, kc0 + GQA_KV_HEADS + h)),
        ],
        out_specs=pl.BlockSpec((1, tq, GQA_GROUP * HEAD_DIM), lambda bb, h, i: (bb, i, h)),
        out_shape=jax.ShapeDtypeStruct((b, n, GQA_Q_WIDTH), BF16),
        compiler_params=_cparams(("arbitrary", "arbitrary", "arbitrary")),
        name="gqa_attn",
    )(p, p, p, p, p, pc, pc)


def _ctx_attn_kernel(q_ref, k_ref, v_ref, o_ref):
    s = _dot_nt(q_ref[0], k_ref[0])
    m = jnp.max(s, axis=-1, keepdims=True)
    p = jnp.exp(s - m)
    den = jnp.sum(p, axis=-1, keepdims=True)
    o_ref[0] = (_dot(p.astype(BF16), v_ref[0]) / den).astype(BF16)


def _ctx_attn(pc):
    b, nc, _ = pc.shape
    nh = NA_HEADS + GQA_Q_HEADS

    def kmap(bb, h):
        g = jnp.maximum(h - NA_HEADS, 0) // GQA_GROUP
        return bb, 0, jnp.where(h < NA_HEADS, HB_NA_K + h, HB_GQA_K + g)

    def vmap_(bb, h):
        g = jnp.maximum(h - NA_HEADS, 0) // GQA_GROUP
        return bb, 0, jnp.where(h < NA_HEADS, HB_NA_V + h, HB_GQA_V + g)

    return pl.pallas_call(
        _ctx_attn_kernel,
        grid=(b, nh),
        in_specs=[
            pl.BlockSpec((1, nc, HEAD_DIM), lambda bb, h: (bb, 0, HB_NA_Q + h)),
            pl.BlockSpec((1, nc, HEAD_DIM), kmap),
            pl.BlockSpec((1, nc, HEAD_DIM), vmap_),
        ],
        out_specs=pl.BlockSpec((1, nc, HEAD_DIM), lambda bb, h: (bb, 0, h)),
        out_shape=jax.ShapeDtypeStruct((b, nc, nh * HEAD_DIM), BF16),
        compiler_params=_cparams(("arbitrary", "arbitrary")),
        name="ctx_attn",
    )(pc, pc, pc)


def _merge_kernel(x_ref, ga_ref, gb_ref, gc_ref, yp_ref, yn_ref, yg_ref, wbr_ref, wout_ref, g1_ref, o_ref):
    r1 = POOL_WIDTH
    r2 = POOL_WIDTH + NA_WIDTH
    z = ga_ref[0].astype(F32) * _dot(yp_ref[0], wbr_ref[0:r1, :])
    z = z + gb_ref[0].astype(F32) * _dot(yn_ref[0], wbr_ref[r1:r2, :])
    z = z + gc_ref[0].astype(F32) * _dot(yg_ref[0], wbr_ref[r2:, :])
    o_ref[0] = x_ref[0] + g1_ref[0] * _dot(z.astype(BF16), wout_ref[...])


def _merge(x, p, y_pool, y_na, na_cb, y_gqa, gqa_cb, w_br, w_out, g1):
    b, n, d = x.shape
    tm = min(256, n)
    const = lambda bb, i: (0, 0)
    return pl.pallas_call(
        _merge_kernel,
        grid=(b, n // tm),
        in_specs=[
            pl.BlockSpec((1, tm, d), lambda bb, i: (bb, i, 0)),
            pl.BlockSpec((1, tm, d), lambda bb, i: (bb, i, 0)),
            pl.BlockSpec((1, tm, d), lambda bb, i: (bb, i, 1)),
            pl.BlockSpec((1, tm, d), lambda bb, i: (bb, i, 2)),
            pl.BlockSpec((1, tm, POOL_WIDTH), lambda bb, i: (bb, i, 0)),
            pl.BlockSpec((1, tm, NA_WIDTH), lambda bb, i: (bb, i, na_cb)),
            pl.BlockSpec((1, tm, GQA_Q_WIDTH), lambda bb, i: (bb, i, gqa_cb)),
            pl.BlockSpec(w_br.shape, const, pipeline_mode=pl.Buffered(1)),
            pl.BlockSpec(w_out.shape, const, pipeline_mode=pl.Buffered(1)),
            pl.BlockSpec((1, 1, d), lambda bb, i: (bb, 0, 0)),
        ],
        out_specs=pl.BlockSpec((1, tm, d), lambda bb, i: (bb, i, 0)),
        out_shape=jax.ShapeDtypeStruct((b, n, d), F32),
        compiler_params=_cparams(("arbitrary", "arbitrary")),
        name="branch_merge",
    )(x, p, p, p, y_pool, y_na, y_gqa, w_br, w_out, g1)


def _top2_of4(a, b, c, d):
    hi1, lo1 = jnp.maximum(a, b), jnp.minimum(a, b)
    hi2, lo2 = jnp.maximum(c, d), jnp.minimum(c, d)
    return jnp.maximum(hi1, hi2) + jnp.maximum(jnp.minimum(hi1, hi2), jnp.maximum(lo1, lo2))


def _router_kernel(x_ref, g_ref, sc_ref, sh_ref, whi_ref, wlo_ref, br_ref, h_ref, dw_ref):
    x = x_ref[0]
    h = x * lax.rsqrt(jnp.mean(x * x, axis=-1, keepdims=True) + EPS) * g_ref[...]
    h = h * (1.0 + sc_ref[0]) + sh_ref[0]
    h_hi = h.astype(BF16)
    h_ref[0] = h_hi
    h_lo = (h - h_hi.astype(F32)).astype(BF16)
    whi = whi_ref[...]
    logit = _dot_nt(whi, h_hi) + _dot_nt(whi, h_lo) + _dot_nt(wlo_ref[...], h_hi)
    s = jax.nn.sigmoid(logit)
    sel = s + br_ref[...]
    epg = EXPERTS_PER_GROUP
    row = lambda a, e: a[e:e + 1, :]
    gscore = [_top2_of4(*[row(sel, g * epg + j) for j in range(epg)]) for g in range(N_GROUPS)]
    g_best = jnp.zeros_like(gscore[0], dtype=jnp.int32)
    best = gscore[0]
    for g in range(1, N_GROUPS):
        upd = gscore[g] > best
        g_best = jnp.where(upd, g, g_best)
        best = jnp.where(upd, gscore[g], best)
    vs, ss = [], []
    for j in range(epg):
        v = row(sel, j)
        sv = row(s, j)
        for g in range(1, N_GROUPS):
            v = jnp.where(g_best == g, row(sel, g * epg + j), v)
            sv = jnp.where(g_best == g, row(s, g * epg + j), sv)
        vs.append(v)
        ss.append(sv)
    i1 = jnp.zeros_like(g_best)
    v1 = vs[0]
    for j in range(1, epg):
        upd = vs[j] > v1
        i1 = jnp.where(upd, j, i1)
        v1 = jnp.where(upd, vs[j], v1)
    i2 = jnp.full_like(g_best, -1)
    v2 = jnp.full_like(v1, -jnp.inf)
    for j in range(epg):
        upd = (i1 != j) & ((i2 < 0) | (vs[j] > v2))
        i2 = jnp.where(upd, j, i2)
        v2 = jnp.where(upd, vs[j], v2)
    w1 = sum(jnp.where(i1 == j, ss[j], 0.0) for j in range(epg))
    w2 = sum(jnp.where(i2 == j, ss[j], 0.0) for j in range(epg))
    tot = w1 + w2
    w1 = w1 / tot
    w2 = w2 / tot
    rows_out = []
    for e in range(N_EXPERTS):
        g, j = divmod(e, epg)
        we = jnp.where(i1 == j, w1, 0.0) + jnp.where(i2 == j, w2, 0.0)
        rows_out.append(jnp.where(g_best == g, we, 0.0))
    dw_ref[0] = jnp.concatenate(rows_out, axis=0)


def _router(x, gain, sc, sh, wr_hi, wr_lo, b_router):
    b, n, d = x.shape
    tm = min(512, n)
    ne = wr_hi.shape[0]
    return pl.pallas_call(
        _router_kernel,
        grid=(b, n // tm),
        in_specs=[
            pl.BlockSpec((1, tm, d), lambda bb, i: (bb, i, 0)),
            pl.BlockSpec((1, d), lambda bb, i: (0, 0)),
            pl.BlockSpec((1, 1, d), lambda bb, i: (bb, 0, 0)),
            pl.BlockSpec((1, 1, d), lambda bb, i: (bb, 0, 0)),
            pl.BlockSpec((ne, d), lambda bb, i: (0, 0)),
            pl.BlockSpec((ne, d), lambda bb, i: (0, 0)),
            pl.BlockSpec((ne, 1), lambda bb, i: (0, 0)),
        ],
        out_specs=[
            pl.BlockSpec((1, tm, d), lambda bb, i: (bb, i, 0)),
            pl.BlockSpec((1, ne, tm), lambda bb, i: (bb, 0, i)),
        ],
        out_shape=[jax.ShapeDtypeStruct((b, n, d), BF16), jax.ShapeDtypeStruct((b, ne, n), F32)],
        compiler_params=_cparams(("arbitrary", "arbitrary")),
        name="norm_router",
    )(x, gain.reshape(1, d), sc, sh, wr_hi, wr_lo, b_router.reshape(ne, 1))


def _moe_kernel(h_ref, dw_ref, wg_ref, wu_ref, wd_ref, x_ref, g2_ref, o_ref):
    e = pl.program_id(2)
    f = pl.program_id(3)

    @pl.when((e == 0) & (f == 0))
    def _():
        o_ref[0] = x_ref[0]

    h = h_ref[0]
    gate = _dot(h, wg_ref[0])
    up = _dot(h, wu_ref[0])
    a = (gate * jax.nn.sigmoid(gate) * up).astype(BF16)
    y = _dot(a, wd_ref[0])
    dw = dw_ref[0]
    lane = lax.broadcasted_iota(jnp.int32, dw.shape, 1)
    we = jnp.sum(jnp.where(lane == e, dw, 0.0), axis=1, keepdims=True)
    o_ref[0] += (we * g2_ref[0]) * y


def _moe_dense(h, dw, w_gu, w_down, x, g2):
    b, n, d = x.shape
    ne, _, f2 = w_gu.shape
    ff = f2 // 2
    fs = 512
    nf = ff // fs
    tm = min(512, n)
    return pl.pallas_call(
        _moe_kernel,
        grid=(b, n // tm, ne, nf),
        in_specs=[
            pl.BlockSpec((1, tm, d), lambda bb, i, e, f: (bb, i, 0)),
            pl.BlockSpec((1, tm, ne), lambda bb, i, e, f: (bb, i, 0)),
            pl.BlockSpec((1, d, fs), lambda bb, i, e, f: (e, 0, f)),
            pl.BlockSpec((1, d, fs), lambda bb, i, e, f: (e, 0, nf + f)),
            pl.BlockSpec((1, fs, d), lambda bb, i, e, f: (e, f, 0)),
            pl.BlockSpec((1, tm, d), lambda bb, i, e, f: (bb, i, 0)),
            pl.BlockSpec((1, 1, d), lambda bb, i, e, f: (bb, 0, 0)),
        ],
        out_specs=pl.BlockSpec((1, tm, d), lambda bb, i, e, f: (bb, i, 0)),
        out_shape=jax.ShapeDtypeStruct((b, n, d), F32),
        compiler_params=_cparams(("arbitrary", "arbitrary", "arbitrary", "arbitrary")),
        name="moe_experts",
    )(h, dw, w_gu, w_gu, w_down, x, g2)


def _rope_tables(n):
    t = jnp.arange(n, dtype=jnp.int32)
    row = (t // GRID_W).astype(F32)
    col = (t % GRID_W).astype(F32)
    axis_dim = HEAD_DIM // 2
    inv = ROPE_THETA ** (-jnp.arange(0, axis_dim, 2, dtype=F32) / axis_dim)
    ang = jnp.concatenate([row[:, None] * inv, col[:, None] * inv], axis=-1)
    cos, sin = jnp.cos(ang), jnp.sin(ang)
    return jnp.concatenate([cos, cos], axis=-1), jnp.concatenate([-sin, sin], axis=-1)


def kernel(x, c, ctx, c_ctx, w_mod, b_mod, norm1, norm2, w_in, qk_gain, pool_w, pool_scale,
           na_rpb, w_br, w_out, w_router, b_router, w_gu, w_down):
    b, n, d = x.shape
    depth = w_mod.shape[0]
    rows = n // GRID_W
    kr = min(NA_ROWS, rows)
    assert n % GRID_W == 0 and rows % kr == 0 and b + 1 <= 8

    cos, sin = _rope_tables(n)
    mods = _modulation(jnp.concatenate([c, c_ctx[None, :]], axis=0), w_mod, b_mod)
    wr_t = w_router.T
    wr_hi = wr_t.astype(BF16)
    wr_lo = (wr_t - wr_hi.astype(F32)).astype(BF16)

    for l in range(depth):
        last = l == depth - 1
        mx = mods[l, :b].reshape(b, 1, N_MOD, d)
        mc = jnp.broadcast_to(mods[l, b].reshape(1, 1, N_MOD, d), (b, 1, N_MOD, d))
        x_sh1, x_sc1, x_g1, x_sh2, x_sc2, x_g2 = [mx[:, :, k] for k in range(N_MOD)]
        c_sh1, c_sc1, c_g1, c_sh2, c_sc2, c_g2 = [mc[:, :, k] for k in range(N_MOD)]
        w_in_l = w_in[l].astype(BF16)
        w_br_l = w_br[l].astype(BF16)
        w_out_l = w_out[l].astype(BF16)
        pool_w_l = pool_w[l].astype(BF16)
        bias = _na_bias(na_rpb[l], kr)

        pc = _inproj(ctx, norm1[l], c_sc1, c_sh1, w_in_l, qk_gain[l], None, None, kv_only=last)
        kvc0 = 0 if last else HB_KV0
        px = _inproj(x, norm1[l], x_sc1, x_sh1, w_in_l, qk_gain[l], cos, sin)
        y_pool = _pool(px, pool_w_l, pool_scale[l])
        y_na = _na(px, pc, kvc0, bias)
        y_gqa = _gqa(px, pc, kvc0)
        x = _merge(x, px, y_pool, y_na, 0, y_gqa, 0, w_br_l, w_out_l, x_g1)

        w_gu_l = w_gu[l].astype(BF16)
        w_down_l = w_down[l].astype(BF16)
        if not last:
            yc_pool = _pool(pc, pool_w_l, pool_scale[l])
            yc = _ctx_attn(pc)
            ctx = _merge(ctx, pc, yc_pool, yc, 0, yc, 1, w_br_l, w_out_l, c_g1)
            hc, dwc = _router(ctx, norm2[l], c_sc2, c_sh2, wr_hi, wr_lo, b_router)
            ctx = _moe_dense(hc, dwc.transpose(0, 2, 1), w_gu_l, w_down_l, ctx, c_g2)
        hx, dwx = _router(x, norm2[l], x_sc2, x_sh2, wr_hi, wr_lo, b_router)
        x = _moe_dense(hx, dwx.transpose(0, 2, 1), w_gu_l, w_down_l, x, x_g2)
    return x
```

```python
import functools

import numpy as np
import jax
import jax.numpy as jnp
from jax import lax
from jax.experimental import pallas as pl
from jax.experimental.pallas import tpu as pltpu

F32 = jnp.float32
BF16 = jnp.bfloat16

GRID_W = 64
HEAD_DIM = 128
ROPE_THETA = 10000.0
EPS = 1e-6
POOL_WINDOWS = (2, 4, 8, 16)
POOL_CH = 128
POOL_WIDTH = len(POOL_WINDOWS) * POOL_CH
NA_HEADS = 6
NA_WIDTH = NA_HEADS * HEAD_DIM
NA_ROWS = 8
NA_COLS = 16
GQA_Q_HEADS = 6
GQA_KV_HEADS = 2
GQA_GROUP = GQA_Q_HEADS // GQA_KV_HEADS
GQA_Q_WIDTH = GQA_Q_HEADS * HEAD_DIM
N_BRANCH = 3
N_EXPERTS = 16
N_GROUPS = 4
EXPERTS_PER_GROUP = N_EXPERTS // N_GROUPS
N_MOD = 6
ATTN_SCALE = HEAD_DIM ** -0.5

COL_BLOCK = 2048
CB_POOL = 3 * COL_BLOCK // POOL_WIDTH
HB_NA_Q = (3 * COL_BLOCK + POOL_WIDTH) // HEAD_DIM
HB_GQA_Q = HB_NA_Q + NA_HEADS
HB_NA_K = HB_GQA_Q + GQA_Q_HEADS
HB_NA_V = HB_NA_K + NA_HEADS
HB_GQA_K = HB_NA_V + NA_HEADS
HB_GQA_V = HB_GQA_K + GQA_KV_HEADS
HB_KV0 = HB_NA_K

V7X_VMEM_LIMIT = 56 * 1024 * 1024
NEG_BIG = -1e30


def _cparams(sem):
    return pltpu.CompilerParams(dimension_semantics=sem, vmem_limit_bytes=V7X_VMEM_LIMIT)


def _dot(a, b):
    return jnp.dot(a, b, preferred_element_type=F32)


def _dot_nt(a, b):
    return lax.dot_general(a, b, (((1,), (1,)), ((), ())), preferred_element_type=F32)


def _mod_kernel(ct_ref, w_ref, b_ref, o_ref, *, n_rows):
    ct = ct_ref[...]
    a = ct * jax.nn.sigmoid(ct)
    w = w_ref[0]
    rows = [jnp.sum(w * a[:, r:r + 1], axis=0, keepdims=True) for r in range(n_rows)]
    rows += [jnp.zeros_like(rows[0])] * (8 - n_rows)
    o_ref[0] = jnp.concatenate(rows, axis=0) + b_ref[0]


def _modulation(c_rows, w_mod, b_mod):
    depth, d, nm = w_mod.shape
    n_rows = c_rows.shape[0]
    ct = jnp.zeros((d, 8), F32).at[:, :n_rows].set(c_rows.T)
    tn = 1024
    return pl.pallas_call(
        functools.partial(_mod_kernel, n_rows=n_rows),
        grid=(depth, nm // tn),
        in_specs=[
            pl.BlockSpec((d, 8), lambda l, j: (0, 0)),
            pl.BlockSpec((1, d, tn), lambda l, j: (l, 0, j)),
            pl.BlockSpec((1, 1, tn), lambda l, j: (l, 0, j)),
        ],
        out_specs=pl.BlockSpec((1, 8, tn), lambda l, j: (l, 0, j)),
        out_shape=jax.ShapeDtypeStruct((depth, 8, nm), F32),
        compiler_params=_cparams(("arbitrary", "arbitrary")),
        name="adaln_mod",
    )(ct, w_mod, b_mod.reshape(depth, 1, nm))


def _head_norm(a, gain):
    return a * lax.rsqrt(jnp.mean(a * a, axis=-1, keepdims=True) + EPS) * gain


def _rope(y, cos, sin):
    return y * cos + pltpu.roll(y, HEAD_DIM // 2, 1) * sin


def _inproj_kernel(*refs, rope, j_off):
    if rope:
        x_ref, g_ref, sc_ref, sh_ref, w_ref, qg_ref, cos_ref, sin_ref, o_ref = refs
    else:
        x_ref, g_ref, sc_ref, sh_ref, w_ref, qg_ref, o_ref = refs
    j = pl.program_id(0) + j_off
    x = x_ref[0]
    h = x * lax.rsqrt(jnp.mean(x * x, axis=-1, keepdims=True) + EPS) * g_ref[...]
    h = h * (1.0 + sc_ref[0]) + sh_ref[0]
    acc = _dot(h.astype(BF16), w_ref[...])

    def head(c):
        return acc[:, c * HEAD_DIM:(c + 1) * HEAD_DIM]

    def put(c, y):
        o_ref[0, :, c * HEAD_DIM:(c + 1) * HEAD_DIM] = y.astype(BF16)

    @pl.when(j < 3)
    def _():
        o_ref[0] = jax.nn.sigmoid(acc).astype(BF16)

    @pl.when(j == 3)
    def _():
        npool = POOL_WIDTH // HEAD_DIM
        o_ref[0, :, :POOL_WIDTH] = acc[:, :POOL_WIDTH].astype(BF16)
        for c in range(npool, npool + NA_HEADS):
            put(c, _head_norm(head(c), qg_ref[0:1, :]) * ATTN_SCALE)
        for c in range(npool + NA_HEADS, npool + NA_HEADS + GQA_Q_HEADS):
            y = _head_norm(head(c), qg_ref[2:3, :])
            if rope:
                y = _rope(y, cos_ref[...], sin_ref[...])
            put(c, y * ATTN_SCALE)

    @pl.when(j == 4)
    def _():
        for c in range(NA_HEADS):
            put(c, _head_norm(head(c), qg_ref[1:2, :]))
        c0 = 2 * NA_HEADS
        o_ref[0, :, NA_WIDTH:c0 * HEAD_DIM] = acc[:, NA_WIDTH:c0 * HEAD_DIM].astype(BF16)
        for c in range(c0, c0 + GQA_KV_HEADS):
            y = _head_norm(head(c), qg_ref[3:4, :])
            if rope:
                y = _rope(y, cos_ref[...], sin_ref[...])
            put(c, y)
        c1 = c0 + GQA_KV_HEADS
        o_ref[0, :, c1 * HEAD_DIM:] = acc[:, c1 * HEAD_DIM:].astype(BF16)


def _inproj(x, gain, sc, sh, w, qgain, cos, sin, *, kv_only=False):
    b, n, d = x.shape
    rope = cos is not None
    tm = min(512, n)
    ncb = w.shape[1] // COL_BLOCK
    j_off = ncb - 1 if kv_only else 0
    nj = ncb - j_off
    in_specs = [
        pl.BlockSpec((1, tm, d), lambda j, bb, i: (bb, i, 0)),
        pl.BlockSpec((1, d), lambda j, bb, i: (0, 0)),
        pl.BlockSpec((1, 1, d), lambda j, bb, i: (bb, 0, 0)),
        pl.BlockSpec((1, 1, d), lambda j, bb, i: (bb, 0, 0)),
        pl.BlockSpec((d, COL_BLOCK), lambda j, bb, i: (0, j + j_off)),
        pl.BlockSpec((4, HEAD_DIM), lambda j, bb, i: (0, 0)),
    ]
    args = [x, gain.reshape(1, d), sc, sh, w, qgain]
    if rope:
        in_specs += [pl.BlockSpec((tm, HEAD_DIM), lambda j, bb, i: (i, 0))] * 2
        args += [cos, sin]
    return pl.pallas_call(
        functools.partial(_inproj_kernel, rope=rope, j_off=j_off),
        grid=(nj, b, n // tm),
        in_specs=in_specs,
        out_specs=pl.BlockSpec((1, tm, COL_BLOCK), lambda j, bb, i: (bb, i, j)),
        out_shape=jax.ShapeDtypeStruct((b, n, nj * COL_BLOCK), BF16),
        compiler_params=_cparams(("arbitrary", "arbitrary", "arbitrary")),
        name="inproj_rope" if rope else ("inproj_kv" if kv_only else "inproj_ctx"),
    )(*args)


POOL_HALO = 16


def _pool_kernel(prev_ref, cur_ref, next_ref, w_ref, s_ref, o_ref, buf_ref, *, tm, n):
    i = pl.program_id(1)
    nt = pl.num_programs(1)
    hl = POOL_HALO
    buf_ref[pl.ds(hl, tm), :] = cur_ref[0].astype(F32)
    buf_ref[pl.ds(0, hl), :] = jnp.where(i > 0, prev_ref[0].astype(F32), 0.0)
    buf_ref[pl.ds(hl + tm, hl), :] = jnp.where(i < nt - 1, next_ref[0].astype(F32), 0.0)
    t = i * tm + lax.broadcasted_iota(jnp.int32, (tm, 1), 0)
    for g, w in enumerate(POOL_WINDOWS):
        sl = slice(g * POOL_CH, (g + 1) * POOL_CH)
        acc = buf_ref[pl.ds(hl - w // 2, tm), sl]
        for off in range(-w // 2 + 1, w // 2):
            acc = acc + buf_ref[pl.ds(hl + off, tm), sl]
        cnt = (jnp.minimum(t + w // 2, n) - jnp.maximum(t - w // 2, 0)).astype(F32)
        dlt = acc / cnt - buf_ref[pl.ds(hl, tm), sl]
        y = _dot(dlt.astype(BF16), w_ref[g]) * s_ref[:, sl]
        o_ref[0, :, sl] = y.astype(BF16)


def _pool(p, pool_w, pool_scale):
    b, n, _ = p.shape
    tm = min(512, n)
    hl = POOL_HALO
    hb = tm // hl
    last = n // hl - 1
    return pl.pallas_call(
        functools.partial(_pool_kernel, tm=tm, n=n),
        grid=(b, n // tm),
        in_specs=[
            pl.BlockSpec((1, hl, POOL_WIDTH), lambda bb, i: (bb, jnp.maximum(i * hb - 1, 0), CB_POOL)),
            pl.BlockSpec((1, tm, POOL_WIDTH), lambda bb, i: (bb, i, CB_POOL)),
            pl.BlockSpec((1, hl, POOL_WIDTH), lambda bb, i: (bb, jnp.minimum((i + 1) * hb, last), CB_POOL)),
            pl.BlockSpec((len(POOL_WINDOWS), POOL_CH, POOL_CH), lambda bb, i: (0, 0, 0)),
            pl.BlockSpec((1, POOL_WIDTH), lambda bb, i: (0, 0)),
        ],
        out_specs=pl.BlockSpec((1, tm, POOL_WIDTH), lambda bb, i: (bb, i, 0)),
        out_shape=jax.ShapeDtypeStruct((b, n, POOL_WIDTH), BF16),
        scratch_shapes=[pltpu.VMEM((tm + 2 * hl, POOL_WIDTH), F32)],
        compiler_params=_cparams(("arbitrary", "arbitrary")),
        name="pool_mixer",
    )(p, p, p, pool_w, pool_scale.reshape(1, POOL_WIDTH))


def _na_bias(rpb, kr):
    col = np.arange(GRID_W)
    col_start = np.clip(col - NA_COLS // 2, 0, GRID_W - NA_COLS)
    dcol = col[None, :] - col[:, None] + (NA_COLS - 1)
    ok = (col[None, :] >= col_start[:, None]) & (col[None, :] < col_start[:, None] + NA_COLS)
    onehot = ((dcol[:, :, None] == np.arange(2 * NA_COLS - 1)) & ok[:, :, None]).astype(np.float32)
    colbias = jnp.einsum("hrc,qkc->hrqk", rpb.astype(F32), onehot, precision=lax.Precision.HIGHEST)
    colbias = jnp.where(ok[None, None], colbias, NEG_BIG)
    per_d = [colbias[:, NA_ROWS - 1 - d:NA_ROWS - 1 - d + kr] for d in range(kr)]
    bias = jnp.stack(per_d)
    h = rpb.shape[0]
    return bias.transpose(0, 1, 3, 2, 4).reshape(kr, h, GRID_W, kr * GRID_W)


NA_UNROLL = 4


def _na_kernel(q_ref, k_ref, v_ref, kc_ref, vc_ref, bias_ref, o_ref, *, rb, rows, kr):
    i = pl.program_id(2)
    kc = kc_ref[0]
    vc = vc_ref[0]
    w = GRID_W

    def body(rr, carry):
        r = i * rb + rr
        rs = jnp.clip(r - kr // 2, 0, rows - kr)
        q = q_ref[0, pl.ds(pl.multiple_of(rr * w, w), w), :]
        kw = k_ref[0, pl.ds(pl.multiple_of(rs * w, w), kr * w), :]
        vw = v_ref[0, pl.ds(pl.multiple_of(rs * w, w), kr * w), :]
        s_loc = _dot_nt(q, kw) + bias_ref[r - rs, 0]
        s_ctx = _dot_nt(q, kc)
        m = jnp.maximum(jnp.max(s_loc, axis=-1, keepdims=True), jnp.max(s_ctx, axis=-1, keepdims=True))
        p_loc = jnp.exp(s_loc - m)
        p_ctx = jnp.exp(s_ctx - m)
        den = jnp.sum(p_loc, axis=-1, keepdims=True) + jnp.sum(p_ctx, axis=-1, keepdims=True)
        o = _dot(p_loc.astype(BF16), vw) + _dot(p_ctx.astype(BF16), vc)
        o_ref[0, pl.ds(pl.multiple_of(rr * w, w), w), :] = (o / den).astype(BF16)
        return carry

    lax.fori_loop(0, rb, body, 0, unroll=NA_UNROLL)


def _na(p, pc, kvc0, bias):
    b, n, _ = p.shape
    nc = pc.shape[1]
    rows = n // GRID_W
    kr = bias.shape[0]
    rb = min(16, rows)
    return pl.pallas_call(
        functools.partial(_na_kernel, rb=rb, rows=rows, kr=kr),
        grid=(b, NA_HEADS, rows // rb),
        in_specs=[
            pl.BlockSpec((1, rb * GRID_W, HEAD_DIM), lambda bb, h, i: (bb, i, HB_NA_Q + h)),
            pl.BlockSpec((1, n, HEAD_DIM), lambda bb, h, i: (bb, 0, HB_NA_K + h)),
            pl.BlockSpec((1, n, HEAD_DIM), lambda bb, h, i: (bb, 0, HB_NA_V + h)),
            pl.BlockSpec((1, nc, HEAD_DIM), lambda bb, h, i: (bb, 0, kvc0 + h)),
            pl.BlockSpec((1, nc, HEAD_DIM), lambda bb, h, i: (bb, 0, kvc0 + NA_HEADS + h)),
            pl.BlockSpec((kr, 1, GRID_W, kr * GRID_W), lambda bb, h, i: (0, h, 0, 0)),
        ],
        out_specs=pl.BlockSpec((1, rb * GRID_W, HEAD_DIM), lambda bb, h, i: (bb, i, h)),
        out_shape=jax.ShapeDtypeStruct((b, n, NA_WIDTH), BF16),
        compiler_params=_cparams(("arbitrary", "arbitrary", "arbitrary")),
        name="na_attn",
    )(p, p, p, pc, pc, bias)


GQA_ONES_ROWS = 16


def _gqa_kernel(q0_ref, q1_ref, q2_ref, k_ref, v_ref, kc_ref, vc_ref, o_ref,
                qt_s, vt_s, vct_s, s_s, sc_s, m_s, acc_s, *, tk, n):
    i = pl.program_id(2)
    tq = q0_ref.shape[1]
    nchunk = n // tk
    hd = HEAD_DIM

    def to_t(a):
        return a.astype(F32).T.astype(BF16)

    @pl.when(i == 0)
    def _():
        def tr(c, carry):
            off = pl.multiple_of(c * tk, tk)
            vt_s[c, 0:hd, :] = to_t(v_ref[0, pl.ds(off, tk), :])
            vt_s[c, hd:, :] = jnp.ones((GQA_ONES_ROWS, tk), BF16)
            return carry
        lax.fori_loop(0, nchunk, tr, 0)
        vct_s[0:hd, :] = to_t(vc_ref[0])
        vct_s[hd:, :] = jnp.ones((GQA_ONES_ROWS, vct_s.shape[1]), BF16)

    for g, qr in enumerate((q0_ref, q1_ref, q2_ref)):
        qt_s[:, g * tq:(g + 1) * tq] = to_t(qr[0])

    def scores(c, slot):
        off = pl.multiple_of(c * tk, tk)
        s_s[slot] = _dot(k_ref[0, pl.ds(off, tk), :], qt_s[...])

    def update(s, vt, first=False):
        smax = jnp.max(s, axis=0, keepdims=True)
        if first:
            m_new = smax
        else:
            m_prev = m_s[...]
            m_new = jnp.maximum(m_prev, smax)
            alpha = jnp.exp(m_prev - m_new)
        p = jnp.exp(s - m_new).astype(BF16)
        pv = _dot(vt, p)
        acc_s[...] = pv if first else alpha * acc_s[...] + pv
        m_s[...] = m_new

    scores(0, 0)
    sc_s[...] = _dot(kc_ref[0], qt_s[...])
    update(sc_s[...], vct_s[...], first=True)

    def body(c2, carry):
        c = 2 * c2
        scores(c + 1, 1)
        update(s_s[0], vt_s[c])
        scores(jnp.minimum(c + 2, nchunk - 1), 0)
        update(s_s[1], vt_s[c + 1])
        return carry

    lax.fori_loop(0, nchunk // 2, body, 0)
    acc = acc_s[...]
    o = (acc[0:hd] / acc[hd:hd + 1]).T
    for g in range(GQA_GROUP):
        o_ref[0, :, g * hd:(g + 1) * hd] = o[g * tq:(g + 1) * tq].astype(BF16)


def _gqa(p, pc, kvc0):
    b, n, _ = p.shape
    nc = pc.shape[1]
    tq = min(256, n)
    tk = min(512, n // 2)
    assert n % (2 * tk) == 0
    kc0 = kvc0 + 2 * NA_HEADS
    nq = GQA_GROUP * tq
    vr = HEAD_DIM + GQA_ONES_ROWS

    def qspec(g):
        return pl.BlockSpec((1, tq, HEAD_DIM), lambda bb, h, i: (bb, i, HB_GQA_Q + h * GQA_GROUP + g))

    return pl.pallas_call(
        functools.partial(_gqa_kernel, tk=tk, n=n),
        grid=(b, GQA_KV_HEADS, n // tq),
        in_specs=[
            qspec(0), qspec(1), qspec(2),
            pl.BlockSpec((1, n, HEAD_DIM), lambda bb, h, i: (bb, 0, HB_GQA_K + h)),
            pl.BlockSpec((1, n, HEAD_DIM), lambda bb, h, i: (bb, 0, HB_GQA_V + h)),
            pl.BlockSpec((1, nc, HEAD_DIM), lambda bb, h, i: (bb, 0, kc0 + h)),
            pl.BlockSpec((1, nc, HEAD_DIM), lambda bb, h, i: (bb, 0, kc0 + GQA_KV_HEADS + h)),
        ],
        out_specs=pl.BlockSpec((1, tq, GQA_GROUP * HEAD_DIM), lambda bb, h, i: (bb, i, h)),
        out_shape=jax.ShapeDtypeStruct((b, n, GQA_Q_WIDTH), BF16),
        scratch_shapes=[
            pltpu.VMEM((HEAD_DIM, nq), BF16),
            pltpu.VMEM((n // tk, vr, tk), BF16),
            pltpu.VMEM((vr, nc), BF16),
            pltpu.VMEM((2, tk, nq), F32),
            pltpu.VMEM((nc, nq), F32),
            pltpu.VMEM((1, nq), F32),
            pltpu.VMEM((vr, nq), F32),
        ],
        compiler_params=_cparams(("arbitrary", "arbitrary", "arbitrary")),
        name="gqa_attn",
    )(p, p, p, p, p, pc, pc)


def _ctx_attn_kernel(q_ref, k_ref, v_ref, o_ref):
    s = _dot_nt(q_ref[0], k_ref[0])
    m = jnp.max(s, axis=-1, keepdims=True)
    p = jnp.exp(s - m)
    den = jnp.sum(p, axis=-1, keepdims=True)
    o_ref[0] = (_dot(p.astype(BF16), v_ref[0]) / den).astype(BF16)


def _ctx_attn(pc):
    b, nc, _ = pc.shape
    nh = NA_HEADS + GQA_Q_HEADS

    def kmap(bb, h):
        g = jnp.maximum(h - NA_HEADS, 0) // GQA_GROUP
        return bb, 0, jnp.where(h < NA_HEADS, HB_NA_K + h, HB_GQA_K + g)

    def vmap_(bb, h):
        g = jnp.maximum(h - NA_HEADS, 0) // GQA_GROUP
        return bb, 0, jnp.where(h < NA_HEADS, HB_NA_V + h, HB_GQA_V + g)

    return pl.pallas_call(
        _ctx_attn_kernel,
        grid=(b, nh),
        in_specs=[
            pl.BlockSpec((1, nc, HEAD_DIM), lambda bb, h: (bb, 0, HB_NA_Q + h)),
            pl.BlockSpec((1, nc, HEAD_DIM), kmap),
            pl.BlockSpec((1, nc, HEAD_DIM), vmap_),
        ],
        out_specs=pl.BlockSpec((1, nc, HEAD_DIM), lambda bb, h: (bb, 0, h)),
        out_shape=jax.ShapeDtypeStruct((b, nc, nh * HEAD_DIM), BF16),
        compiler_params=_cparams(("arbitrary", "arbitrary")),
        name="ctx_attn",
    )(pc, pc, pc)


def _merge_kernel(x_ref, ga_ref, gb_ref, gc_ref, yp_ref, yn_ref, yg_ref, wbr_ref, wout_ref, g1_ref, o_ref):
    r1 = POOL_WIDTH
    r2 = POOL_WIDTH + NA_WIDTH
    z = ga_ref[0].astype(F32) * _dot(yp_ref[0], wbr_ref[0:r1, :])
    z = z + gb_ref[0].astype(F32) * _dot(yn_ref[0], wbr_ref[r1:r2, :])
    z = z + gc_ref[0].astype(F32) * _dot(yg_ref[0], wbr_ref[r2:, :])
    o_ref[0] = x_ref[0] + g1_ref[0] * _dot(z.astype(BF16), wout_ref[...])


def _merge(x, p, y_pool, y_na, na_cb, y_gqa, gqa_cb, w_br, w_out, g1):
    b, n, d = x.shape
    tm = min(256, n)
    const = lambda bb, i: (0, 0)
    return pl.pallas_call(
        _merge_kernel,
        grid=(b, n // tm),
        in_specs=[
            pl.BlockSpec((1, tm, d), lambda bb, i: (bb, i, 0)),
            pl.BlockSpec((1, tm, d), lambda bb, i: (bb, i, 0)),
            pl.BlockSpec((1, tm, d), lambda bb, i: (bb, i, 1)),
            pl.BlockSpec((1, tm, d), lambda bb, i: (bb, i, 2)),
            pl.BlockSpec((1, tm, POOL_WIDTH), lambda bb, i: (bb, i, 0)),
            pl.BlockSpec((1, tm, NA_WIDTH), lambda bb, i: (bb, i, na_cb)),
            pl.BlockSpec((1, tm, GQA_Q_WIDTH), lambda bb, i: (bb, i, gqa_cb)),
            pl.BlockSpec(w_br.shape, const, pipeline_mode=pl.Buffered(1)),
            pl.BlockSpec(w_out.shape, const, pipeline_mode=pl.Buffered(1)),
            pl.BlockSpec((1, 1, d), lambda bb, i: (bb, 0, 0)),
        ],
        out_specs=pl.BlockSpec((1, tm, d), lambda bb, i: (bb, i, 0)),
        out_shape=jax.ShapeDtypeStruct((b, n, d), F32),
        compiler_params=_cparams(("arbitrary", "arbitrary")),
        name="branch_merge",
    )(x, p, p, p, y_pool, y_na, y_gqa, w_br, w_out, g1)


def _top2_of4(a, b, c, d):
    hi1, lo1 = jnp.maximum(a, b), jnp.minimum(a, b)
    hi2, lo2 = jnp.maximum(c, d), jnp.minimum(c, d)
    return jnp.maximum(hi1, hi2) + jnp.maximum(jnp.minimum(hi1, hi2), jnp.maximum(lo1, lo2))


def _router_kernel(x_ref, g_ref, sc_ref, sh_ref, whi_ref, wlo_ref, br_ref, h_ref, dw_ref):
    x = x_ref[0]
    h = x * lax.rsqrt(jnp.mean(x * x, axis=-1, keepdims=True) + EPS) * g_ref[...]
    h = h * (1.0 + sc_ref[0]) + sh_ref[0]
    h_hi = h.astype(BF16)
    h_ref[0] = h_hi
    h_lo = (h - h_hi.astype(F32)).astype(BF16)
    whi = whi_ref[...]
    logit = _dot_nt(whi, h_hi) + _dot_nt(whi, h_lo) + _dot_nt(wlo_ref[...], h_hi)
    s = jax.nn.sigmoid(logit)
    sel = s + br_ref[...]
    epg = EXPERTS_PER_GROUP
    row = lambda a, e: a[e:e + 1, :]
    gscore = [_top2_of4(*[row(sel, g * epg + j) for j in range(epg)]) for g in range(N_GROUPS)]
    g_best = jnp.zeros_like(gscore[0], dtype=jnp.int32)
    best = gscore[0]
    for g in range(1, N_GROUPS):
        upd = gscore[g] > best
        g_best = jnp.where(upd, g, g_best)
        best = jnp.where(upd, gscore[g], best)
    vs, ss = [], []
    for j in range(epg):
        v = row(sel, j)
        sv = row(s, j)
        for g in range(1, N_GROUPS):
            v = jnp.where(g_best == g, row(sel, g * epg + j), v)
            sv = jnp.where(g_best == g, row(s, g * epg + j), sv)
        vs.append(v)
        ss.append(sv)
    i1 = jnp.zeros_like(g_best)
    v1 = vs[0]
    for j in range(1, epg):
        upd = vs[j] > v1
        i1 = jnp.where(upd, j, i1)
        v1 = jnp.where(upd, vs[j], v1)
    i2 = jnp.full_like(g_best, -1)
    v2 = jnp.full_like(v1, -jnp.inf)
    for j in range(epg):
        upd = (i1 != j) & ((i2 < 0) | (vs[j] > v2))
        i2 = jnp.where(upd, j, i2)
        v2 = jnp.where(upd, vs[j], v2)
    w1 = sum(jnp.where(i1 == j, ss[j], 0.0) for j in range(epg))
    w2 = sum(jnp.where(i2 == j, ss[j], 0.0) for j in range(epg))
    tot = w1 + w2
    w1 = w1 / tot
    w2 = w2 / tot
    rows_out = []
    for e in range(N_EXPERTS):
        g, j = divmod(e, epg)
        we = jnp.where(i1 == j, w1, 0.0) + jnp.where(i2 == j, w2, 0.0)
        rows_out.append(jnp.where(g_best == g, we, 0.0))
    dw_ref[0] = jnp.concatenate(rows_out, axis=0)


def _router(x, gain, sc, sh, wr_hi, wr_lo, b_router):
    b, n, d = x.shape
    tm = min(512, n)
    ne = wr_hi.shape[0]
    return pl.pallas_call(
        _router_kernel,
        grid=(b, n // tm),
        in_specs=[
            pl.BlockSpec((1, tm, d), lambda bb, i: (bb, i, 0)),
            pl.BlockSpec((1, d), lambda bb, i: (0, 0)),
            pl.BlockSpec((1, 1, d), lambda bb, i: (bb, 0, 0)),
            pl.BlockSpec((1, 1, d), lambda bb, i: (bb, 0, 0)),
            pl.BlockSpec((ne, d), lambda bb, i: (0, 0)),
            pl.BlockSpec((ne, d), lambda bb, i: (0, 0)),
            pl.BlockSpec((ne, 1), lambda bb, i: (0, 0)),
        ],
        out_specs=[
            pl.BlockSpec((1, tm, d), lambda bb, i: (bb, i, 0)),
            pl.BlockSpec((1, ne, tm), lambda bb, i: (bb, 0, i)),
        ],
        out_shape=[jax.ShapeDtypeStruct((b, n, d), BF16), jax.ShapeDtypeStruct((b, ne, n), F32)],
        compiler_params=_cparams(("arbitrary", "arbitrary")),
        name="norm_router",
    )(x, gain.reshape(1, d), sc, sh, wr_hi, wr_lo, b_router.reshape(ne, 1))


def _moe_kernel(h_ref, dw_ref, wg_ref, wu_ref, wd_ref, x_ref, g2_ref, o_ref):
    e = pl.program_id(2)
    f = pl.program_id(3)

    @pl.when((e == 0) & (f == 0))
    def _():
        o_ref[0] = x_ref[0]

    h = h_ref[0]
    gate = _dot(h, wg_ref[0])
    up = _dot(h, wu_ref[0])
    a = (gate * jax.nn.sigmoid(gate) * up).astype(BF16)
    y = _dot(a, wd_ref[0])
    dw = dw_ref[0]
    lane = lax.broadcasted_iota(jnp.int32, dw.shape, 1)
    we = jnp.sum(jnp.where(lane == e, dw, 0.0), axis=1, keepdims=True)
    o_ref[0] += (we * g2_ref[0]) * y


def _moe_dense(h, dw, w_gu, w_down, x, g2):
    b, n, d = x.shape
    ne, _, f2 = w_gu.shape
    ff = f2 // 2
    fs = 512
    nf = ff // fs
    tm = min(512, n)
    return pl.pallas_call(
        _moe_kernel,
        grid=(b, n // tm, ne, nf),
        in_specs=[
            pl.BlockSpec((1, tm, d), lambda bb, i, e, f: (bb, i, 0)),
            pl.BlockSpec((1, tm, ne), lambda bb, i, e, f: (bb, i, 0)),
            pl.BlockSpec((1, d, fs), lambda bb, i, e, f: (e, 0, f)),
            pl.BlockSpec((1, d, fs), lambda bb, i, e, f: (e, 0, nf + f)),
            pl.BlockSpec((1, fs, d), lambda bb, i, e, f: (e, f, 0)),
            pl.BlockSpec((1, tm, d), lambda bb, i, e, f: (bb, i, 0)),
            pl.BlockSpec((1, 1, d), lambda bb, i, e, f: (bb, 0, 0)),
        ],
        out_specs=pl.BlockSpec((1, tm, d), lambda bb, i, e, f: (bb, i, 0)),
        out_shape=jax.ShapeDtypeStruct((b, n, d), F32),
        compiler_params=_cparams(("arbitrary", "arbitrary", "arbitrary", "arbitrary")),
        name="moe_experts",
    )(h, dw, w_gu, w_gu, w_down, x, g2)


def _rope_tables(n):
    t = jnp.arange(n, dtype=jnp.int32)
    row = (t // GRID_W).astype(F32)
    col = (t % GRID_W).astype(F32)
    axis_dim = HEAD_DIM // 2
    inv = ROPE_THETA ** (-jnp.arange(0, axis_dim, 2, dtype=F32) / axis_dim)
    ang = jnp.concatenate([row[:, None] * inv, col[:, None] * inv], axis=-1)
    cos, sin = jnp.cos(ang), jnp.sin(ang)
    return jnp.concatenate([cos, cos], axis=-1), jnp.concatenate([-sin, sin], axis=-1)


def kernel(x, c, ctx, c_ctx, w_mod, b_mod, norm1, norm2, w_in, qk_gain, pool_w, pool_scale,
           na_rpb, w_br, w_out, w_router, b_router, w_gu, w_down):
    b, n, d = x.shape
    depth = w_mod.shape[0]
    rows = n // GRID_W
    kr = min(NA_ROWS, rows)
    assert n % GRID_W == 0 and rows % kr == 0 and b + 1 <= 8

    cos, sin = _rope_tables(n)
    mods = _modulation(jnp.concatenate([c, c_ctx[None, :]], axis=0), w_mod, b_mod)
    wr_t = w_router.T
    wr_hi = wr_t.astype(BF16)
    wr_lo = (wr_t - wr_hi.astype(F32)).astype(BF16)

    for l in range(depth):
        last = l == depth - 1
        mx = mods[l, :b].reshape(b, 1, N_MOD, d)
        mc = jnp.broadcast_to(mods[l, b].reshape(1, 1, N_MOD, d), (b, 1, N_MOD, d))
        x_sh1, x_sc1, x_g1, x_sh2, x_sc2, x_g2 = [mx[:, :, k] for k in range(N_MOD)]
        c_sh1, c_sc1, c_g1, c_sh2, c_sc2, c_g2 = [mc[:, :, k] for k in range(N_MOD)]
        w_in_l = w_in[l].astype(BF16)
        w_br_l = w_br[l].astype(BF16)
        w_out_l = w_out[l].astype(BF16)
        pool_w_l = pool_w[l].astype(BF16)
        bias = _na_bias(na_rpb[l], kr)

        pc = _inproj(ctx, norm1[l], c_sc1, c_sh1, w_in_l, qk_gain[l], None, None, kv_only=last)
        kvc0 = 0 if last else HB_KV0
        px = _inproj(x, norm1[l], x_sc1, x_sh1, w_in_l, qk_gain[l], cos, sin)
        y_pool = _pool(px, pool_w_l, pool_scale[l])
        y_na = _na(px, pc, kvc0, bias)
        y_gqa = _gqa(px, pc, kvc0)
        x = _merge(x, px, y_pool, y_na, 0, y_gqa, 0, w_br_l, w_out_l, x_g1)

        w_gu_l = w_gu[l].astype(BF16)
        w_down_l = w_down[l].astype(BF16)
        if not last:
            yc_pool = _pool(pc, pool_w_l, pool_scale[l])
            yc = _ctx_attn(pc)
            ctx = _merge(ctx, pc, yc_pool, yc, 0, yc, 1, w_br_l, w_out_l, c_g1)
            hc, dwc = _router(ctx, norm2[l], c_sc2, c_sh2, wr_hi, wr_lo, b_router)
            ctx = _moe_dense(hc, dwc.transpose(0, 2, 1), w_gu_l, w_down_l, ctx, c_g2)
        hx, dwx = _router(x, norm2[l], x_sc2, x_sh2, wr_hi, wr_lo, b_router)
        x = _moe_dense(hx, dwx.transpose(0, 2, 1), w_gu_l, w_down_l, x, x_g2)
    return x
```

```python
import functools

import numpy as np
import jax
import jax.numpy as jnp
from jax import lax
from jax.experimental import pallas as pl
from jax.experimental.pallas import tpu as pltpu

F32 = jnp.float32
BF16 = jnp.bfloat16

GRID_W = 64
HEAD_DIM = 128
ROPE_THETA = 10000.0
EPS = 1e-6
POOL_WINDOWS = (2, 4, 8, 16)
POOL_CH = 128
POOL_WIDTH = len(POOL_WINDOWS) * POOL_CH
NA_HEADS = 6
NA_WIDTH = NA_HEADS * HEAD_DIM
NA_ROWS = 8
NA_COLS = 16
GQA_Q_HEADS = 6
GQA_KV_HEADS = 2
GQA_GROUP = GQA_Q_HEADS // GQA_KV_HEADS
GQA_Q_WIDTH = GQA_Q_HEADS * HEAD_DIM
N_BRANCH = 3
N_EXPERTS = 16
N_GROUPS = 4
EXPERTS_PER_GROUP = N_EXPERTS // N_GROUPS
N_MOD = 6
ATTN_SCALE = HEAD_DIM ** -0.5

COL_BLOCK = 2048
CB_POOL = 3 * COL_BLOCK // POOL_WIDTH
HB_NA_Q = (3 * COL_BLOCK + POOL_WIDTH) // HEAD_DIM
HB_GQA_Q = HB_NA_Q + NA_HEADS
HB_NA_K = HB_GQA_Q + GQA_Q_HEADS
HB_NA_V = HB_NA_K + NA_HEADS
HB_GQA_K = HB_NA_V + NA_HEADS
HB_GQA_V = HB_GQA_K + GQA_KV_HEADS
HB_KV0 = HB_NA_K

V7X_VMEM_LIMIT = 56 * 1024 * 1024
NEG_BIG = -1e30


def _cparams(sem):
    return pltpu.CompilerParams(dimension_semantics=sem, vmem_limit_bytes=V7X_VMEM_LIMIT)


def _dot(a, b):
    return jnp.dot(a, b, preferred_element_type=F32)


def _dot_nt(a, b):
    return lax.dot_general(a, b, (((1,), (1,)), ((), ())), preferred_element_type=F32)


def _mod_kernel(ct_ref, w_ref, b_ref, o_ref, *, n_rows):
    ct = ct_ref[...]
    a = ct * jax.nn.sigmoid(ct)
    w = w_ref[0]
    rows = [jnp.sum(w * a[:, r:r + 1], axis=0, keepdims=True) for r in range(n_rows)]
    rows += [jnp.zeros_like(rows[0])] * (8 - n_rows)
    o_ref[0] = jnp.concatenate(rows, axis=0) + b_ref[0]


def _modulation(c_rows, w_mod, b_mod):
    depth, d, nm = w_mod.shape
    n_rows = c_rows.shape[0]
    ct = jnp.zeros((d, 8), F32).at[:, :n_rows].set(c_rows.T)
    tn = 1024
    return pl.pallas_call(
        functools.partial(_mod_kernel, n_rows=n_rows),
        grid=(depth, nm // tn),
        in_specs=[
            pl.BlockSpec((d, 8), lambda l, j: (0, 0)),
            pl.BlockSpec((1, d, tn), lambda l, j: (l, 0, j)),
            pl.BlockSpec((1, 1, tn), lambda l, j: (l, 0, j)),
        ],
        out_specs=pl.BlockSpec((1, 8, tn), lambda l, j: (l, 0, j)),
        out_shape=jax.ShapeDtypeStruct((depth, 8, nm), F32),
        compiler_params=_cparams(("arbitrary", "arbitrary")),
        name="adaln_mod",
    )(ct, w_mod, b_mod.reshape(depth, 1, nm))


def _head_norm(a, gain):
    return a * lax.rsqrt(jnp.mean(a * a, axis=-1, keepdims=True) + EPS) * gain


def _rope(y, cos, sin):
    return y * cos + pltpu.roll(y, HEAD_DIM // 2, 1) * sin


def _inproj_kernel(*refs, rope, j_off):
    if rope:
        x_ref, g_ref, sc_ref, sh_ref, w_ref, qg_ref, cos_ref, sin_ref, o_ref = refs
    else:
        x_ref, g_ref, sc_ref, sh_ref, w_ref, qg_ref, o_ref = refs
    j = pl.program_id(0) + j_off
    x = x_ref[0]
    h = x * lax.rsqrt(jnp.mean(x * x, axis=-1, keepdims=True) + EPS) * g_ref[...]
    h = h * (1.0 + sc_ref[0]) + sh_ref[0]
    acc = _dot(h.astype(BF16), w_ref[...])

    def head(c):
        return acc[:, c * HEAD_DIM:(c + 1) * HEAD_DIM]

    def put(c, y):
        o_ref[0, :, c * HEAD_DIM:(c + 1) * HEAD_DIM] = y.astype(BF16)

    @pl.when(j < 3)
    def _():
        o_ref[0] = jax.nn.sigmoid(acc).astype(BF16)

    @pl.when(j == 3)
    def _():
        npool = POOL_WIDTH // HEAD_DIM
        o_ref[0, :, :POOL_WIDTH] = acc[:, :POOL_WIDTH].astype(BF16)
        for c in range(npool, npool + NA_HEADS):
            put(c, _head_norm(head(c), qg_ref[0:1, :]) * ATTN_SCALE)
        for c in range(npool + NA_HEADS, npool + NA_HEADS + GQA_Q_HEADS):
            y = _head_norm(head(c), qg_ref[2:3, :])
            if rope:
                y = _rope(y, cos_ref[...], sin_ref[...])
            put(c, y * ATTN_SCALE)

    @pl.when(j == 4)
    def _():
        for c in range(NA_HEADS):
            put(c, _head_norm(head(c), qg_ref[1:2, :]))
        c0 = 2 * NA_HEADS
        o_ref[0, :, NA_WIDTH:c0 * HEAD_DIM] = acc[:, NA_WIDTH:c0 * HEAD_DIM].astype(BF16)
        for c in range(c0, c0 + GQA_KV_HEADS):
            y = _head_norm(head(c), qg_ref[3:4, :])
            if rope:
                y = _rope(y, cos_ref[...], sin_ref[...])
            put(c, y)
        c1 = c0 + GQA_KV_HEADS
        o_ref[0, :, c1 * HEAD_DIM:] = acc[:, c1 * HEAD_DIM:].astype(BF16)


def _inproj(x, gain, sc, sh, w, qgain, cos, sin, *, kv_only=False):
    b, n, d = x.shape
    rope = cos is not None
    tm = min(512, n)
    ncb = w.shape[1] // COL_BLOCK
    j_off = ncb - 1 if kv_only else 0
    nj = ncb - j_off
    in_specs = [
        pl.BlockSpec((1, tm, d), lambda j, bb, i: (bb, i, 0)),
        pl.BlockSpec((1, d), lambda j, bb, i: (0, 0)),
        pl.BlockSpec((1, 1, d), lambda j, bb, i: (bb, 0, 0)),
        pl.BlockSpec((1, 1, d), lambda j, bb, i: (bb, 0, 0)),
        pl.BlockSpec((d, COL_BLOCK), lambda j, bb, i: (0, j + j_off)),
        pl.BlockSpec((4, HEAD_DIM), lambda j, bb, i: (0, 0)),
    ]
    args = [x, gain.reshape(1, d), sc, sh, w, qgain]
    if rope:
        in_specs += [pl.BlockSpec((tm, HEAD_DIM), lambda j, bb, i: (i, 0))] * 2
        args += [cos, sin]
    return pl.pallas_call(
        functools.partial(_inproj_kernel, rope=rope, j_off=j_off),
        grid=(nj, b, n // tm),
        in_specs=in_specs,
        out_specs=pl.BlockSpec((1, tm, COL_BLOCK), lambda j, bb, i: (bb, i, j)),
        out_shape=jax.ShapeDtypeStruct((b, n, nj * COL_BLOCK), BF16),
        compiler_params=_cparams(("arbitrary", "arbitrary", "arbitrary")),
        name="inproj_rope" if rope else ("inproj_kv" if kv_only else "inproj_ctx"),
    )(*args)


POOL_HALO = 16


def _pool_kernel(prev_ref, cur_ref, next_ref, w_ref, s_ref, o_ref, buf_ref, *, tm, n):
    i = pl.program_id(1)
    nt = pl.num_programs(1)
    hl = POOL_HALO
    buf_ref[pl.ds(hl, tm), :] = cur_ref[0].astype(F32)
    buf_ref[pl.ds(0, hl), :] = jnp.where(i > 0, prev_ref[0].astype(F32), 0.0)
    buf_ref[pl.ds(hl + tm, hl), :] = jnp.where(i < nt - 1, next_ref[0].astype(F32), 0.0)
    t = i * tm + lax.broadcasted_iota(jnp.int32, (tm, 1), 0)
    for g, w in enumerate(POOL_WINDOWS):
        sl = slice(g * POOL_CH, (g + 1) * POOL_CH)
        acc = buf_ref[pl.ds(hl - w // 2, tm), sl]
        for off in range(-w // 2 + 1, w // 2):
            acc = acc + buf_ref[pl.ds(hl + off, tm), sl]
        cnt = (jnp.minimum(t + w // 2, n) - jnp.maximum(t - w // 2, 0)).astype(F32)
        dlt = acc / cnt - buf_ref[pl.ds(hl, tm), sl]
        y = _dot(dlt.astype(BF16), w_ref[g]) * s_ref[:, sl]
        o_ref[0, :, sl] = y.astype(BF16)


def _pool(p, pool_w, pool_scale):
    b, n, _ = p.shape
    tm = min(512, n)
    hl = POOL_HALO
    hb = tm // hl
    last = n // hl - 1
    return pl.pallas_call(
        functools.partial(_pool_kernel, tm=tm, n=n),
        grid=(b, n // tm),
        in_specs=[
            pl.BlockSpec((1, hl, POOL_WIDTH), lambda bb, i: (bb, jnp.maximum(i * hb - 1, 0), CB_POOL)),
            pl.BlockSpec((1, tm, POOL_WIDTH), lambda bb, i: (bb, i, CB_POOL)),
            pl.BlockSpec((1, hl, POOL_WIDTH), lambda bb, i: (bb, jnp.minimum((i + 1) * hb, last), CB_POOL)),
            pl.BlockSpec((len(POOL_WINDOWS), POOL_CH, POOL_CH), lambda bb, i: (0, 0, 0)),
            pl.BlockSpec((1, POOL_WIDTH), lambda bb, i: (0, 0)),
        ],
        out_specs=pl.BlockSpec((1, tm, POOL_WIDTH), lambda bb, i: (bb, i, 0)),
        out_shape=jax.ShapeDtypeStruct((b, n, POOL_WIDTH), BF16),
        scratch_shapes=[pltpu.VMEM((tm + 2 * hl, POOL_WIDTH), F32)],
        compiler_params=_cparams(("arbitrary", "arbitrary")),
        name="pool_mixer",
    )(p, p, p, pool_w, pool_scale.reshape(1, POOL_WIDTH))


def _na_bias(rpb, kr):
    col = np.arange(GRID_W)
    col_start = np.clip(col - NA_COLS // 2, 0, GRID_W - NA_COLS)
    dcol = col[None, :] - col[:, None] + (NA_COLS - 1)
    ok = (col[None, :] >= col_start[:, None]) & (col[None, :] < col_start[:, None] + NA_COLS)
    onehot = ((dcol[:, :, None] == np.arange(2 * NA_COLS - 1)) & ok[:, :, None]).astype(np.float32)
    colbias = jnp.einsum("hrc,qkc->hrqk", rpb.astype(F32), onehot, precision=lax.Precision.HIGHEST)
    colbias = jnp.where(ok[None, None], colbias, NEG_BIG)
    per_d = [colbias[:, NA_ROWS - 1 - d:NA_ROWS - 1 - d + kr] for d in range(kr)]
    bias = jnp.stack(per_d)
    h = rpb.shape[0]
    return bias.transpose(0, 1, 3, 2, 4).reshape(kr, h, GRID_W, kr * GRID_W)


NA_UNROLL = 4


def _na_kernel(q_ref, k_ref, v_ref, kc_ref, vc_ref, bias_ref, o_ref, *, rb, rows, kr):
    i = pl.program_id(2)
    kc = kc_ref[0]
    vc = vc_ref[0]
    w = GRID_W

    def body(rr, carry):
        r = i * rb + rr
        rs = jnp.clip(r - kr // 2, 0, rows - kr)
        q = q_ref[0, pl.ds(pl.multiple_of(rr * w, w), w), :]
        kw = k_ref[0, pl.ds(pl.multiple_of(rs * w, w), kr * w), :]
        vw = v_ref[0, pl.ds(pl.multiple_of(rs * w, w), kr * w), :]
        s_loc = _dot_nt(q, kw) + bias_ref[r - rs, 0]
        s_ctx = _dot_nt(q, kc)
        m = jnp.maximum(jnp.max(s_loc, axis=-1, keepdims=True), jnp.max(s_ctx, axis=-1, keepdims=True))
        p_loc = jnp.exp(s_loc - m)
        p_ctx = jnp.exp(s_ctx - m)
        den = jnp.sum(p_loc, axis=-1, keepdims=True) + jnp.sum(p_ctx, axis=-1, keepdims=True)
        o = _dot(p_loc.astype(BF16), vw) + _dot(p_ctx.astype(BF16), vc)
        o_ref[0, pl.ds(pl.multiple_of(rr * w, w), w), :] = (o / den).astype(BF16)
        return carry

    lax.fori_loop(0, rb, body, 0, unroll=NA_UNROLL)


def _na(p, pc, kvc0, bias):
    b, n, _ = p.shape
    nc = pc.shape[1]
    rows = n // GRID_W
    kr = bias.shape[0]
    rb = min(16, rows)
    return pl.pallas_call(
        functools.partial(_na_kernel, rb=rb, rows=rows, kr=kr),
        grid=(b, NA_HEADS, rows // rb),
        in_specs=[
            pl.BlockSpec((1, rb * GRID_W, HEAD_DIM), lambda bb, h, i: (bb, i, HB_NA_Q + h)),
            pl.BlockSpec((1, n, HEAD_DIM), lambda bb, h, i: (bb, 0, HB_NA_K + h)),
            pl.BlockSpec((1, n, HEAD_DIM), lambda bb, h, i: (bb, 0, HB_NA_V + h)),
            pl.BlockSpec((1, nc, HEAD_DIM), lambda bb, h, i: (bb, 0, kvc0 + h)),
            pl.BlockSpec((1, nc, HEAD_DIM), lambda bb, h, i: (bb, 0, kvc0 + NA_HEADS + h)),
            pl.BlockSpec((kr, 1, GRID_W, kr * GRID_W), lambda bb, h, i: (0, h, 0, 0)),
        ],
        out_specs=pl.BlockSpec((1, rb * GRID_W, HEAD_DIM), lambda bb, h, i: (bb, i, h)),
        out_shape=jax.ShapeDtypeStruct((b, n, NA_WIDTH), BF16),
        compiler_params=_cparams(("arbitrary", "arbitrary", "arbitrary")),
        name="na_attn",
    )(p, p, p, pc, pc, bias)


GQA_ONES_ROWS = 16


def _gqa_kernel(q0_ref, q1_ref, q2_ref, k_ref, v_ref, kc_ref, vc_ref, o_ref,
                qt_s, vt_s, vct_s, s_s, sc_s, m_s, acc_s, *, tk, n):
    i = pl.program_id(2)
    tq = q0_ref.shape[1]
    nchunk = n // tk
    hd = HEAD_DIM

    def to_t(a):
        return a.astype(F32).T.astype(BF16)

    @pl.when(i == 0)
    def _():
        def tr(c, carry):
            off = pl.multiple_of(c * tk, tk)
            vt_s[c, 0:hd, :] = to_t(v_ref[0, pl.ds(off, tk), :])
            vt_s[c, hd:, :] = jnp.ones((GQA_ONES_ROWS, tk), BF16)
            return carry
        lax.fori_loop(0, nchunk, tr, 0)
        vct_s[0:hd, :] = to_t(vc_ref[0])
        vct_s[hd:, :] = jnp.ones((GQA_ONES_ROWS, vct_s.shape[1]), BF16)

    for g, qr in enumerate((q0_ref, q1_ref, q2_ref)):
        qt_s[:, g * tq:(g + 1) * tq] = to_t(qr[0])

    def scores(c, slot):
        off = pl.multiple_of(c * tk, tk)
        s_s[slot] = _dot(k_ref[0, pl.ds(off, tk), :], qt_s[...])

    def update(s, vt, first=False):
        smax = jnp.max(s, axis=0, keepdims=True)
        if first:
            m_new = smax
        else:
            m_prev = m_s[...]
            m_new = jnp.maximum(m_prev, smax)
            alpha = jnp.exp(m_prev - m_new)
        p = jnp.exp(s - m_new).astype(BF16)
        pv = _dot(vt, p)
        acc_s[...] = pv if first else alpha * acc_s[...] + pv
        m_s[...] = m_new

    scores(0, 0)
    sc_s[...] = _dot(kc_ref[0], qt_s[...])
    update(sc_s[...], vct_s[...], first=True)

    def body(c2, carry):
        c = 2 * c2
        scores(c + 1, 1)
        update(s_s[0], vt_s[c])
        scores(jnp.minimum(c + 2, nchunk - 1), 0)
        update(s_s[1], vt_s[c + 1])
        return carry

    lax.fori_loop(0, nchunk // 2, body, 0)
    acc = acc_s[...]
    o = (acc[0:hd] / acc[hd:hd + 1]).T
    for g in range(GQA_GROUP):
        o_ref[0, :, g * hd:(g + 1) * hd] = o[g * tq:(g + 1) * tq].astype(BF16)


def _gqa(p, pc, kvc0):
    b, n, _ = p.shape
    nc = pc.shape[1]
    tq = min(256, n)
    tk = min(512, n // 2)
    assert n % (2 * tk) == 0
    kc0 = kvc0 + 2 * NA_HEADS
    nq = GQA_GROUP * tq
    vr = HEAD_DIM + GQA_ONES_ROWS

    def qspec(g):
        return pl.BlockSpec((1, tq, HEAD_DIM), lambda bb, h, i: (bb, i, HB_GQA_Q + h * GQA_GROUP + g))

    return pl.pallas_call(
        functools.partial(_gqa_kernel, tk=tk, n=n),
        grid=(b, GQA_KV_HEADS, n // tq),
        in_specs=[
            qspec(0), qspec(1), qspec(2),
            pl.BlockSpec((1, n, HEAD_DIM), lambda bb, h, i: (bb, 0, HB_GQA_K + h)),
            pl.BlockSpec((1, n, HEAD_DIM), lambda bb, h, i: (bb, 0, HB_GQA_V + h)),
            pl.BlockSpec((1, nc, HEAD_DIM), lambda bb, h, i: (bb, 0, kc0 + h)),
            pl.BlockSpec((1, nc, HEAD_DIM), lambda bb, h, i: (bb, 0, kc0 + GQA_KV_HEADS + h)),
        ],
        out_specs=pl.BlockSpec((1, tq, GQA_GROUP * HEAD_DIM), lambda bb, h, i: (bb, i, h)),
        out_shape=jax.ShapeDtypeStruct((b, n, GQA_Q_WIDTH), BF16),
        scratch_shapes=[
            pltpu.VMEM((HEAD_DIM, nq), BF16),
            pltpu.VMEM((n // tk, vr, tk), BF16),
            pltpu.VMEM((vr, nc), BF16),
            pltpu.VMEM((2, tk, nq), F32),
            pltpu.VMEM((nc, nq), F32),
            pltpu.VMEM((1, nq), F32),
            pltpu.VMEM((vr, nq), F32),
        ],
        compiler_params=_cparams(("arbitrary", "arbitrary", "arbitrary")),
        name="gqa_attn",
    )(p, p, p, p, p, pc, pc)


def _ctx_attn_kernel(q_ref, k_ref, v_ref, o_ref):
    s = _dot_nt(q_ref[0], k_ref[0])
    m = jnp.max(s, axis=-1, keepdims=True)
    p = jnp.exp(s - m)
    den = jnp.sum(p, axis=-1, keepdims=True)
    o_ref[0] = (_dot(p.astype(BF16), v_ref[0]) / den).astype(BF16)


def _ctx_attn(pc):
    b, nc, _ = pc.shape
    nh = NA_HEADS + GQA_Q_HEADS

    def kmap(bb, h):
        g = jnp.maximum(h - NA_HEADS, 0) // GQA_GROUP
        return bb, 0, jnp.where(h < NA_HEADS, HB_NA_K + h, HB_GQA_K + g)

    def vmap_(bb, h):
        g = jnp.maximum(h - NA_HEADS, 0) // GQA_GROUP
        return bb, 0, jnp.where(h < NA_HEADS, HB_NA_V + h, HB_GQA_V + g)

    return pl.pallas_call(
        _ctx_attn_kernel,
        grid=(b, nh),
        in_specs=[
            pl.BlockSpec((1, nc, HEAD_DIM), lambda bb, h: (bb, 0, HB_NA_Q + h)),
            pl.BlockSpec((1, nc, HEAD_DIM), kmap),
            pl.BlockSpec((1, nc, HEAD_DIM), vmap_),
        ],
        out_specs=pl.BlockSpec((1, nc, HEAD_DIM), lambda bb, h: (bb, 0, h)),
        out_shape=jax.ShapeDtypeStruct((b, nc, nh * HEAD_DIM), BF16),
        compiler_params=_cparams(("arbitrary", "arbitrary")),
        name="ctx_attn",
    )(pc, pc, pc)


def _merge_kernel(x_ref, ga_ref, gb_ref, gc_ref, yp_ref, yn_ref, yg_ref, wbr_ref, wout_ref, g1_ref, o_ref):
    r1 = POOL_WIDTH
    r2 = POOL_WIDTH + NA_WIDTH
    z = ga_ref[0].astype(F32) * _dot(yp_ref[0], wbr_ref[0:r1, :])
    z = z + gb_ref[0].astype(F32) * _dot(yn_ref[0], wbr_ref[r1:r2, :])
    z = z + gc_ref[0].astype(F32) * _dot(yg_ref[0], wbr_ref[r2:, :])
    o_ref[0] = x_ref[0] + g1_ref[0] * _dot(z.astype(BF16), wout_ref[...])


def _merge(x, p, y_pool, y_na, na_cb, y_gqa, gqa_cb, w_br, w_out, g1):
    b, n, d = x.shape
    tm = min(256, n)
    const = lambda bb, i: (0, 0)
    return pl.pallas_call(
        _merge_kernel,
        grid=(b, n // tm),
        in_specs=[
            pl.BlockSpec((1, tm, d), lambda bb, i: (bb, i, 0)),
            pl.BlockSpec((1, tm, d), lambda bb, i: (bb, i, 0)),
            pl.BlockSpec((1, tm, d), lambda bb, i: (bb, i, 1)),
            pl.BlockSpec((1, tm, d), lambda bb, i: (bb, i, 2)),
            pl.BlockSpec((1, tm, POOL_WIDTH), lambda bb, i: (bb, i, 0)),
            pl.BlockSpec((1, tm, NA_WIDTH), lambda bb, i: (bb, i, na_cb)),
            pl.BlockSpec((1, tm, GQA_Q_WIDTH), lambda bb, i: (bb, i, gqa_cb)),
            pl.BlockSpec(w_br.shape, const, pipeline_mode=pl.Buffered(1)),
            pl.BlockSpec(w_out.shape, const, pipeline_mode=pl.Buffered(1)),
            pl.BlockSpec((1, 1, d), lambda bb, i: (bb, 0, 0)),
        ],
        out_specs=pl.BlockSpec((1, tm, d), lambda bb, i: (bb, i, 0)),
        out_shape=jax.ShapeDtypeStruct((b, n, d), F32),
        compiler_params=_cparams(("arbitrary", "arbitrary")),
        name="branch_merge",
    )(x, p, p, p, y_pool, y_na, y_gqa, w_br, w_out, g1)


def _top2_of4(a, b, c, d):
    hi1, lo1 = jnp.maximum(a, b), jnp.minimum(a, b)
    hi2, lo2 = jnp.maximum(c, d), jnp.minimum(c, d)
    return jnp.maximum(hi1, hi2) + jnp.maximum(jnp.minimum(hi1, hi2), jnp.maximum(lo1, lo2))


def _router_kernel(x_ref, g_ref, sc_ref, sh_ref, whi_ref, wlo_ref, br_ref,
                   h_ref, e_ref, w_ref, rank_ref, cnt_ref, carry_ref):
    first = (pl.program_id(0) == 0) & (pl.program_id(1) == 0)

    @pl.when(first)
    def _():
        carry_ref[...] = jnp.zeros_like(carry_ref)

    x = x_ref[0]
    h = x * lax.rsqrt(jnp.mean(x * x, axis=-1, keepdims=True) + EPS) * g_ref[...]
    h = h * (1.0 + sc_ref[0]) + sh_ref[0]
    h_ref[0] = h
    h_hi = h.astype(BF16)
    h_lo = (h - h_hi.astype(F32)).astype(BF16)
    whi = whi_ref[...]
    logit = _dot_nt(whi, h_hi) + _dot_nt(whi, h_lo) + _dot_nt(wlo_ref[...], h_hi)
    s = jax.nn.sigmoid(logit)
    sel = s + br_ref[...]
    epg = EXPERTS_PER_GROUP
    row = lambda a, e: a[e:e + 1, :]
    gscore = [_top2_of4(*[row(sel, g * epg + j) for j in range(epg)]) for g in range(N_GROUPS)]
    g_best = jnp.zeros_like(gscore[0], dtype=jnp.int32)
    best = gscore[0]
    for g in range(1, N_GROUPS):
        upd = gscore[g] > best
        g_best = jnp.where(upd, g, g_best)
        best = jnp.where(upd, gscore[g], best)
    vs, ss = [], []
    for j in range(epg):
        v = row(sel, j)
        sv = row(s, j)
        for g in range(1, N_GROUPS):
            v = jnp.where(g_best == g, row(sel, g * epg + j), v)
            sv = jnp.where(g_best == g, row(s, g * epg + j), sv)
        vs.append(v)
        ss.append(sv)
    i1 = jnp.zeros_like(g_best)
    v1 = vs[0]
    for j in range(1, epg):
        upd = vs[j] > v1
        i1 = jnp.where(upd, j, i1)
        v1 = jnp.where(upd, vs[j], v1)
    i2 = jnp.full_like(g_best, -1)
    v2 = jnp.full_like(v1, -jnp.inf)
    for j in range(epg):
        upd = (i1 != j) & ((i2 < 0) | (vs[j] > v2))
        i2 = jnp.where(upd, j, i2)
        v2 = jnp.where(upd, vs[j], v2)
    w1 = sum(jnp.where(i1 == j, ss[j], 0.0) for j in range(epg))
    w2 = sum(jnp.where(i2 == j, ss[j], 0.0) for j in range(epg))
    tot = w1 + w2
    w_ref[0] = jnp.concatenate([w1 / tot, w2 / tot], axis=0)
    e1 = g_best * epg + i1
    e2 = g_best * epg + i2
    e_ref[0] = jnp.concatenate([e1, e2], axis=0)

    tm = x.shape[0]
    eidx = lax.broadcasted_iota(jnp.int32, (N_EXPERTS, tm), 0)
    oh1 = eidx == e1
    oh2 = eidx == e2
    oh = jnp.where(oh1 | oh2, 1.0, 0.0)
    before = lax.broadcasted_iota(jnp.int32, (tm, tm), 0) < lax.broadcasted_iota(jnp.int32, (tm, tm), 1)
    prefix = _dot(oh.astype(BF16), jnp.where(before, 1.0, 0.0).astype(BF16))
    base = carry_ref[...] + prefix
    r1 = jnp.sum(jnp.where(oh1, base, 0.0), axis=0, keepdims=True)
    r2 = jnp.sum(jnp.where(oh2, base, 0.0), axis=0, keepdims=True)
    rank_ref[0] = jnp.concatenate([r1, r2], axis=0).astype(jnp.int32)
    carry = carry_ref[...] + jnp.sum(oh, axis=1, keepdims=True)
    carry_ref[...] = carry
    cnt_ref[...] = jnp.broadcast_to(carry, cnt_ref.shape).astype(jnp.int32)


def _router(x, gain, sc, sh, wr_hi, wr_lo, b_router):
    b, n, d = x.shape
    tm = min(512, n)
    ne = wr_hi.shape[0]
    pair = pl.BlockSpec((1, 2, tm), lambda bb, i: (bb, 0, i))
    return pl.pallas_call(
        _router_kernel,
        grid=(b, n // tm),
        in_specs=[
            pl.BlockSpec((1, tm, d), lambda bb, i: (bb, i, 0)),
            pl.BlockSpec((1, d), lambda bb, i: (0, 0)),
            pl.BlockSpec((1, 1, d), lambda bb, i: (bb, 0, 0)),
            pl.BlockSpec((1, 1, d), lambda bb, i: (bb, 0, 0)),
            pl.BlockSpec((ne, d), lambda bb, i: (0, 0)),
            pl.BlockSpec((ne, d), lambda bb, i: (0, 0)),
            pl.BlockSpec((ne, 1), lambda bb, i: (0, 0)),
        ],
        out_specs=[
            pl.BlockSpec((1, tm, d), lambda bb, i: (bb, i, 0)),
            pair, pair, pair,
            pl.BlockSpec((ne, HEAD_DIM), lambda bb, i: (0, 0)),
        ],
        out_shape=[
            jax.ShapeDtypeStruct((b, n, d), F32),
            jax.ShapeDtypeStruct((b, 2, n), jnp.int32),
            jax.ShapeDtypeStruct((b, 2, n), F32),
            jax.ShapeDtypeStruct((b, 2, n), jnp.int32),
            jax.ShapeDtypeStruct((ne, HEAD_DIM), jnp.int32),
        ],
        scratch_shapes=[pltpu.VMEM((ne, 1), F32)],
        compiler_params=_cparams(("arbitrary", "arbitrary")),
        name="norm_router",
    )(x, gain.reshape(1, d), sc, sh, wr_hi, wr_lo, b_router.reshape(ne, 1))


MOE_TILE = 256
MOE_DMA_LAG = 32
MOE_COMBINE_TILE = 256


def _moe_plan(e, rank, cnt):
    b, _, n = e.shape
    t = b * n
    counts = cnt[:, 0]
    ntile_e = (counts + MOE_TILE - 1) // MOE_TILE
    tile_end = jnp.cumsum(ntile_e)
    off = (tile_end - ntile_e) * MOE_TILE
    nt = 2 * t // MOE_TILE + N_EXPERTS
    tile_expert = jnp.sum(jnp.arange(nt)[:, None] >= tile_end[None, :], axis=1)
    tile_expert = jnp.minimum(tile_expert, N_EXPERTS - 1).astype(jnp.int32)
    ef = e.transpose(1, 0, 2).reshape(2, t)
    rf = rank.transpose(1, 0, 2).reshape(2, t)
    pos = rf + jnp.sum(jnp.where(ef[..., None] == jnp.arange(N_EXPERTS), off, 0), axis=-1)
    return pos.reshape(2 * t).astype(jnp.int32), tile_expert, tile_end[-1:].astype(jnp.int32), nt


def _scatter_kernel(pos_ref, h_hbm, xs0_hbm, xs_hbm, sem, *, t):
    del xs0_hbm

    def copies(tok):
        return [pltpu.make_async_copy(h_hbm.at[pl.ds(tok, 1)], xs_hbm.at[pl.ds(pos_ref[k * t + tok], 1)], sem)
                for k in range(2)]

    def wait(tok):
        for cp in copies(tok):
            cp.wait()

    def body(tok, carry):
        for cp in copies(tok):
            cp.start()

        @pl.when(tok >= MOE_DMA_LAG)
        def _():
            wait(tok - MOE_DMA_LAG)
        return carry

    lax.fori_loop(0, t, body, 0)
    lax.fori_loop(max(t - MOE_DMA_LAG, 0), t, lambda tok, c: (wait(tok), c)[1], 0)


def _scatter_rows(pos, h, nrows):
    t, d = h.shape
    any_spec = pl.BlockSpec(memory_space=pl.ANY)
    return pl.pallas_call(
        functools.partial(_scatter_kernel, t=t),
        grid_spec=pltpu.PrefetchScalarGridSpec(
            num_scalar_prefetch=1, grid=(1,),
            in_specs=[any_spec, any_spec], out_specs=any_spec,
            scratch_shapes=[pltpu.SemaphoreType.DMA(())]),
        out_shape=jax.ShapeDtypeStruct((nrows, d), F32),
        input_output_aliases={2: 0},
        compiler_params=pltpu.CompilerParams(dimension_semantics=("arbitrary",), has_side_effects=True),
        name="moe_scatter",
    )(pos, h, jnp.zeros((nrows, d), F32))


def _experts_kernel(te_ref, nv_ref, xs_ref, wgu_ref, wd_ref, ys_ref):
    del te_ref

    @pl.when(pl.program_id(0) < nv_ref[0])
    def _():
        gu = _dot(xs_ref[...].astype(BF16), wgu_ref[0])
        ff = gu.shape[1] // 2
        gate = gu[:, :ff]
        a = (gate * jax.nn.sigmoid(gate) * gu[:, ff:]).astype(BF16)
        ys_ref[...] = _dot(a, wd_ref[0])


def _experts(xs, tile_expert, nvalid, w_gu, w_down):
    nrows, d = xs.shape
    nt = nrows // MOE_TILE
    ne, _, f2 = w_gu.shape
    row = lambda i, te, nv: (jnp.minimum(i, nv[0] - 1), 0)
    wmap = lambda i, te, nv: (te[jnp.minimum(i, nv[0] - 1)], 0, 0)
    return pl.pallas_call(
        _experts_kernel,
        grid_spec=pltpu.PrefetchScalarGridSpec(
            num_scalar_prefetch=2, grid=(nt,),
            in_specs=[
                pl.BlockSpec((MOE_TILE, d), row),
                pl.BlockSpec((1, d, f2), wmap),
                pl.BlockSpec((1, f2 // 2, d), wmap),
            ],
            out_specs=pl.BlockSpec((MOE_TILE, d), row)),
        out_shape=jax.ShapeDtypeStruct((nrows, d), F32),
        compiler_params=_cparams(("arbitrary",)),
        name="moe_experts",
    )(tile_expert, nvalid, xs, w_gu, w_down)


def _combine_kernel(pos_ref, x_ref, w_ref, g2_ref, ys_hbm, o_ref, buf, sem, *, t):
    i = pl.program_id(0)
    nsteps = pl.num_programs(0)
    tm = x_ref.shape[0]

    def copies(tile, slot, j):
        tok = tile * tm + j
        return [pltpu.make_async_copy(ys_hbm.at[pl.ds(pos_ref[k * t + tok], 1)],
                                      buf.at[slot, k, pl.ds(j, 1)], sem.at[slot]) for k in range(2)]

    def issue(tile, slot):
        def body(j, c):
            for cp in copies(tile, slot, j):
                cp.start()
            return c
        lax.fori_loop(0, tm, body, 0)

    def wait(tile, slot):
        def body(j, c):
            for cp in copies(tile, slot, j):
                cp.wait()
            return c
        lax.fori_loop(0, tm, body, 0)

    @pl.when(i == 0)
    def _():
        issue(0, 0)

    @pl.when(i + 1 < nsteps)
    def _():
        issue(i + 1, (i + 1) % 2)

    slot = i % 2
    wait(i, slot)
    w = w_ref[...]
    y = w[:, 0:1] * buf[slot, 0] + w[:, 1:2] * buf[slot, 1]
    o_ref[...] = x_ref[...] + g2_ref[0] * y


def _combine(pos, x, w, g2, ys):
    b, n, d = x.shape
    t = b * n
    tm = min(MOE_COMBINE_TILE, n)
    per_b = n // tm
    out = pl.pallas_call(
        functools.partial(_combine_kernel, t=t),
        grid_spec=pltpu.PrefetchScalarGridSpec(
            num_scalar_prefetch=1, grid=(t // tm,),
            in_specs=[
                pl.BlockSpec((tm, d), lambda i, p: (i, 0)),
                pl.BlockSpec((tm, 2), lambda i, p: (i, 0)),
                pl.BlockSpec((1, 1, d), lambda i, p: (i // per_b, 0, 0)),
                pl.BlockSpec(memory_space=pl.ANY),
            ],
            out_specs=pl.BlockSpec((tm, d), lambda i, p: (i, 0)),
            scratch_shapes=[pltpu.VMEM((2, 2, tm, d), F32), pltpu.SemaphoreType.DMA((2,))]),
        out_shape=jax.ShapeDtypeStruct((t, d), F32),
        compiler_params=_cparams(("arbitrary",)),
        name="moe_combine",
    )(pos, x.reshape(t, d), w, g2, ys)
    return out.reshape(b, n, d)


def _moe(x, gain, sc, sh, g2, wr_hi, wr_lo, b_router, w_gu, w_down):
    b, n, d = x.shape
    h, e, w, rank, cnt = _router(x, gain, sc, sh, wr_hi, wr_lo, b_router)
    pos, tile_expert, nvalid, nt = _moe_plan(e, rank, cnt)
    xs = _scatter_rows(pos, h.reshape(b * n, d), nt * MOE_TILE)
    ys = _experts(xs, tile_expert, nvalid, w_gu, w_down)
    return _combine(pos, x, w.transpose(0, 2, 1).reshape(b * n, 2), g2, ys)


def _rope_tables(n):
    t = jnp.arange(n, dtype=jnp.int32)
    row = (t // GRID_W).astype(F32)
    col = (t % GRID_W).astype(F32)
    axis_dim = HEAD_DIM // 2
    inv = ROPE_THETA ** (-jnp.arange(0, axis_dim, 2, dtype=F32) / axis_dim)
    ang = jnp.concatenate([row[:, None] * inv, col[:, None] * inv], axis=-1)
    cos, sin = jnp.cos(ang), jnp.sin(ang)
    return jnp.concatenate([cos, cos], axis=-1), jnp.concatenate([-sin, sin], axis=-1)


def kernel(x, c, ctx, c_ctx, w_mod, b_mod, norm1, norm2, w_in, qk_gain, pool_w, pool_scale,
           na_rpb, w_br, w_out, w_router, b_router, w_gu, w_down):
    b, n, d = x.shape
    depth = w_mod.shape[0]
    rows = n // GRID_W
    kr = min(NA_ROWS, rows)
    assert n % GRID_W == 0 and rows % kr == 0 and b + 1 <= 8

    cos, sin = _rope_tables(n)
    mods = _modulation(jnp.concatenate([c, c_ctx[None, :]], axis=0), w_mod, b_mod)
    wr_t = w_router.T
    wr_hi = wr_t.astype(BF16)
    wr_lo = (wr_t - wr_hi.astype(F32)).astype(BF16)

    for l in range(depth):
        last = l == depth - 1
        mx = mods[l, :b].reshape(b, 1, N_MOD, d)
        mc = jnp.broadcast_to(mods[l, b].reshape(1, 1, N_MOD, d), (b, 1, N_MOD, d))
        x_sh1, x_sc1, x_g1, x_sh2, x_sc2, x_g2 = [mx[:, :, k] for k in range(N_MOD)]
        c_sh1, c_sc1, c_g1, c_sh2, c_sc2, c_g2 = [mc[:, :, k] for k in range(N_MOD)]
        w_in_l = w_in[l].astype(BF16)
        w_br_l = w_br[l].astype(BF16)
        w_out_l = w_out[l].astype(BF16)
        pool_w_l = pool_w[l].astype(BF16)
        bias = _na_bias(na_rpb[l], kr)

        pc = _inproj(ctx, norm1[l], c_sc1, c_sh1, w_in_l, qk_gain[l], None, None, kv_only=last)
        kvc0 = 0 if last else HB_KV0
        px = _inproj(x, norm1[l], x_sc1, x_sh1, w_in_l, qk_gain[l], cos, sin)
        y_pool = _pool(px, pool_w_l, pool_scale[l])
        y_na = _na(px, pc, kvc0, bias)
        y_gqa = _gqa(px, pc, kvc0)
        x = _merge(x, px, y_pool, y_na, 0, y_gqa, 0, w_br_l, w_out_l, x_g1)

        w_gu_l = w_gu[l].astype(BF16)
        w_down_l = w_down[l].astype(BF16)
        if not last:
            yc_pool = _pool(pc, pool_w_l, pool_scale[l])
            yc = _ctx_attn(pc)
            ctx = _merge(ctx, pc, yc_pool, yc, 0, yc, 1, w_br_l, w_out_l, c_g1)
            ctx = _moe(ctx, norm2[l], c_sc2, c_sh2, c_g2, wr_hi, wr_lo, b_router, w_gu_l, w_down_l)
        x = _moe(x, norm2[l], x_sc2, x_sh2, x_g2, wr_hi, wr_lo, b_router, w_gu_l, w_down_l)
    return x
```

```python
import functools

import numpy as np
import jax
import jax.numpy as jnp
from jax import lax
from jax.experimental import pallas as pl
from jax.experimental.pallas import tpu as pltpu

F32 = jnp.float32
BF16 = jnp.bfloat16

GRID_W = 64
HEAD_DIM = 128
ROPE_THETA = 10000.0
EPS = 1e-6
POOL_WINDOWS = (2, 4, 8, 16)
POOL_CH = 128
POOL_WIDTH = len(POOL_WINDOWS) * POOL_CH
NA_HEADS = 6
NA_WIDTH = NA_HEADS * HEAD_DIM
NA_ROWS = 8
NA_COLS = 16
GQA_Q_HEADS = 6
GQA_KV_HEADS = 2
GQA_GROUP = GQA_Q_HEADS // GQA_KV_HEADS
GQA_Q_WIDTH = GQA_Q_HEADS * HEAD_DIM
N_BRANCH = 3
N_EXPERTS = 16
N_GROUPS = 4
EXPERTS_PER_GROUP = N_EXPERTS // N_GROUPS
N_MOD = 6
ATTN_SCALE = HEAD_DIM ** -0.5

COL_BLOCK = 2048
CB_POOL = 3 * COL_BLOCK // POOL_WIDTH
HB_NA_Q = (3 * COL_BLOCK + POOL_WIDTH) // HEAD_DIM
HB_GQA_Q = HB_NA_Q + NA_HEADS
HB_NA_K = HB_GQA_Q + GQA_Q_HEADS
HB_NA_V = HB_NA_K + NA_HEADS
HB_GQA_K = HB_NA_V + NA_HEADS
HB_GQA_V = HB_GQA_K + GQA_KV_HEADS
HB_KV0 = HB_NA_K

V7X_VMEM_LIMIT = 56 * 1024 * 1024
NEG_BIG = -1e30


def _cparams(sem):
    return pltpu.CompilerParams(dimension_semantics=sem, vmem_limit_bytes=V7X_VMEM_LIMIT)


def _dot(a, b):
    return jnp.dot(a, b, preferred_element_type=F32)


def _dot_nt(a, b):
    return lax.dot_general(a, b, (((1,), (1,)), ((), ())), preferred_element_type=F32)


def _mod_kernel(ct_ref, w_ref, b_ref, o_ref, *, n_rows):
    ct = ct_ref[...]
    a = ct * jax.nn.sigmoid(ct)
    w = w_ref[0]
    rows = [jnp.sum(w * a[:, r:r + 1], axis=0, keepdims=True) for r in range(n_rows)]
    rows += [jnp.zeros_like(rows[0])] * (8 - n_rows)
    o_ref[0] = jnp.concatenate(rows, axis=0) + b_ref[0]


def _modulation(c_rows, w_mod, b_mod):
    depth, d, nm = w_mod.shape
    n_rows = c_rows.shape[0]
    ct = jnp.zeros((d, 8), F32).at[:, :n_rows].set(c_rows.T)
    tn = 1024
    return pl.pallas_call(
        functools.partial(_mod_kernel, n_rows=n_rows),
        grid=(depth, nm // tn),
        in_specs=[
            pl.BlockSpec((d, 8), lambda l, j: (0, 0)),
            pl.BlockSpec((1, d, tn), lambda l, j: (l, 0, j)),
            pl.BlockSpec((1, 1, tn), lambda l, j: (l, 0, j)),
        ],
        out_specs=pl.BlockSpec((1, 8, tn), lambda l, j: (l, 0, j)),
        out_shape=jax.ShapeDtypeStruct((depth, 8, nm), F32),
        compiler_params=_cparams(("arbitrary", "arbitrary")),
        name="adaln_mod",
    )(ct, w_mod, b_mod.reshape(depth, 1, nm))


def _head_norm(a, gain):
    return a * lax.rsqrt(jnp.mean(a * a, axis=-1, keepdims=True) + EPS) * gain


def _rope(y, cos, sin):
    return y * cos + pltpu.roll(y, HEAD_DIM // 2, 1) * sin


def _inproj_kernel(*refs, rope, j_off):
    if rope:
        x_ref, g_ref, sc_ref, sh_ref, w_ref, qg_ref, cos_ref, sin_ref, o_ref = refs
    else:
        x_ref, g_ref, sc_ref, sh_ref, w_ref, qg_ref, o_ref = refs
    j = pl.program_id(0) + j_off
    x = x_ref[0]
    h = x * lax.rsqrt(jnp.mean(x * x, axis=-1, keepdims=True) + EPS) * g_ref[...]
    h = h * (1.0 + sc_ref[0]) + sh_ref[0]
    acc = _dot(h.astype(BF16), w_ref[...])

    def head(c):
        return acc[:, c * HEAD_DIM:(c + 1) * HEAD_DIM]

    def put(c, y):
        o_ref[0, :, c * HEAD_DIM:(c + 1) * HEAD_DIM] = y.astype(BF16)

    @pl.when(j < 3)
    def _():
        o_ref[0] = jax.nn.sigmoid(acc).astype(BF16)

    @pl.when(j == 3)
    def _():
        npool = POOL_WIDTH // HEAD_DIM
        o_ref[0, :, :POOL_WIDTH] = acc[:, :POOL_WIDTH].astype(BF16)
        for c in range(npool, npool + NA_HEADS):
            put(c, _head_norm(head(c), qg_ref[0:1, :]) * ATTN_SCALE)
        for c in range(npool + NA_HEADS, npool + NA_HEADS + GQA_Q_HEADS):
            y = _head_norm(head(c), qg_ref[2:3, :])
            if rope:
                y = _rope(y, cos_ref[...], sin_ref[...])
            put(c, y * ATTN_SCALE)

    @pl.when(j == 4)
    def _():
        for c in range(NA_HEADS):
            put(c, _head_norm(head(c), qg_ref[1:2, :]))
        c0 = 2 * NA_HEADS
        o_ref[0, :, NA_WIDTH:c0 * HEAD_DIM] = acc[:, NA_WIDTH:c0 * HEAD_DIM].astype(BF16)
        for c in range(c0, c0 + GQA_KV_HEADS):
            y = _head_norm(head(c), qg_ref[3:4, :])
            if rope:
                y = _rope(y, cos_ref[...], sin_ref[...])
            put(c, y)
        c1 = c0 + GQA_KV_HEADS
        o_ref[0, :, c1 * HEAD_DIM:] = acc[:, c1 * HEAD_DIM:].astype(BF16)


def _inproj(x, gain, sc, sh, w, qgain, cos, sin, *, kv_only=False):
    b, n, d = x.shape
    rope = cos is not None
    tm = min(512, n)
    ncb = w.shape[1] // COL_BLOCK
    j_off = ncb - 1 if kv_only else 0
    nj = ncb - j_off
    in_specs = [
        pl.BlockSpec((1, tm, d), lambda j, bb, i: (bb, i, 0)),
        pl.BlockSpec((1, d), lambda j, bb, i: (0, 0)),
        pl.BlockSpec((1, 1, d), lambda j, bb, i: (bb, 0, 0)),
        pl.BlockSpec((1, 1, d), lambda j, bb, i: (bb, 0, 0)),
        pl.BlockSpec((d, COL_BLOCK), lambda j, bb, i: (0, j + j_off)),
        pl.BlockSpec((4, HEAD_DIM), lambda j, bb, i: (0, 0)),
    ]
    args = [x, gain.reshape(1, d), sc, sh, w, qgain]
    if rope:
        in_specs += [pl.BlockSpec((tm, HEAD_DIM), lambda j, bb, i: (i, 0))] * 2
        args += [cos, sin]
    return pl.pallas_call(
        functools.partial(_inproj_kernel, rope=rope, j_off=j_off),
        grid=(nj, b, n // tm),
        in_specs=in_specs,
        out_specs=pl.BlockSpec((1, tm, COL_BLOCK), lambda j, bb, i: (bb, i, j)),
        out_shape=jax.ShapeDtypeStruct((b, n, nj * COL_BLOCK), BF16),
        compiler_params=_cparams(("arbitrary", "arbitrary", "arbitrary")),
        name="inproj_rope" if rope else ("inproj_kv" if kv_only else "inproj_ctx"),
    )(*args)


POOL_HALO = 16


def _pool_kernel(prev_ref, cur_ref, next_ref, w_ref, s_ref, o_ref, buf_ref, *, tm, n):
    i = pl.program_id(1)
    nt = pl.num_programs(1)
    hl = POOL_HALO
    buf_ref[pl.ds(hl, tm), :] = cur_ref[0].astype(F32)
    buf_ref[pl.ds(0, hl), :] = jnp.where(i > 0, prev_ref[0].astype(F32), 0.0)
    buf_ref[pl.ds(hl + tm, hl), :] = jnp.where(i < nt - 1, next_ref[0].astype(F32), 0.0)
    t = i * tm + lax.broadcasted_iota(jnp.int32, (tm, 1), 0)
    for g, w in enumerate(POOL_WINDOWS):
        sl = slice(g * POOL_CH, (g + 1) * POOL_CH)
        acc = buf_ref[pl.ds(hl - w // 2, tm), sl]
        for off in range(-w // 2 + 1, w // 2):
            acc = acc + buf_ref[pl.ds(hl + off, tm), sl]
        cnt = (jnp.minimum(t + w // 2, n) - jnp.maximum(t - w // 2, 0)).astype(F32)
        dlt = acc / cnt - buf_ref[pl.ds(hl, tm), sl]
        y = _dot(dlt.astype(BF16), w_ref[g]) * s_ref[:, sl]
        o_ref[0, :, sl] = y.astype(BF16)


def _pool(p, pool_w, pool_scale):
    b, n, _ = p.shape
    tm = min(512, n)
    hl = POOL_HALO
    hb = tm // hl
    last = n // hl - 1
    return pl.pallas_call(
        functools.partial(_pool_kernel, tm=tm, n=n),
        grid=(b, n // tm),
        in_specs=[
            pl.BlockSpec((1, hl, POOL_WIDTH), lambda bb, i: (bb, jnp.maximum(i * hb - 1, 0), CB_POOL)),
            pl.BlockSpec((1, tm, POOL_WIDTH), lambda bb, i: (bb, i, CB_POOL)),
            pl.BlockSpec((1, hl, POOL_WIDTH), lambda bb, i: (bb, jnp.minimum((i + 1) * hb, last), CB_POOL)),
            pl.BlockSpec((len(POOL_WINDOWS), POOL_CH, POOL_CH), lambda bb, i: (0, 0, 0)),
            pl.BlockSpec((1, POOL_WIDTH), lambda bb, i: (0, 0)),
        ],
        out_specs=pl.BlockSpec((1, tm, POOL_WIDTH), lambda bb, i: (bb, i, 0)),
        out_shape=jax.ShapeDtypeStruct((b, n, POOL_WIDTH), BF16),
        scratch_shapes=[pltpu.VMEM((tm + 2 * hl, POOL_WIDTH), F32)],
        compiler_params=_cparams(("arbitrary", "arbitrary")),
        name="pool_mixer",
    )(p, p, p, pool_w, pool_scale.reshape(1, POOL_WIDTH))


NA_QROWS = 4
NA_UROWS = 12
NA_UNROLL = 2


def _na_plan(rows, kr):
    rq, ku = NA_QROWS, NA_UROWS
    if rows < ku or rows % rq:
        rq, ku = 1, kr
    starts, keys = [], []
    for r0 in range(0, rows, rq):
        rs = [min(max(r - kr // 2, 0), rows - kr) for r in range(r0, r0 + rq)]
        us = min(rs[0], rows - ku)
        starts.append(us)
        keys.append(tuple((us - r, rs_q - us) for r, rs_q in zip(range(r0, r0 + rq), rs)))
    tables = sorted(set(keys))
    table_of = np.array([tables.index(k) for k in keys], np.int32)
    return rq, ku, np.array(starts, np.int32), table_of, tables


def _na_bias(rpb, kr, ku, tables):
    col = np.arange(GRID_W)
    col_start = np.clip(col - NA_COLS // 2, 0, GRID_W - NA_COLS)
    dcol = col[None, :] - col[:, None] + (NA_COLS - 1)
    ok = (col[None, :] >= col_start[:, None]) & (col[None, :] < col_start[:, None] + NA_COLS)
    onehot = ((dcol[:, :, None] == np.arange(2 * NA_COLS - 1)) & ok[:, :, None]).astype(np.float32)
    colbias = jnp.einsum("hrc,qkc->hrqk", rpb.astype(F32), onehot, precision=lax.Precision.HIGHEST)
    colbias = jnp.where(ok[None, None], colbias, NEG_BIG)
    h = rpb.shape[0]
    masked = jnp.full((h, GRID_W, GRID_W), NEG_BIG, F32)
    out = []
    for key in tables:
        per_q = []
        for rel0, first in key:
            blocks = [colbias[:, rel0 + i + NA_ROWS - 1] if first <= i < first + kr else masked
                      for i in range(ku)]
            per_q.append(jnp.stack(blocks, axis=2))
        out.append(jnp.stack(per_q, axis=1))
    rq = len(tables[0])
    return jnp.stack(out).reshape(len(tables), h, rq * GRID_W, ku * GRID_W)


def _na_kernel(us_ref, tb_ref, q_ref, k_ref, v_ref, kc_ref, vc_ref, bias_ref, o_ref, *, nb, mq, mk):
    i = pl.program_id(2)
    kc = kc_ref[0]
    vc = vc_ref[0]

    def body(bb, carry):
        blk = i * nb + bb
        koff = pl.multiple_of(us_ref[blk] * GRID_W, GRID_W)
        qoff = pl.multiple_of(bb * mq, mq)
        q = q_ref[0, pl.ds(qoff, mq), :]
        s_loc = _dot_nt(q, k_ref[0, pl.ds(koff, mk), :]) + bias_ref[tb_ref[blk], 0]
        s_ctx = _dot_nt(q, kc)
        m = jnp.maximum(jnp.max(s_loc, axis=-1, keepdims=True), jnp.max(s_ctx, axis=-1, keepdims=True))
        p_loc = jnp.exp(s_loc - m)
        p_ctx = jnp.exp(s_ctx - m)
        den = jnp.sum(p_loc, axis=-1, keepdims=True) + jnp.sum(p_ctx, axis=-1, keepdims=True)
        o = _dot(p_loc.astype(BF16), v_ref[0, pl.ds(koff, mk), :]) + _dot(p_ctx.astype(BF16), vc)
        o_ref[0, pl.ds(qoff, mq), :] = (o / den).astype(BF16)
        return carry

    lax.fori_loop(0, nb, body, 0, unroll=min(NA_UNROLL, nb))


def _na(p, pc, kvc0, rpb):
    b, n, _ = p.shape
    nc = pc.shape[1]
    rows = n // GRID_W
    kr = min(NA_ROWS, rows)
    rq, ku, starts, table_of, tables = _na_plan(rows, kr)
    bias = _na_bias(rpb, kr, ku, tables)
    nblk = rows // rq
    nb = min(4, nblk)
    assert nblk % nb == 0
    mq, mk = rq * GRID_W, ku * GRID_W
    return pl.pallas_call(
        functools.partial(_na_kernel, nb=nb, mq=mq, mk=mk),
        grid_spec=pltpu.PrefetchScalarGridSpec(
            num_scalar_prefetch=2, grid=(b, NA_HEADS, nblk // nb),
            in_specs=[
                pl.BlockSpec((1, nb * mq, HEAD_DIM), lambda bb, h, i, us, tb: (bb, i, HB_NA_Q + h)),
                pl.BlockSpec((1, n, HEAD_DIM), lambda bb, h, i, us, tb: (bb, 0, HB_NA_K + h)),
                pl.BlockSpec((1, n, HEAD_DIM), lambda bb, h, i, us, tb: (bb, 0, HB_NA_V + h)),
                pl.BlockSpec((1, nc, HEAD_DIM), lambda bb, h, i, us, tb: (bb, 0, kvc0 + h)),
                pl.BlockSpec((1, nc, HEAD_DIM), lambda bb, h, i, us, tb: (bb, 0, kvc0 + NA_HEADS + h)),
                pl.BlockSpec((len(tables), 1, mq, mk), lambda bb, h, i, us, tb: (0, h, 0, 0)),
            ],
            out_specs=pl.BlockSpec((1, nb * mq, HEAD_DIM), lambda bb, h, i, us, tb: (bb, i, h))),
        out_shape=jax.ShapeDtypeStruct((b, n, NA_WIDTH), BF16),
        compiler_params=_cparams(("arbitrary", "arbitrary", "arbitrary")),
        name="na_attn",
    )(jnp.asarray(starts), jnp.asarray(table_of), p, p, p, pc, pc, bias)


GQA_ONES_ROWS = 16


def _gqa_kernel(q0_ref, q1_ref, q2_ref, k_ref, v_ref, kc_ref, vc_ref, o_ref,
                qt_s, vt_s, vct_s, s_s, sc_s, m_s, acc_s, *, tk, n):
    i = pl.program_id(2)
    tq = q0_ref.shape[1]
    nchunk = n // tk
    hd = HEAD_DIM

    def to_t(a):
        return a.astype(F32).T.astype(BF16)

    @pl.when(i == 0)
    def _():
        def tr(c, carry):
            off = pl.multiple_of(c * tk, tk)
            vt_s[c, 0:hd, :] = to_t(v_ref[0, pl.ds(off, tk), :])
            vt_s[c, hd:, :] = jnp.ones((GQA_ONES_ROWS, tk), BF16)
            return carry
        lax.fori_loop(0, nchunk, tr, 0)
        vct_s[0:hd, :] = to_t(vc_ref[0])
        vct_s[hd:, :] = jnp.ones((GQA_ONES_ROWS, vct_s.shape[1]), BF16)

    for g, qr in enumerate((q0_ref, q1_ref, q2_ref)):
        qt_s[:, g * tq:(g + 1) * tq] = to_t(qr[0])

    def scores(c, slot):
        off = pl.multiple_of(c * tk, tk)
        s_s[slot] = _dot(k_ref[0, pl.ds(off, tk), :], qt_s[...])

    def update(s, vt, first=False):
        smax = jnp.max(s, axis=0, keepdims=True)
        if first:
            m_new = smax
        else:
            m_prev = m_s[...]
            m_new = jnp.maximum(m_prev, smax)
            alpha = jnp.exp(m_prev - m_new)
        p = jnp.exp(s - m_new).astype(BF16)
        pv = _dot(vt, p)
        acc_s[...] = pv if first else alpha * acc_s[...] + pv
        m_s[...] = m_new

    scores(0, 0)
    sc_s[...] = _dot(kc_ref[0], qt_s[...])
    update(sc_s[...], vct_s[...], first=True)

    def body(c2, carry):
        c = 2 * c2
        scores(c + 1, 1)
        update(s_s[0], vt_s[c])
        scores(jnp.minimum(c + 2, nchunk - 1), 0)
        update(s_s[1], vt_s[c + 1])
        return carry

    lax.fori_loop(0, nchunk // 2, body, 0)
    acc = acc_s[...]
    o = (acc[0:hd] / acc[hd:hd + 1]).T
    for g in range(GQA_GROUP):
        o_ref[0, :, g * hd:(g + 1) * hd] = o[g * tq:(g + 1) * tq].astype(BF16)


def _gqa(p, pc, kvc0):
    b, n, _ = p.shape
    nc = pc.shape[1]
    tq = min(256, n)
    tk = min(512, n // 2)
    assert n % (2 * tk) == 0
    kc0 = kvc0 + 2 * NA_HEADS
    nq = GQA_GROUP * tq
    vr = HEAD_DIM + GQA_ONES_ROWS

    def qspec(g):
        return pl.BlockSpec((1, tq, HEAD_DIM), lambda bb, h, i: (bb, i, HB_GQA_Q + h * GQA_GROUP + g))

    return pl.pallas_call(
        functools.partial(_gqa_kernel, tk=tk, n=n),
        grid=(b, GQA_KV_HEADS, n // tq),
        in_specs=[
            qspec(0), qspec(1), qspec(2),
            pl.BlockSpec((1, n, HEAD_DIM), lambda bb, h, i: (bb, 0, HB_GQA_K + h)),
            pl.BlockSpec((1, n, HEAD_DIM), lambda bb, h, i: (bb, 0, HB_GQA_V + h)),
            pl.BlockSpec((1, nc, HEAD_DIM), lambda bb, h, i: (bb, 0, kc0 + h)),
            pl.BlockSpec((1, nc, HEAD_DIM), lambda bb, h, i: (bb, 0, kc0 + GQA_KV_HEADS + h)),
        ],
        out_specs=pl.BlockSpec((1, tq, GQA_GROUP * HEAD_DIM), lambda bb, h, i: (bb, i, h)),
        out_shape=jax.ShapeDtypeStruct((b, n, GQA_Q_WIDTH), BF16),
        scratch_shapes=[
            pltpu.VMEM((HEAD_DIM, nq), BF16),
            pltpu.VMEM((n // tk, vr, tk), BF16),
            pltpu.VMEM((vr, nc), BF16),
            pltpu.VMEM((2, tk, nq), F32),
            pltpu.VMEM((nc, nq), F32),
            pltpu.VMEM((1, nq), F32),
            pltpu.VMEM((vr, nq), F32),
        ],
        compiler_params=_cparams(("arbitrary", "arbitrary", "arbitrary")),
        name="gqa_attn",
    )(p, p, p, p, p, pc, pc)


def _ctx_attn_kernel(q_ref, k_ref, v_ref, o_ref):
    s = _dot_nt(q_ref[0], k_ref[0])
    m = jnp.max(s, axis=-1, keepdims=True)
    p = jnp.exp(s - m)
    den = jnp.sum(p, axis=-1, keepdims=True)
    o_ref[0] = (_dot(p.astype(BF16), v_ref[0]) / den).astype(BF16)


def _ctx_attn(pc):
    b, nc, _ = pc.shape
    nh = NA_HEADS + GQA_Q_HEADS

    def kmap(bb, h):
        g = jnp.maximum(h - NA_HEADS, 0) // GQA_GROUP
        return bb, 0, jnp.where(h < NA_HEADS, HB_NA_K + h, HB_GQA_K + g)

    def vmap_(bb, h):
        g = jnp.maximum(h - NA_HEADS, 0) // GQA_GROUP
        return bb, 0, jnp.where(h < NA_HEADS, HB_NA_V + h, HB_GQA_V + g)

    return pl.pallas_call(
        _ctx_attn_kernel,
        grid=(b, nh),
        in_specs=[
            pl.BlockSpec((1, nc, HEAD_DIM), lambda bb, h: (bb, 0, HB_NA_Q + h)),
            pl.BlockSpec((1, nc, HEAD_DIM), kmap),
            pl.BlockSpec((1, nc, HEAD_DIM), vmap_),
        ],
        out_specs=pl.BlockSpec((1, nc, HEAD_DIM), lambda bb, h: (bb, 0, h)),
        out_shape=jax.ShapeDtypeStruct((b, nc, nh * HEAD_DIM), BF16),
        compiler_params=_cparams(("arbitrary", "arbitrary")),
        name="ctx_attn",
    )(pc, pc, pc)


def _merge_kernel(x_ref, ga_ref, gb_ref, gc_ref, yp_ref, yn_ref, yg_ref, wbr_ref, wout_ref, g1_ref, o_ref):
    r1 = POOL_WIDTH
    r2 = POOL_WIDTH + NA_WIDTH
    z = ga_ref[0].astype(F32) * _dot(yp_ref[0], wbr_ref[0:r1, :])
    z = z + gb_ref[0].astype(F32) * _dot(yn_ref[0], wbr_ref[r1:r2, :])
    z = z + gc_ref[0].astype(F32) * _dot(yg_ref[0], wbr_ref[r2:, :])
    o_ref[0] = x_ref[0] + g1_ref[0] * _dot(z.astype(BF16), wout_ref[...])


def _merge(x, p, y_pool, y_na, na_cb, y_gqa, gqa_cb, w_br, w_out, g1):
    b, n, d = x.shape
    tm = min(256, n)
    const = lambda bb, i: (0, 0)
    return pl.pallas_call(
        _merge_kernel,
        grid=(b, n // tm),
        in_specs=[
            pl.BlockSpec((1, tm, d), lambda bb, i: (bb, i, 0)),
            pl.BlockSpec((1, tm, d), lambda bb, i: (bb, i, 0)),
            pl.BlockSpec((1, tm, d), lambda bb, i: (bb, i, 1)),
            pl.BlockSpec((1, tm, d), lambda bb, i: (bb, i, 2)),
            pl.BlockSpec((1, tm, POOL_WIDTH), lambda bb, i: (bb, i, 0)),
            pl.BlockSpec((1, tm, NA_WIDTH), lambda bb, i: (bb, i, na_cb)),
            pl.BlockSpec((1, tm, GQA_Q_WIDTH), lambda bb, i: (bb, i, gqa_cb)),
            pl.BlockSpec(w_br.shape, const, pipeline_mode=pl.Buffered(1)),
            pl.BlockSpec(w_out.shape, const, pipeline_mode=pl.Buffered(1)),
            pl.BlockSpec((1, 1, d), lambda bb, i: (bb, 0, 0)),
        ],
        out_specs=pl.BlockSpec((1, tm, d), lambda bb, i: (bb, i, 0)),
        out_shape=jax.ShapeDtypeStruct((b, n, d), F32),
        compiler_params=_cparams(("arbitrary", "arbitrary")),
        name="branch_merge",
    )(x, p, p, p, y_pool, y_na, y_gqa, w_br, w_out, g1)


def _top2_of4(a, b, c, d):
    hi1, lo1 = jnp.maximum(a, b), jnp.minimum(a, b)
    hi2, lo2 = jnp.maximum(c, d), jnp.minimum(c, d)
    return jnp.maximum(hi1, hi2) + jnp.maximum(jnp.minimum(hi1, hi2), jnp.maximum(lo1, lo2))


def _router_kernel(x_ref, g_ref, sc_ref, sh_ref, whi_ref, wlo_ref, br_ref,
                   h_ref, e_ref, w_ref, rank_ref, cnt_ref, carry_ref):
    first = (pl.program_id(0) == 0) & (pl.program_id(1) == 0)

    @pl.when(first)
    def _():
        carry_ref[...] = jnp.zeros_like(carry_ref)

    x = x_ref[0]
    h = x * lax.rsqrt(jnp.mean(x * x, axis=-1, keepdims=True) + EPS) * g_ref[...]
    h = h * (1.0 + sc_ref[0]) + sh_ref[0]
    h_ref[0] = h
    h_hi = h.astype(BF16)
    h_lo = (h - h_hi.astype(F32)).astype(BF16)
    whi = whi_ref[...]
    logit = _dot_nt(whi, h_hi) + _dot_nt(whi, h_lo) + _dot_nt(wlo_ref[...], h_hi)
    s = jax.nn.sigmoid(logit)
    sel = s + br_ref[...]
    epg = EXPERTS_PER_GROUP
    row = lambda a, e: a[e:e + 1, :]
    gscore = [_top2_of4(*[row(sel, g * epg + j) for j in range(epg)]) for g in range(N_GROUPS)]
    g_best = jnp.zeros_like(gscore[0], dtype=jnp.int32)
    best = gscore[0]
    for g in range(1, N_GROUPS):
        upd = gscore[g] > best
        g_best = jnp.where(upd, g, g_best)
        best = jnp.where(upd, gscore[g], best)
    vs, ss = [], []
    for j in range(epg):
        v = row(sel, j)
        sv = row(s, j)
        for g in range(1, N_GROUPS):
            v = jnp.where(g_best == g, row(sel, g * epg + j), v)
            sv = jnp.where(g_best == g, row(s, g * epg + j), sv)
        vs.append(v)
        ss.append(sv)
    i1 = jnp.zeros_like(g_best)
    v1 = vs[0]
    for j in range(1, epg):
        upd = vs[j] > v1
        i1 = jnp.where(upd, j, i1)
        v1 = jnp.where(upd, vs[j], v1)
    i2 = jnp.full_like(g_best, -1)
    v2 = jnp.full_like(v1, -jnp.inf)
    for j in range(epg):
        upd = (i1 != j) & ((i2 < 0) | (vs[j] > v2))
        i2 = jnp.where(upd, j, i2)
        v2 = jnp.where(upd, vs[j], v2)
    w1 = sum(jnp.where(i1 == j, ss[j], 0.0) for j in range(epg))
    w2 = sum(jnp.where(i2 == j, ss[j], 0.0) for j in range(epg))
    tot = w1 + w2
    w_ref[0] = jnp.concatenate([w1 / tot, w2 / tot], axis=0)
    e1 = g_best * epg + i1
    e2 = g_best * epg + i2
    e_ref[0] = jnp.concatenate([e1, e2], axis=0)

    tm = x.shape[0]
    eidx = lax.broadcasted_iota(jnp.int32, (N_EXPERTS, tm), 0)
    oh1 = eidx == e1
    oh2 = eidx == e2
    oh = jnp.where(oh1 | oh2, 1.0, 0.0)
    before = lax.broadcasted_iota(jnp.int32, (tm, tm), 0) < lax.broadcasted_iota(jnp.int32, (tm, tm), 1)
    prefix = _dot(oh.astype(BF16), jnp.where(before, 1.0, 0.0).astype(BF16))
    base = carry_ref[...] + prefix
    r1 = jnp.sum(jnp.where(oh1, base, 0.0), axis=0, keepdims=True)
    r2 = jnp.sum(jnp.where(oh2, base, 0.0), axis=0, keepdims=True)
    rank_ref[0] = jnp.concatenate([r1, r2], axis=0).astype(jnp.int32)
    carry = carry_ref[...] + jnp.sum(oh, axis=1, keepdims=True)
    carry_ref[...] = carry
    cnt_ref[...] = jnp.broadcast_to(carry, cnt_ref.shape).astype(jnp.int32)


def _router(x, gain, sc, sh, wr_hi, wr_lo, b_router):
    b, n, d = x.shape
    tm = min(512, n)
    ne = wr_hi.shape[0]
    pair = pl.BlockSpec((1, 2, tm), lambda bb, i: (bb, 0, i))
    return pl.pallas_call(
        _router_kernel,
        grid=(b, n // tm),
        in_specs=[
            pl.BlockSpec((1, tm, d), lambda bb, i: (bb, i, 0)),
            pl.BlockSpec((1, d), lambda bb, i: (0, 0)),
            pl.BlockSpec((1, 1, d), lambda bb, i: (bb, 0, 0)),
            pl.BlockSpec((1, 1, d), lambda bb, i: (bb, 0, 0)),
            pl.BlockSpec((ne, d), lambda bb, i: (0, 0)),
            pl.BlockSpec((ne, d), lambda bb, i: (0, 0)),
            pl.BlockSpec((ne, 1), lambda bb, i: (0, 0)),
        ],
        out_specs=[
            pl.BlockSpec((1, tm, d), lambda bb, i: (bb, i, 0)),
            pair, pair, pair,
            pl.BlockSpec((ne, HEAD_DIM), lambda bb, i: (0, 0)),
        ],
        out_shape=[
            jax.ShapeDtypeStruct((b, n, d), F32),
            jax.ShapeDtypeStruct((b, 2, n), jnp.int32),
            jax.ShapeDtypeStruct((b, 2, n), F32),
            jax.ShapeDtypeStruct((b, 2, n), jnp.int32),
            jax.ShapeDtypeStruct((ne, HEAD_DIM), jnp.int32),
        ],
        scratch_shapes=[pltpu.VMEM((ne, 1), F32)],
        compiler_params=_cparams(("arbitrary", "arbitrary")),
        name="norm_router",
    )(x, gain.reshape(1, d), sc, sh, wr_hi, wr_lo, b_router.reshape(ne, 1))


MOE_TILE = 256
MOE_SCATTER_TILE = 512
MOE_COMBINE_TILE = 256
MOE_DMA_UNROLL = 8


def _moe_plan(e, rank, cnt):
    b, _, n = e.shape
    t = b * n
    counts = cnt[:, 0]
    ntile_e = (counts + MOE_TILE - 1) // MOE_TILE
    tile_end = jnp.cumsum(ntile_e)
    off = (tile_end - ntile_e) * MOE_TILE
    nt = 2 * t // MOE_TILE + N_EXPERTS
    tile_expert = jnp.sum(jnp.arange(nt)[:, None] >= tile_end[None, :], axis=1)
    tile_expert = jnp.minimum(tile_expert, N_EXPERTS - 1).astype(jnp.int32)
    ef = e.transpose(1, 0, 2).reshape(2, t)
    rf = rank.transpose(1, 0, 2).reshape(2, t)
    pos = rf + jnp.sum(jnp.where(ef[..., None] == jnp.arange(N_EXPERTS), off, 0), axis=-1)
    return pos.reshape(2 * t).astype(jnp.int32), tile_expert, tile_end[-1:].astype(jnp.int32), nt


def _scatter_kernel(pos_ref, h_ref, xs0_hbm, xs_hbm, sem, *, t):
    del xs0_hbm
    tm = h_ref.shape[0]
    base = pl.program_id(0) * tm

    def copies(j):
        return [pltpu.make_async_copy(h_ref.at[pl.ds(j, 1)], xs_hbm.at[pl.ds(pos_ref[k * t + base + j], 1)], sem)
                for k in range(2)]

    def start(j, c):
        for cp in copies(j):
            cp.start()
        return c

    def wait(j, c):
        for cp in copies(j):
            cp.wait()
        return c

    lax.fori_loop(0, tm, start, 0, unroll=MOE_DMA_UNROLL)
    lax.fori_loop(0, tm, wait, 0, unroll=MOE_DMA_UNROLL)


def _scatter_rows(pos, h, nrows):
    t, d = h.shape
    tm = min(MOE_SCATTER_TILE, t)
    any_spec = pl.BlockSpec(memory_space=pl.ANY)
    return pl.pallas_call(
        functools.partial(_scatter_kernel, t=t),
        grid_spec=pltpu.PrefetchScalarGridSpec(
            num_scalar_prefetch=1, grid=(t // tm,),
            in_specs=[pl.BlockSpec((tm, d), lambda i, p: (i, 0)), any_spec], out_specs=any_spec,
            scratch_shapes=[pltpu.SemaphoreType.DMA(())]),
        out_shape=jax.ShapeDtypeStruct((nrows, d), F32),
        input_output_aliases={2: 0},
        compiler_params=pltpu.CompilerParams(dimension_semantics=("arbitrary",), has_side_effects=True),
        name="moe_scatter",
    )(pos, h, jnp.zeros((nrows, d), F32))


def _experts_kernel(te_ref, nv_ref, xs_ref, wgu_ref, wd_ref, ys_ref):
    del te_ref

    @pl.when(pl.program_id(0) < nv_ref[0])
    def _():
        gu = _dot(xs_ref[...].astype(BF16), wgu_ref[0])
        ff = gu.shape[1] // 2
        gate = gu[:, :ff]
        a = (gate * jax.nn.sigmoid(gate) * gu[:, ff:]).astype(BF16)
        ys_ref[...] = _dot(a, wd_ref[0])


def _experts(xs, tile_expert, nvalid, w_gu, w_down):
    nrows, d = xs.shape
    nt = nrows // MOE_TILE
    ne, _, f2 = w_gu.shape
    row = lambda i, te, nv: (jnp.minimum(i, nv[0] - 1), 0)
    wmap = lambda i, te, nv: (te[jnp.minimum(i, nv[0] - 1)], 0, 0)
    return pl.pallas_call(
        _experts_kernel,
        grid_spec=pltpu.PrefetchScalarGridSpec(
            num_scalar_prefetch=2, grid=(nt,),
            in_specs=[
                pl.BlockSpec((MOE_TILE, d), row),
                pl.BlockSpec((1, d, f2), wmap),
                pl.BlockSpec((1, f2 // 2, d), wmap),
            ],
            out_specs=pl.BlockSpec((MOE_TILE, d), row)),
        out_shape=jax.ShapeDtypeStruct((nrows, d), F32),
        compiler_params=_cparams(("arbitrary",)),
        name="moe_experts",
    )(tile_expert, nvalid, xs, w_gu, w_down)


def _combine_kernel(pos_ref, x_ref, w_ref, g2_ref, ys_hbm, o_ref, buf, sem, *, t):
    i = pl.program_id(0)
    nsteps = pl.num_programs(0)
    tm = x_ref.shape[0]

    def copies(tile, slot, j):
        tok = tile * tm + j
        return [pltpu.make_async_copy(ys_hbm.at[pl.ds(pos_ref[k * t + tok], 1)],
                                      buf.at[slot, k, pl.ds(j, 1)], sem.at[slot]) for k in range(2)]

    def issue(tile, slot):
        def body(j, c):
            for cp in copies(tile, slot, j):
                cp.start()
            return c
        lax.fori_loop(0, tm, body, 0, unroll=MOE_DMA_UNROLL)

    def wait(tile, slot):
        def body(j, c):
            for cp in copies(tile, slot, j):
                cp.wait()
            return c
        lax.fori_loop(0, tm, body, 0, unroll=MOE_DMA_UNROLL)

    @pl.when(i == 0)
    def _():
        issue(0, 0)

    @pl.when(i + 1 < nsteps)
    def _():
        issue(i + 1, (i + 1) % 2)

    slot = i % 2
    wait(i, slot)
    w = w_ref[...]
    y = w[:, 0:1] * buf[slot, 0] + w[:, 1:2] * buf[slot, 1]
    o_ref[...] = x_ref[...] + g2_ref[0] * y


def _combine(pos, x, w, g2, ys):
    b, n, d = x.shape
    t = b * n
    tm = min(MOE_COMBINE_TILE, n)
    per_b = n // tm
    out = pl.pallas_call(
        functools.partial(_combine_kernel, t=t),
        grid_spec=pltpu.PrefetchScalarGridSpec(
            num_scalar_prefetch=1, grid=(t // tm,),
            in_specs=[
                pl.BlockSpec((tm, d), lambda i, p: (i, 0)),
                pl.BlockSpec((tm, 2), lambda i, p: (i, 0)),
                pl.BlockSpec((1, 1, d), lambda i, p: (i // per_b, 0, 0)),
                pl.BlockSpec(memory_space=pl.ANY),
            ],
            out_specs=pl.BlockSpec((tm, d), lambda i, p: (i, 0)),
            scratch_shapes=[pltpu.VMEM((2, 2, tm, d), F32), pltpu.SemaphoreType.DMA((2,))]),
        out_shape=jax.ShapeDtypeStruct((t, d), F32),
        compiler_params=_cparams(("arbitrary",)),
        name="moe_combine",
    )(pos, x.reshape(t, d), w, g2, ys)
    return out.reshape(b, n, d)


def _moe(x, gain, sc, sh, g2, wr_hi, wr_lo, b_router, w_gu, w_down):
    b, n, d = x.shape
    h, e, w, rank, cnt = _router(x, gain, sc, sh, wr_hi, wr_lo, b_router)
    pos, tile_expert, nvalid, nt = _moe_plan(e, rank, cnt)
    xs = _scatter_rows(pos, h.reshape(b * n, d), nt * MOE_TILE)
    ys = _experts(xs, tile_expert, nvalid, w_gu, w_down)
    return _combine(pos, x, w.transpose(0, 2, 1).reshape(b * n, 2), g2, ys)


def _rope_tables(n):
    t = jnp.arange(n, dtype=jnp.int32)
    row = (t // GRID_W).astype(F32)
    col = (t % GRID_W).astype(F32)
    axis_dim = HEAD_DIM // 2
    inv = ROPE_THETA ** (-jnp.arange(0, axis_dim, 2, dtype=F32) / axis_dim)
    ang = jnp.concatenate([row[:, None] * inv, col[:, None] * inv], axis=-1)
    cos, sin = jnp.cos(ang), jnp.sin(ang)
    return jnp.concatenate([cos, cos], axis=-1), jnp.concatenate([-sin, sin], axis=-1)


def kernel(x, c, ctx, c_ctx, w_mod, b_mod, norm1, norm2, w_in, qk_gain, pool_w, pool_scale,
           na_rpb, w_br, w_out, w_router, b_router, w_gu, w_down):
    b, n, d = x.shape
    depth = w_mod.shape[0]
    rows = n // GRID_W
    kr = min(NA_ROWS, rows)
    assert n % GRID_W == 0 and rows % kr == 0 and b + 1 <= 8

    cos, sin = _rope_tables(n)
    mods = _modulation(jnp.concatenate([c, c_ctx[None, :]], axis=0), w_mod, b_mod)
    wr_t = w_router.T
    wr_hi = wr_t.astype(BF16)
    wr_lo = (wr_t - wr_hi.astype(F32)).astype(BF16)

    for l in range(depth):
        last = l == depth - 1
        mx = mods[l, :b].reshape(b, 1, N_MOD, d)
        mc = jnp.broadcast_to(mods[l, b].reshape(1, 1, N_MOD, d), (b, 1, N_MOD, d))
        x_sh1, x_sc1, x_g1, x_sh2, x_sc2, x_g2 = [mx[:, :, k] for k in range(N_MOD)]
        c_sh1, c_sc1, c_g1, c_sh2, c_sc2, c_g2 = [mc[:, :, k] for k in range(N_MOD)]
        w_in_l = w_in[l].astype(BF16)
        w_br_l = w_br[l].astype(BF16)
        w_out_l = w_out[l].astype(BF16)
        pool_w_l = pool_w[l].astype(BF16)

        pc = _inproj(ctx, norm1[l], c_sc1, c_sh1, w_in_l, qk_gain[l], None, None, kv_only=last)
        kvc0 = 0 if last else HB_KV0
        px = _inproj(x, norm1[l], x_sc1, x_sh1, w_in_l, qk_gain[l], cos, sin)
        y_pool = _pool(px, pool_w_l, pool_scale[l])
        y_na = _na(px, pc, kvc0, na_rpb[l])
        y_gqa = _gqa(px, pc, kvc0)
        x = _merge(x, px, y_pool, y_na, 0, y_gqa, 0, w_br_l, w_out_l, x_g1)

        w_gu_l = w_gu[l].astype(BF16)
        w_down_l = w_down[l].astype(BF16)
        if not last:
            yc_pool = _pool(pc, pool_w_l, pool_scale[l])
            yc = _ctx_attn(pc)
            ctx = _merge(ctx, pc, yc_pool, yc, 0, yc, 1, w_br_l, w_out_l, c_g1)
            ctx = _moe(ctx, norm2[l], c_sc2, c_sh2, c_g2, wr_hi, wr_lo, b_router, w_gu_l, w_down_l)
        x = _moe(x, norm2[l], x_sc2, x_sh2, x_g2, wr_hi, wr_lo, b_router, w_gu_l, w_down_l)
    return x
```

```python
import functools

import numpy as np
import jax
import jax.numpy as jnp
from jax import lax
from jax.experimental import pallas as pl
from jax.experimental.pallas import tpu as pltpu

F32 = jnp.float32
BF16 = jnp.bfloat16

GRID_W = 64
HEAD_DIM = 128
ROPE_THETA = 10000.0
EPS = 1e-6
POOL_WINDOWS = (2, 4, 8, 16)
POOL_CH = 128
POOL_WIDTH = len(POOL_WINDOWS) * POOL_CH
NA_HEADS = 6
NA_WIDTH = NA_HEADS * HEAD_DIM
NA_ROWS = 8
NA_COLS = 16
GQA_Q_HEADS = 6
GQA_KV_HEADS = 2
GQA_GROUP = GQA_Q_HEADS // GQA_KV_HEADS
GQA_Q_WIDTH = GQA_Q_HEADS * HEAD_DIM
N_BRANCH = 3
N_EXPERTS = 16
N_GROUPS = 4
EXPERTS_PER_GROUP = N_EXPERTS // N_GROUPS
N_MOD = 6
ATTN_SCALE = HEAD_DIM ** -0.5

COL_BLOCK = 2048
CB_POOL = 0
HB_NA_Q = POOL_WIDTH // HEAD_DIM
HB_GQA_Q = HB_NA_Q + NA_HEADS
HB_NA_K = 0
HB_NA_V = HB_NA_K + NA_HEADS
HB_GQA_K = HB_NA_V + NA_HEADS
HB_GQA_V = HB_GQA_K + GQA_KV_HEADS
PROJ_KINDS = {"gates": (0, 3), "q": (3, 1), "kv": (4, 1)}

V7X_VMEM_LIMIT = 56 * 1024 * 1024
NEG_BIG = -1e30


def _cparams(sem):
    return pltpu.CompilerParams(dimension_semantics=sem, vmem_limit_bytes=V7X_VMEM_LIMIT)


def _dot(a, b):
    return jnp.dot(a, b, preferred_element_type=F32)


def _dot_nt(a, b):
    return lax.dot_general(a, b, (((1,), (1,)), ((), ())), preferred_element_type=F32)


def _mod_kernel(ct_ref, w_ref, b_ref, o_ref, *, n_rows):
    ct = ct_ref[...]
    a = ct * jax.nn.sigmoid(ct)
    w = w_ref[0]
    rows = [jnp.sum(w * a[:, r:r + 1], axis=0, keepdims=True) for r in range(n_rows)]
    rows += [jnp.zeros_like(rows[0])] * (8 - n_rows)
    o_ref[0] = jnp.concatenate(rows, axis=0) + b_ref[0]


def _modulation(c_rows, w_mod, b_mod):
    depth, d, nm = w_mod.shape
    n_rows = c_rows.shape[0]
    ct = jnp.zeros((d, 8), F32).at[:, :n_rows].set(c_rows.T)
    tn = 1024
    return pl.pallas_call(
        functools.partial(_mod_kernel, n_rows=n_rows),
        grid=(depth, nm // tn),
        in_specs=[
            pl.BlockSpec((d, 8), lambda l, j: (0, 0)),
            pl.BlockSpec((1, d, tn), lambda l, j: (l, 0, j)),
            pl.BlockSpec((1, 1, tn), lambda l, j: (l, 0, j)),
        ],
        out_specs=pl.BlockSpec((1, 8, tn), lambda l, j: (l, 0, j)),
        out_shape=jax.ShapeDtypeStruct((depth, 8, nm), F32),
        compiler_params=_cparams(("arbitrary", "arbitrary")),
        name="adaln_mod",
    )(ct, w_mod, b_mod.reshape(depth, 1, nm))


def _head_norm(a, gain):
    return a * lax.rsqrt(jnp.mean(a * a, axis=-1, keepdims=True) + EPS) * gain


def _rope(y, cos, sin):
    return y * cos + pltpu.roll(y, HEAD_DIM // 2, 1) * sin


PROJ_CHUNK = 2 * HEAD_DIM


def _proj_epilogue(kind, head, a, qg_ref, rope_tables):
    if kind == "gates":
        return jax.nn.sigmoid(a)
    if kind == "q":
        if head < HB_NA_Q:
            return a
        if head < HB_GQA_Q:
            return _head_norm(a, qg_ref[0:1, :]) * ATTN_SCALE
        y = _head_norm(a, qg_ref[2:3, :])
        return (_rope(y, *rope_tables) if rope_tables else y) * ATTN_SCALE
    if head < HB_NA_V:
        return _head_norm(a, qg_ref[1:2, :])
    if HB_GQA_K <= head < HB_GQA_V:
        y = _head_norm(a, qg_ref[3:4, :])
        return _rope(y, *rope_tables) if rope_tables else y
    return a


def _inproj_kernel(*refs, kind, rope):
    if rope:
        x_ref, g_ref, sc_ref, sh_ref, w_ref, qg_ref, cos_ref, sin_ref, o_ref, wb_s = refs
    else:
        x_ref, g_ref, sc_ref, sh_ref, w_ref, qg_ref, o_ref, wb_s = refs

    @pl.when((pl.program_id(1) == 0) & (pl.program_id(2) == 0))
    def _():
        wb_s[...] = w_ref[0].astype(BF16)

    x = x_ref[0]
    h = x * lax.rsqrt(jnp.mean(x * x, axis=-1, keepdims=True) + EPS) * g_ref[...]
    hb = (h * (1.0 + sc_ref[0]) + sh_ref[0]).astype(BF16)
    tables = (cos_ref[...], sin_ref[...]) if rope else None
    per = PROJ_CHUNK // HEAD_DIM
    for c in range(COL_BLOCK // PROJ_CHUNK):
        acc = _dot(hb, wb_s[:, c * PROJ_CHUNK:(c + 1) * PROJ_CHUNK])
        for u in range(per):
            head = c * per + u
            y = _proj_epilogue(kind, head, acc[:, u * HEAD_DIM:(u + 1) * HEAD_DIM], qg_ref, tables)
            o_ref[0, :, head * HEAD_DIM:(head + 1) * HEAD_DIM] = y.astype(BF16)


def _inproj(x, gain, sc, sh, w_in, layer, qgain, cos, sin, *, kind):
    b, n, d = x.shape
    rope = cos is not None and kind != "gates"
    tm = min(512, n)
    j0, nj = PROJ_KINDS[kind]
    in_specs = [
        pl.BlockSpec((1, tm, d), lambda j, bb, i: (bb, i, 0)),
        pl.BlockSpec((1, d), lambda j, bb, i: (0, 0)),
        pl.BlockSpec((1, 1, d), lambda j, bb, i: (bb, 0, 0)),
        pl.BlockSpec((1, 1, d), lambda j, bb, i: (bb, 0, 0)),
        pl.BlockSpec((1, d, COL_BLOCK), lambda j, bb, i: (layer, 0, j + j0), pipeline_mode=pl.Buffered(1)),
        pl.BlockSpec((4, HEAD_DIM), lambda j, bb, i: (0, 0)),
    ]
    args = [x, gain.reshape(1, d), sc, sh, w_in, qgain]
    if rope:
        in_specs += [pl.BlockSpec((tm, HEAD_DIM), lambda j, bb, i: (i, 0))] * 2
        args += [cos, sin]
    return pl.pallas_call(
        functools.partial(_inproj_kernel, kind=kind, rope=rope),
        grid=(nj, b, n // tm),
        in_specs=in_specs,
        out_specs=pl.BlockSpec((1, tm, COL_BLOCK), lambda j, bb, i: (bb, i, j)),
        out_shape=jax.ShapeDtypeStruct((b, n, nj * COL_BLOCK), BF16),
        scratch_shapes=[pltpu.VMEM((d, COL_BLOCK), BF16)],
        compiler_params=_cparams(("arbitrary", "arbitrary", "arbitrary")),
        name="inproj_" + kind + ("_rope" if rope else ""),
    )(*args)


POOL_HALO = 16


def _pool_kernel(prev_ref, cur_ref, next_ref, w_ref, s_ref, o_ref, buf_ref, *, tm, n):
    i = pl.program_id(1)
    nt = pl.num_programs(1)
    hl = POOL_HALO
    buf_ref[pl.ds(hl, tm), :] = cur_ref[0].astype(F32)
    buf_ref[pl.ds(0, hl), :] = jnp.where(i > 0, prev_ref[0].astype(F32), 0.0)
    buf_ref[pl.ds(hl + tm, hl), :] = jnp.where(i < nt - 1, next_ref[0].astype(F32), 0.0)
    t = i * tm + lax.broadcasted_iota(jnp.int32, (tm, 1), 0)
    for g, w in enumerate(POOL_WINDOWS):
        sl = slice(g * POOL_CH, (g + 1) * POOL_CH)
        acc = buf_ref[pl.ds(hl - w // 2, tm), sl]
        for off in range(-w // 2 + 1, w // 2):
            acc = acc + buf_ref[pl.ds(hl + off, tm), sl]
        cnt = (jnp.minimum(t + w // 2, n) - jnp.maximum(t - w // 2, 0)).astype(F32)
        dlt = acc / cnt - buf_ref[pl.ds(hl, tm), sl]
        y = _dot(dlt.astype(BF16), w_ref[g]) * s_ref[:, sl]
        o_ref[0, :, sl] = y.astype(BF16)


def _pool(p, pool_w, pool_scale):
    b, n, _ = p.shape
    tm = min(512, n)
    hl = POOL_HALO
    hb = tm // hl
    last = n // hl - 1
    return pl.pallas_call(
        functools.partial(_pool_kernel, tm=tm, n=n),
        grid=(b, n // tm),
        in_specs=[
            pl.BlockSpec((1, hl, POOL_WIDTH), lambda bb, i: (bb, jnp.maximum(i * hb - 1, 0), CB_POOL)),
            pl.BlockSpec((1, tm, POOL_WIDTH), lambda bb, i: (bb, i, CB_POOL)),
            pl.BlockSpec((1, hl, POOL_WIDTH), lambda bb, i: (bb, jnp.minimum((i + 1) * hb, last), CB_POOL)),
            pl.BlockSpec((len(POOL_WINDOWS), POOL_CH, POOL_CH), lambda bb, i: (0, 0, 0)),
            pl.BlockSpec((1, POOL_WIDTH), lambda bb, i: (0, 0)),
        ],
        out_specs=pl.BlockSpec((1, tm, POOL_WIDTH), lambda bb, i: (bb, i, 0)),
        out_shape=jax.ShapeDtypeStruct((b, n, POOL_WIDTH), BF16),
        scratch_shapes=[pltpu.VMEM((tm + 2 * hl, POOL_WIDTH), F32)],
        compiler_params=_cparams(("arbitrary", "arbitrary")),
        name="pool_mixer",
    )(p, p, p, pool_w, pool_scale.reshape(1, POOL_WIDTH))


NA_QROWS = 4
NA_UROWS = 12
NA_UNROLL = 2


def _na_plan(rows, kr):
    rq, ku = NA_QROWS, NA_UROWS
    if rows < ku or rows % rq:
        rq, ku = 1, kr
    starts, keys = [], []
    for r0 in range(0, rows, rq):
        rs = [min(max(r - kr // 2, 0), rows - kr) for r in range(r0, r0 + rq)]
        us = min(rs[0], rows - ku)
        starts.append(us)
        keys.append(tuple((us - r, rs_q - us) for r, rs_q in zip(range(r0, r0 + rq), rs)))
    tables = sorted(set(keys))
    table_of = np.array([tables.index(k) for k in keys], np.int32)
    return rq, ku, np.array(starts, np.int32), table_of, tables


def _na_bias(rpb, kr, ku, tables):
    col = np.arange(GRID_W)
    col_start = np.clip(col - NA_COLS // 2, 0, GRID_W - NA_COLS)
    dcol = col[None, :] - col[:, None] + (NA_COLS - 1)
    ok = (col[None, :] >= col_start[:, None]) & (col[None, :] < col_start[:, None] + NA_COLS)
    onehot = ((dcol[:, :, None] == np.arange(2 * NA_COLS - 1)) & ok[:, :, None]).astype(np.float32)
    colbias = jnp.einsum("hrc,qkc->hrqk", rpb.astype(F32), onehot, precision=lax.Precision.HIGHEST)
    colbias = jnp.where(ok[None, None], colbias, NEG_BIG)
    h = rpb.shape[0]
    masked = jnp.full((h, GRID_W, GRID_W), NEG_BIG, F32)
    out = []
    for key in tables:
        per_q = []
        for rel0, first in key:
            blocks = [colbias[:, rel0 + i + NA_ROWS - 1] if first <= i < first + kr else masked
                      for i in range(ku)]
            per_q.append(jnp.stack(blocks, axis=2))
        out.append(jnp.stack(per_q, axis=1))
    rq = len(tables[0])
    return jnp.stack(out).reshape(len(tables), h, rq * GRID_W, ku * GRID_W)


def _na_kernel(us_ref, tb_ref, q_ref, k_ref, v_ref, kc_ref, vc_ref, bias_ref, o_ref, *, nb, mq, mk):
    i = pl.program_id(2)
    kc = kc_ref[0]
    vc = vc_ref[0]

    def body(bb, carry):
        blk = i * nb + bb
        koff = pl.multiple_of(us_ref[blk] * GRID_W, GRID_W)
        qoff = pl.multiple_of(bb * mq, mq)
        q = q_ref[0, pl.ds(qoff, mq), :]
        s_loc = _dot_nt(q, k_ref[0, pl.ds(koff, mk), :]) + bias_ref[tb_ref[blk], 0]
        s_ctx = _dot_nt(q, kc)
        m = jnp.maximum(jnp.max(s_loc, axis=-1, keepdims=True), jnp.max(s_ctx, axis=-1, keepdims=True))
        p_loc = jnp.exp(s_loc - m)
        p_ctx = jnp.exp(s_ctx - m)
        den = jnp.sum(p_loc, axis=-1, keepdims=True) + jnp.sum(p_ctx, axis=-1, keepdims=True)
        o = _dot(p_loc.astype(BF16), v_ref[0, pl.ds(koff, mk), :]) + _dot(p_ctx.astype(BF16), vc)
        o_ref[0, pl.ds(qoff, mq), :] = (o / den).astype(BF16)
        return carry

    lax.fori_loop(0, nb, body, 0, unroll=min(NA_UNROLL, nb))


def _na(pq, pkv, pckv, rpb):
    b, n, _ = pq.shape
    nc = pckv.shape[1]
    rows = n // GRID_W
    kr = min(NA_ROWS, rows)
    rq, ku, starts, table_of, tables = _na_plan(rows, kr)
    bias = _na_bias(rpb, kr, ku, tables)
    nblk = rows // rq
    nb = min(4, nblk)
    assert nblk % nb == 0
    mq, mk = rq * GRID_W, ku * GRID_W
    return pl.pallas_call(
        functools.partial(_na_kernel, nb=nb, mq=mq, mk=mk),
        grid_spec=pltpu.PrefetchScalarGridSpec(
            num_scalar_prefetch=2, grid=(b, NA_HEADS, nblk // nb),
            in_specs=[
                pl.BlockSpec((1, nb * mq, HEAD_DIM), lambda bb, h, i, us, tb: (bb, i, HB_NA_Q + h)),
                pl.BlockSpec((1, n, HEAD_DIM), lambda bb, h, i, us, tb: (bb, 0, HB_NA_K + h)),
                pl.BlockSpec((1, n, HEAD_DIM), lambda bb, h, i, us, tb: (bb, 0, HB_NA_V + h)),
                pl.BlockSpec((1, nc, HEAD_DIM), lambda bb, h, i, us, tb: (bb, 0, HB_NA_K + h)),
                pl.BlockSpec((1, nc, HEAD_DIM), lambda bb, h, i, us, tb: (bb, 0, HB_NA_V + h)),
                pl.BlockSpec((len(tables), 1, mq, mk), lambda bb, h, i, us, tb: (0, h, 0, 0)),
            ],
            out_specs=pl.BlockSpec((1, nb * mq, HEAD_DIM), lambda bb, h, i, us, tb: (bb, i, h))),
        out_shape=jax.ShapeDtypeStruct((b, n, NA_WIDTH), BF16),
        compiler_params=_cparams(("arbitrary", "arbitrary", "arbitrary")),
        name="na_attn",
    )(jnp.asarray(starts), jnp.asarray(table_of), pq, pkv, pkv, pckv, pckv, bias)


GQA_ONES_ROWS = 16
GQA_CHUNKS_PER_TRIP = 4


def _gqa_kernel(bound_ref, q0_ref, q1_ref, q2_ref, k_ref, v_ref, kc_ref, vc_ref, o_ref,
                qt_s, vt_s, vct_s, s_s, sc_s, m_s, acc_s, *, tk, n, fixed_shift):
    i = pl.program_id(2)
    tq = q0_ref.shape[1]
    nchunk = n // tk
    hd = HEAD_DIM

    def to_t(a):
        return a.astype(F32).T.astype(BF16)

    @pl.when(i == 0)
    def _():
        def tr(c, carry):
            off = pl.multiple_of(c * tk, tk)
            vt_s[c, 0:hd, :] = to_t(v_ref[0, pl.ds(off, tk), :])
            vt_s[c, hd:, :] = jnp.ones((GQA_ONES_ROWS, tk), BF16)
            return carry
        lax.fori_loop(0, nchunk, tr, 0)
        vct_s[0:hd, :] = to_t(vc_ref[0])
        vct_s[hd:, :] = jnp.ones((GQA_ONES_ROWS, vct_s.shape[1]), BF16)

    for g, qr in enumerate((q0_ref, q1_ref, q2_ref)):
        qt_s[:, g * tq:(g + 1) * tq] = to_t(qr[0])

    def scores(c, slot):
        off = pl.multiple_of(c * tk, tk)
        s_s[slot] = _dot(k_ref[0, pl.ds(off, tk), :], qt_s[...])

    def update(s, vt, first=False):
        if fixed_shift:
            pv = _dot(vt, jnp.exp(s - bound_ref[0]).astype(BF16))
            acc_s[...] = pv if first else acc_s[...] + pv
            return
        smax = jnp.max(s, axis=0, keepdims=True)
        if first:
            m_new = smax
        else:
            m_prev = m_s[...]
            m_new = jnp.maximum(m_prev, smax)
            alpha = jnp.exp(m_prev - m_new)
        p = jnp.exp(s - m_new).astype(BF16)
        pv = _dot(vt, p)
        acc_s[...] = pv if first else alpha * acc_s[...] + pv
        m_s[...] = m_new

    scores(0, 0)
    sc_s[...] = _dot(kc_ref[0], qt_s[...])
    update(sc_s[...], vct_s[...], first=True)

    per_trip = GQA_CHUNKS_PER_TRIP if nchunk % GQA_CHUNKS_PER_TRIP == 0 else 2

    def body(ct, carry):
        c = per_trip * ct
        for u in range(per_trip):
            scores(jnp.minimum(c + u + 1, nchunk - 1), (u + 1) % 2)
            update(s_s[u % 2], vt_s[c + u])
        return carry

    lax.fori_loop(0, nchunk // per_trip, body, 0)
    acc = acc_s[...]
    o = (acc[0:hd] / acc[hd:hd + 1]).T
    for g in range(GQA_GROUP):
        o_ref[0, :, g * hd:(g + 1) * hd] = o[g * tq:(g + 1) * tq].astype(BF16)


GQA_FIXED_SHIFT_LIMIT = 40.0


def _gqa(pq, pkv, pckv, gain_q, gain_k):
    bound = 1.01 * HEAD_DIM * ATTN_SCALE * jnp.max(jnp.abs(gain_q)) * jnp.max(jnp.abs(gain_k))
    bound = bound.astype(F32).reshape(1)
    return lax.cond(bound[0] <= GQA_FIXED_SHIFT_LIMIT,
                    functools.partial(_gqa_call, fixed_shift=True),
                    functools.partial(_gqa_call, fixed_shift=False),
                    pq, pkv, pckv, bound)


def _gqa_call(pq, pkv, pckv, bound, *, fixed_shift):
    b, n, _ = pq.shape
    nc = pckv.shape[1]
    tq = min(512, n)
    tk = min(512, n // 2)
    assert n % (2 * tk) == 0
    nq = GQA_GROUP * tq
    vr = HEAD_DIM + GQA_ONES_ROWS

    def qspec(g):
        return pl.BlockSpec((1, tq, HEAD_DIM), lambda bb, h, i: (bb, i, HB_GQA_Q + h * GQA_GROUP + g))

    return pl.pallas_call(
        functools.partial(_gqa_kernel, tk=tk, n=n, fixed_shift=fixed_shift),
        grid=(b, GQA_KV_HEADS, n // tq),
        in_specs=[
            pl.BlockSpec(memory_space=pltpu.SMEM),
            qspec(0), qspec(1), qspec(2),
            pl.BlockSpec((1, n, HEAD_DIM), lambda bb, h, i: (bb, 0, HB_GQA_K + h)),
            pl.BlockSpec((1, n, HEAD_DIM), lambda bb, h, i: (bb, 0, HB_GQA_V + h)),
            pl.BlockSpec((1, nc, HEAD_DIM), lambda bb, h, i: (bb, 0, HB_GQA_K + h)),
            pl.BlockSpec((1, nc, HEAD_DIM), lambda bb, h, i: (bb, 0, HB_GQA_V + h)),
        ],
        out_specs=pl.BlockSpec((1, tq, GQA_GROUP * HEAD_DIM), lambda bb, h, i: (bb, i, h)),
        out_shape=jax.ShapeDtypeStruct((b, n, GQA_Q_WIDTH), BF16),
        scratch_shapes=[
            pltpu.VMEM((HEAD_DIM, nq), BF16),
            pltpu.VMEM((n // tk, vr, tk), BF16),
            pltpu.VMEM((vr, nc), BF16),
            pltpu.VMEM((2, tk, nq), F32),
            pltpu.VMEM((nc, nq), F32),
            pltpu.VMEM((1, nq), F32),
            pltpu.VMEM((vr, nq), F32),
        ],
        compiler_params=_cparams(("arbitrary", "arbitrary", "arbitrary")),
        name="gqa_attn_fixed_shift" if fixed_shift else "gqa_attn_running_max",
    )(bound, pq, pq, pq, pkv, pkv, pckv, pckv)


def _ctx_attn_kernel(q_ref, k_ref, v_ref, o_ref):
    s = _dot_nt(q_ref[0], k_ref[0])
    m = jnp.max(s, axis=-1, keepdims=True)
    p = jnp.exp(s - m)
    den = jnp.sum(p, axis=-1, keepdims=True)
    o_ref[0] = (_dot(p.astype(BF16), v_ref[0]) / den).astype(BF16)


def _ctx_attn(pcq, pckv):
    b, nc, _ = pcq.shape
    nh = NA_HEADS + GQA_Q_HEADS

    def kmap(bb, h):
        g = jnp.maximum(h - NA_HEADS, 0) // GQA_GROUP
        return bb, 0, jnp.where(h < NA_HEADS, HB_NA_K + h, HB_GQA_K + g)

    def vmap_(bb, h):
        g = jnp.maximum(h - NA_HEADS, 0) // GQA_GROUP
        return bb, 0, jnp.where(h < NA_HEADS, HB_NA_V + h, HB_GQA_V + g)

    return pl.pallas_call(
        _ctx_attn_kernel,
        grid=(b, nh),
        in_specs=[
            pl.BlockSpec((1, nc, HEAD_DIM), lambda bb, h: (bb, 0, HB_NA_Q + h)),
            pl.BlockSpec((1, nc, HEAD_DIM), kmap),
            pl.BlockSpec((1, nc, HEAD_DIM), vmap_),
        ],
        out_specs=pl.BlockSpec((1, nc, HEAD_DIM), lambda bb, h: (bb, 0, h)),
        out_shape=jax.ShapeDtypeStruct((b, nc, nh * HEAD_DIM), BF16),
        compiler_params=_cparams(("arbitrary", "arbitrary")),
        name="ctx_attn",
    )(pcq, pckv, pckv)


def _merge_kernel(x_ref, ga_ref, gb_ref, gc_ref, yp_ref, yn_ref, yg_ref, wbr_ref, wout_ref, g1_ref, o_ref):
    r1 = POOL_WIDTH
    r2 = POOL_WIDTH + NA_WIDTH
    z = ga_ref[0].astype(F32) * _dot(yp_ref[0], wbr_ref[0:r1, :])
    z = z + gb_ref[0].astype(F32) * _dot(yn_ref[0], wbr_ref[r1:r2, :])
    z = z + gc_ref[0].astype(F32) * _dot(yg_ref[0], wbr_ref[r2:, :])
    o_ref[0] = x_ref[0] + g1_ref[0] * _dot(z.astype(BF16), wout_ref[...])


def _merge(x, p, y_pool, y_na, na_cb, y_gqa, gqa_cb, w_br, w_out, g1):
    b, n, d = x.shape
    tm = min(256, n)
    const = lambda bb, i: (0, 0)
    return pl.pallas_call(
        _merge_kernel,
        grid=(b, n // tm),
        in_specs=[
            pl.BlockSpec((1, tm, d), lambda bb, i: (bb, i, 0)),
            pl.BlockSpec((1, tm, d), lambda bb, i: (bb, i, 0)),
            pl.BlockSpec((1, tm, d), lambda bb, i: (bb, i, 1)),
            pl.BlockSpec((1, tm, d), lambda bb, i: (bb, i, 2)),
            pl.BlockSpec((1, tm, POOL_WIDTH), lambda bb, i: (bb, i, 0)),
            pl.BlockSpec((1, tm, NA_WIDTH), lambda bb, i: (bb, i, na_cb)),
            pl.BlockSpec((1, tm, GQA_Q_WIDTH), lambda bb, i: (bb, i, gqa_cb)),
            pl.BlockSpec(w_br.shape, const, pipeline_mode=pl.Buffered(1)),
            pl.BlockSpec(w_out.shape, const, pipeline_mode=pl.Buffered(1)),
            pl.BlockSpec((1, 1, d), lambda bb, i: (bb, 0, 0)),
        ],
        out_specs=pl.BlockSpec((1, tm, d), lambda bb, i: (bb, i, 0)),
        out_shape=jax.ShapeDtypeStruct((b, n, d), F32),
        compiler_params=_cparams(("arbitrary", "arbitrary")),
        name="branch_merge",
    )(x, p, p, p, y_pool, y_na, y_gqa, w_br, w_out, g1)


def _top2_of4(a, b, c, d):
    hi1, lo1 = jnp.maximum(a, b), jnp.minimum(a, b)
    hi2, lo2 = jnp.maximum(c, d), jnp.minimum(c, d)
    return jnp.maximum(hi1, hi2) + jnp.maximum(jnp.minimum(hi1, hi2), jnp.maximum(lo1, lo2))


def _router_kernel(x_ref, g_ref, sc_ref, sh_ref, whi_ref, wlo_ref, br_ref,
                   h_ref, e_ref, w_ref, rank_ref, cnt_ref, carry_ref):
    first = (pl.program_id(0) == 0) & (pl.program_id(1) == 0)

    @pl.when(first)
    def _():
        carry_ref[...] = jnp.zeros_like(carry_ref)

    x = x_ref[0]
    h = x * lax.rsqrt(jnp.mean(x * x, axis=-1, keepdims=True) + EPS) * g_ref[...]
    h = h * (1.0 + sc_ref[0]) + sh_ref[0]
    h_ref[0] = h
    h_hi = h.astype(BF16)
    h_lo = (h - h_hi.astype(F32)).astype(BF16)
    whi = whi_ref[...]
    logit = _dot_nt(whi, h_hi) + _dot_nt(whi, h_lo) + _dot_nt(wlo_ref[...], h_hi)
    s = jax.nn.sigmoid(logit)
    sel = s + br_ref[...]
    epg = EXPERTS_PER_GROUP
    row = lambda a, e: a[e:e + 1, :]
    gscore = [_top2_of4(*[row(sel, g * epg + j) for j in range(epg)]) for g in range(N_GROUPS)]
    g_best = jnp.zeros_like(gscore[0], dtype=jnp.int32)
    best = gscore[0]
    for g in range(1, N_GROUPS):
        upd = gscore[g] > best
        g_best = jnp.where(upd, g, g_best)
        best = jnp.where(upd, gscore[g], best)
    vs, ss = [], []
    for j in range(epg):
        v = row(sel, j)
        sv = row(s, j)
        for g in range(1, N_GROUPS):
            v = jnp.where(g_best == g, row(sel, g * epg + j), v)
            sv = jnp.where(g_best == g, row(s, g * epg + j), sv)
        vs.append(v)
        ss.append(sv)
    i1 = jnp.zeros_like(g_best)
    v1 = vs[0]
    for j in range(1, epg):
        upd = vs[j] > v1
        i1 = jnp.where(upd, j, i1)
        v1 = jnp.where(upd, vs[j], v1)
    i2 = jnp.full_like(g_best, -1)
    v2 = jnp.full_like(v1, -jnp.inf)
    for j in range(epg):
        upd = (i1 != j) & ((i2 < 0) | (vs[j] > v2))
        i2 = jnp.where(upd, j, i2)
        v2 = jnp.where(upd, vs[j], v2)
    w1 = sum(jnp.where(i1 == j, ss[j], 0.0) for j in range(epg))
    w2 = sum(jnp.where(i2 == j, ss[j], 0.0) for j in range(epg))
    tot = w1 + w2
    w_ref[0] = jnp.concatenate([w1 / tot, w2 / tot], axis=0)
    e1 = g_best * epg + i1
    e2 = g_best * epg + i2
    e_ref[0] = jnp.concatenate([e1, e2], axis=0)

    tm = x.shape[0]
    eidx = lax.broadcasted_iota(jnp.int32, (N_EXPERTS, tm), 0)
    oh1 = eidx == e1
    oh2 = eidx == e2
    oh = jnp.where(oh1 | oh2, 1.0, 0.0)
    before = lax.broadcasted_iota(jnp.int32, (tm, tm), 0) < lax.broadcasted_iota(jnp.int32, (tm, tm), 1)
    prefix = _dot(oh.astype(BF16), jnp.where(before, 1.0, 0.0).astype(BF16))
    base = carry_ref[...] + prefix
    r1 = jnp.sum(jnp.where(oh1, base, 0.0), axis=0, keepdims=True)
    r2 = jnp.sum(jnp.where(oh2, base, 0.0), axis=0, keepdims=True)
    rank_ref[0] = jnp.concatenate([r1, r2], axis=0).astype(jnp.int32)
    carry = carry_ref[...] + jnp.sum(oh, axis=1, keepdims=True)
    carry_ref[...] = carry
    cnt_ref[...] = jnp.broadcast_to(carry, cnt_ref.shape).astype(jnp.int32)


def _router(x, gain, sc, sh, wr_hi, wr_lo, b_router):
    b, n, d = x.shape
    tm = min(512, n)
    ne = wr_hi.shape[0]
    pair = pl.BlockSpec((1, 2, tm), lambda bb, i: (bb, 0, i))
    return pl.pallas_call(
        _router_kernel,
        grid=(b, n // tm),
        in_specs=[
            pl.BlockSpec((1, tm, d), lambda bb, i: (bb, i, 0)),
            pl.BlockSpec((1, d), lambda bb, i: (0, 0)),
            pl.BlockSpec((1, 1, d), lambda bb, i: (bb, 0, 0)),
            pl.BlockSpec((1, 1, d), lambda bb, i: (bb, 0, 0)),
            pl.BlockSpec((ne, d), lambda bb, i: (0, 0)),
            pl.BlockSpec((ne, d), lambda bb, i: (0, 0)),
            pl.BlockSpec((ne, 1), lambda bb, i: (0, 0)),
        ],
        out_specs=[
            pl.BlockSpec((1, tm, d), lambda bb, i: (bb, i, 0)),
            pair, pair, pair,
            pl.BlockSpec((ne, HEAD_DIM), lambda bb, i: (0, 0)),
        ],
        out_shape=[
            jax.ShapeDtypeStruct((b, n, d), F32),
            jax.ShapeDtypeStruct((b, 2, n), jnp.int32),
            jax.ShapeDtypeStruct((b, 2, n), F32),
            jax.ShapeDtypeStruct((b, 2, n), jnp.int32),
            jax.ShapeDtypeStruct((ne, HEAD_DIM), jnp.int32),
        ],
        scratch_shapes=[pltpu.VMEM((ne, 1), F32)],
        compiler_params=_cparams(("arbitrary", "arbitrary")),
        name="norm_router",
    )(x, gain.reshape(1, d), sc, sh, wr_hi, wr_lo, b_router.reshape(ne, 1))


MOE_TILE = 256
MOE_SCATTER_TILE = 512
MOE_COMBINE_TILE = 256
MOE_DMA_UNROLL = 8


def _moe_plan(e, rank, cnt):
    b, _, n = e.shape
    t = b * n
    counts = cnt[:, 0]
    ntile_e = (counts + MOE_TILE - 1) // MOE_TILE
    tile_end = jnp.cumsum(ntile_e)
    off = (tile_end - ntile_e) * MOE_TILE
    nt = 2 * t // MOE_TILE + N_EXPERTS
    tile_expert = jnp.sum(jnp.arange(nt)[:, None] >= tile_end[None, :], axis=1)
    tile_expert = jnp.minimum(tile_expert, N_EXPERTS - 1).astype(jnp.int32)
    ef = e.transpose(1, 0, 2).reshape(2, t)
    rf = rank.transpose(1, 0, 2).reshape(2, t)
    pos = rf + jnp.sum(jnp.where(ef[..., None] == jnp.arange(N_EXPERTS), off, 0), axis=-1)
    return pos.reshape(2 * t).astype(jnp.int32), tile_expert, tile_end[-1:].astype(jnp.int32), nt


def _scatter_kernel(pos_ref, h_ref, xs0_hbm, xs_hbm, sem, *, t):
    del xs0_hbm
    tm = h_ref.shape[0]
    base = pl.program_id(0) * tm

    def copies(j):
        return [pltpu.make_async_copy(h_ref.at[pl.ds(j, 1)], xs_hbm.at[pl.ds(pos_ref[k * t + base + j], 1)], sem)
                for k in range(2)]

    def start(j, c):
        for cp in copies(j):
            cp.start()
        return c

    def wait(j, c):
        for cp in copies(j):
            cp.wait()
        return c

    lax.fori_loop(0, tm, start, 0, unroll=MOE_DMA_UNROLL)
    lax.fori_loop(0, tm, wait, 0, unroll=MOE_DMA_UNROLL)


def _scatter_rows(pos, h, nrows):
    t, d = h.shape
    tm = min(MOE_SCATTER_TILE, t)
    any_spec = pl.BlockSpec(memory_space=pl.ANY)
    return pl.pallas_call(
        functools.partial(_scatter_kernel, t=t),
        grid_spec=pltpu.PrefetchScalarGridSpec(
            num_scalar_prefetch=1, grid=(t // tm,),
            in_specs=[pl.BlockSpec((tm, d), lambda i, p: (i, 0)), any_spec], out_specs=any_spec,
            scratch_shapes=[pltpu.SemaphoreType.DMA(())]),
        out_shape=jax.ShapeDtypeStruct((nrows, d), F32),
        input_output_aliases={2: 0},
        compiler_params=pltpu.CompilerParams(dimension_semantics=("arbitrary",), has_side_effects=True),
        name="moe_scatter",
    )(pos, h, jnp.zeros((nrows, d), F32))


def _experts_kernel(te_ref, nv_ref, xs_ref, wgu_ref, wd_ref, ys_ref):
    del te_ref

    @pl.when(pl.program_id(0) < nv_ref[0])
    def _():
        gu = _dot(xs_ref[...].astype(BF16), wgu_ref[0])
        ff = gu.shape[1] // 2
        gate = gu[:, :ff]
        a = (gate * jax.nn.sigmoid(gate) * gu[:, ff:]).astype(BF16)
        ys_ref[...] = _dot(a, wd_ref[0])


def _experts(xs, tile_expert, nvalid, w_gu, w_down):
    nrows, d = xs.shape
    nt = nrows // MOE_TILE
    ne, _, f2 = w_gu.shape
    row = lambda i, te, nv: (jnp.minimum(i, nv[0] - 1), 0)
    wmap = lambda i, te, nv: (te[jnp.minimum(i, nv[0] - 1)], 0, 0)
    return pl.pallas_call(
        _experts_kernel,
        grid_spec=pltpu.PrefetchScalarGridSpec(
            num_scalar_prefetch=2, grid=(nt,),
            in_specs=[
                pl.BlockSpec((MOE_TILE, d), row),
                pl.BlockSpec((1, d, f2), wmap),
                pl.BlockSpec((1, f2 // 2, d), wmap),
            ],
            out_specs=pl.BlockSpec((MOE_TILE, d), row)),
        out_shape=jax.ShapeDtypeStruct((nrows, d), F32),
        compiler_params=_cparams(("arbitrary",)),
        name="moe_experts",
    )(tile_expert, nvalid, xs, w_gu, w_down)


def _combine_kernel(pos_ref, x_ref, w_ref, g2_ref, ys_hbm, o_ref, buf, sem, *, t):
    i = pl.program_id(0)
    nsteps = pl.num_programs(0)
    tm = x_ref.shape[0]

    def copies(tile, slot, j):
        tok = tile * tm + j
        return [pltpu.make_async_copy(ys_hbm.at[pl.ds(pos_ref[k * t + tok], 1)],
                                      buf.at[slot, k, pl.ds(j, 1)], sem.at[slot]) for k in range(2)]

    def issue(tile, slot):
        def body(j, c):
            for cp in copies(tile, slot, j):
                cp.start()
            return c
        lax.fori_loop(0, tm, body, 0, unroll=MOE_DMA_UNROLL)

    def wait(tile, slot):
        def body(j, c):
            for cp in copies(tile, slot, j):
                cp.wait()
            return c
        lax.fori_loop(0, tm, body, 0, unroll=MOE_DMA_UNROLL)

    @pl.when(i == 0)
    def _():
        issue(0, 0)

    @pl.when(i + 1 < nsteps)
    def _():
        issue(i + 1, (i + 1) % 2)

    slot = i % 2
    wait(i, slot)
    w = w_ref[...]
    y = w[:, 0:1] * buf[slot, 0] + w[:, 1:2] * buf[slot, 1]
    o_ref[...] = x_ref[...] + g2_ref[0] * y


def _combine(pos, x, w, g2, ys):
    b, n, d = x.shape
    t = b * n
    tm = min(MOE_COMBINE_TILE, n)
    per_b = n // tm
    out = pl.pallas_call(
        functools.partial(_combine_kernel, t=t),
        grid_spec=pltpu.PrefetchScalarGridSpec(
            num_scalar_prefetch=1, grid=(t // tm,),
            in_specs=[
                pl.BlockSpec((tm, d), lambda i, p: (i, 0)),
                pl.BlockSpec((tm, 2), lambda i, p: (i, 0)),
                pl.BlockSpec((1, 1, d), lambda i, p: (i // per_b, 0, 0)),
                pl.BlockSpec(memory_space=pl.ANY),
            ],
            out_specs=pl.BlockSpec((tm, d), lambda i, p: (i, 0)),
            scratch_shapes=[pltpu.VMEM((2, 2, tm, d), F32), pltpu.SemaphoreType.DMA((2,))]),
        out_shape=jax.ShapeDtypeStruct((t, d), F32),
        compiler_params=_cparams(("arbitrary",)),
        name="moe_combine",
    )(pos, x.reshape(t, d), w, g2, ys)
    return out.reshape(b, n, d)


def _moe(x, gain, sc, sh, g2, wr_hi, wr_lo, b_router, w_gu, w_down):
    b, n, d = x.shape
    h, e, w, rank, cnt = _router(x, gain, sc, sh, wr_hi, wr_lo, b_router)
    pos, tile_expert, nvalid, nt = _moe_plan(e, rank, cnt)
    xs = _scatter_rows(pos, h.reshape(b * n, d), nt * MOE_TILE)
    ys = _experts(xs, tile_expert, nvalid, w_gu, w_down)
    return _combine(pos, x, w.transpose(0, 2, 1).reshape(b * n, 2), g2, ys)


def _rope_tables(n):
    t = jnp.arange(n, dtype=jnp.int32)
    row = (t // GRID_W).astype(F32)
    col = (t % GRID_W).astype(F32)
    axis_dim = HEAD_DIM // 2
    inv = ROPE_THETA ** (-jnp.arange(0, axis_dim, 2, dtype=F32) / axis_dim)
    ang = jnp.concatenate([row[:, None] * inv, col[:, None] * inv], axis=-1)
    cos, sin = jnp.cos(ang), jnp.sin(ang)
    return jnp.concatenate([cos, cos], axis=-1), jnp.concatenate([-sin, sin], axis=-1)


def kernel(x, c, ctx, c_ctx, w_mod, b_mod, norm1, norm2, w_in, qk_gain, pool_w, pool_scale,
           na_rpb, w_br, w_out, w_router, b_router, w_gu, w_down):
    b, n, d = x.shape
    depth = w_mod.shape[0]
    rows = n // GRID_W
    kr = min(NA_ROWS, rows)
    assert n % GRID_W == 0 and rows % kr == 0 and b + 1 <= 8

    cos, sin = _rope_tables(n)
    mods = _modulation(jnp.concatenate([c, c_ctx[None, :]], axis=0), w_mod, b_mod)
    wr_t = w_router.T
    wr_hi = wr_t.astype(BF16)
    wr_lo = (wr_t - wr_hi.astype(F32)).astype(BF16)

    for l in range(depth):
        last = l == depth - 1
        mx = mods[l, :b].reshape(b, 1, N_MOD, d)
        mc = jnp.broadcast_to(mods[l, b].reshape(1, 1, N_MOD, d), (b, 1, N_MOD, d))
        x_sh1, x_sc1, x_g1, x_sh2, x_sc2, x_g2 = [mx[:, :, k] for k in range(N_MOD)]
        c_sh1, c_sc1, c_g1, c_sh2, c_sc2, c_g2 = [mc[:, :, k] for k in range(N_MOD)]
        w_br_l = w_br[l].astype(BF16)
        w_out_l = w_out[l].astype(BF16)
        pool_w_l = pool_w[l].astype(BF16)

        def proj(t, sc, sh, kind, tables=(None, None)):
            return _inproj(t, norm1[l], sc, sh, w_in, l, qk_gain[l], *tables, kind=kind)

        pckv = proj(ctx, c_sc1, c_sh1, "kv")
        pq = proj(x, x_sc1, x_sh1, "q", (cos, sin))
        pkv = proj(x, x_sc1, x_sh1, "kv", (cos, sin))
        pg = proj(x, x_sc1, x_sh1, "gates")
        y_pool = _pool(pq, pool_w_l, pool_scale[l])
        y_na = _na(pq, pkv, pckv, na_rpb[l])
        y_gqa = _gqa(pq, pkv, pckv, qk_gain[l, 2], qk_gain[l, 3])
        x = _merge(x, pg, y_pool, y_na, 0, y_gqa, 0, w_br_l, w_out_l, x_g1)

        w_gu_l = w_gu[l].astype(BF16)
        w_down_l = w_down[l].astype(BF16)
        if not last:
            pcq = proj(ctx, c_sc1, c_sh1, "q")
            pcg = proj(ctx, c_sc1, c_sh1, "gates")
            yc_pool = _pool(pcq, pool_w_l, pool_scale[l])
            yc = _ctx_attn(pcq, pckv)
            ctx = _merge(ctx, pcg, yc_pool, yc, 0, yc, 1, w_br_l, w_out_l, c_g1)
            ctx = _moe(ctx, norm2[l], c_sc2, c_sh2, c_g2, wr_hi, wr_lo, b_router, w_gu_l, w_down_l)
        x = _moe(x, norm2[l], x_sc2, x_sh2, x_g2, wr_hi, wr_lo, b_router, w_gu_l, w_down_l)
    return x
```

```python
import functools

import numpy as np
import jax
import jax.numpy as jnp
from jax import lax
from jax.experimental import pallas as pl
from jax.experimental.pallas import tpu as pltpu

F32 = jnp.float32
BF16 = jnp.bfloat16

GRID_W = 64
HEAD_DIM = 128
ROPE_THETA = 10000.0
EPS = 1e-6
POOL_WINDOWS = (2, 4, 8, 16)
POOL_CH = 128
POOL_WIDTH = len(POOL_WINDOWS) * POOL_CH
NA_HEADS = 6
NA_WIDTH = NA_HEADS * HEAD_DIM
NA_ROWS = 8
NA_COLS = 16
GQA_Q_HEADS = 6
GQA_KV_HEADS = 2
GQA_GROUP = GQA_Q_HEADS // GQA_KV_HEADS
GQA_Q_WIDTH = GQA_Q_HEADS * HEAD_DIM
N_BRANCH = 3
N_EXPERTS = 16
N_GROUPS = 4
EXPERTS_PER_GROUP = N_EXPERTS // N_GROUPS
N_MOD = 6
ATTN_SCALE = HEAD_DIM ** -0.5

COL_BLOCK = 2048
CB_POOL = 0
HB_NA_Q = POOL_WIDTH // HEAD_DIM
HB_GQA_Q = HB_NA_Q + NA_HEADS
HB_NA_K = 0
HB_NA_V = HB_NA_K + NA_HEADS
HB_GQA_K = HB_NA_V + NA_HEADS
HB_GQA_V = HB_GQA_K + GQA_KV_HEADS
PROJ_KINDS = {"gates": (0, 3), "q": (3, 1), "kv": (4, 1)}

V7X_VMEM_LIMIT = 56 * 1024 * 1024
NEG_BIG = -1e30


def _cparams(sem):
    return pltpu.CompilerParams(dimension_semantics=sem, vmem_limit_bytes=V7X_VMEM_LIMIT)


def _dot(a, b):
    return jnp.dot(a, b, preferred_element_type=F32)


def _dot_nt(a, b):
    return lax.dot_general(a, b, (((1,), (1,)), ((), ())), preferred_element_type=F32)


def _mod_kernel(ct_ref, w_ref, b_ref, o_ref, *, n_rows):
    ct = ct_ref[...]
    a = ct * jax.nn.sigmoid(ct)
    w = w_ref[0]
    rows = [jnp.sum(w * a[:, r:r + 1], axis=0, keepdims=True) for r in range(n_rows)]
    rows += [jnp.zeros_like(rows[0])] * (8 - n_rows)
    o_ref[0] = jnp.concatenate(rows, axis=0) + b_ref[0]


def _modulation(c_rows, w_mod, b_mod):
    depth, d, nm = w_mod.shape
    n_rows = c_rows.shape[0]
    ct = jnp.zeros((d, 8), F32).at[:, :n_rows].set(c_rows.T)
    tn = 1024
    return pl.pallas_call(
        functools.partial(_mod_kernel, n_rows=n_rows),
        grid=(depth, nm // tn),
        in_specs=[
            pl.BlockSpec((d, 8), lambda l, j: (0, 0)),
            pl.BlockSpec((1, d, tn), lambda l, j: (l, 0, j)),
            pl.BlockSpec((1, 1, tn), lambda l, j: (l, 0, j)),
        ],
        out_specs=pl.BlockSpec((1, 8, tn), lambda l, j: (l, 0, j)),
        out_shape=jax.ShapeDtypeStruct((depth, 8, nm), F32),
        compiler_params=_cparams(("arbitrary", "arbitrary")),
        name="adaln_mod",
    )(ct, w_mod, b_mod.reshape(depth, 1, nm))


def _head_norm(a, gain):
    return a * lax.rsqrt(jnp.mean(a * a, axis=-1, keepdims=True) + EPS) * gain


def _rope(y, cos, sin):
    return y * cos + pltpu.roll(y, HEAD_DIM // 2, 1) * sin


PROJ_CHUNK = 2 * HEAD_DIM


def _proj_epilogue(kind, head, a, qg_ref, rope_tables):
    if kind == "gates":
        return jax.nn.sigmoid(a)
    if kind == "q":
        if head < HB_NA_Q:
            return a
        if head < HB_GQA_Q:
            return _head_norm(a, qg_ref[0:1, :]) * ATTN_SCALE
        y = _head_norm(a, qg_ref[2:3, :])
        return (_rope(y, *rope_tables) if rope_tables else y) * ATTN_SCALE
    if head < HB_NA_V:
        return _head_norm(a, qg_ref[1:2, :])
    if HB_GQA_K <= head < HB_GQA_V:
        y = _head_norm(a, qg_ref[3:4, :])
        return _rope(y, *rope_tables) if rope_tables else y
    return a


def _inproj_kernel(*refs, kind, rope):
    if rope:
        x_ref, g_ref, sc_ref, sh_ref, w_ref, qg_ref, cos_ref, sin_ref, o_ref, wb_s = refs
    else:
        x_ref, g_ref, sc_ref, sh_ref, w_ref, qg_ref, o_ref, wb_s = refs

    @pl.when((pl.program_id(1) == 0) & (pl.program_id(2) == 0))
    def _():
        wb_s[...] = w_ref[0].astype(BF16)

    x = x_ref[0]
    h = x * lax.rsqrt(jnp.mean(x * x, axis=-1, keepdims=True) + EPS) * g_ref[...]
    hb = (h * (1.0 + sc_ref[0]) + sh_ref[0]).astype(BF16)
    tables = (cos_ref[...], sin_ref[...]) if rope else None
    per = PROJ_CHUNK // HEAD_DIM
    for c in range(COL_BLOCK // PROJ_CHUNK):
        acc = _dot(hb, wb_s[:, c * PROJ_CHUNK:(c + 1) * PROJ_CHUNK])
        for u in range(per):
            head = c * per + u
            y = _proj_epilogue(kind, head, acc[:, u * HEAD_DIM:(u + 1) * HEAD_DIM], qg_ref, tables)
            o_ref[0, :, head * HEAD_DIM:(head + 1) * HEAD_DIM] = y.astype(BF16)


def _inproj(x, gain, sc, sh, w_in, layer, qgain, cos, sin, *, kind):
    b, n, d = x.shape
    rope = cos is not None and kind != "gates"
    tm = min(512, n)
    j0, nj = PROJ_KINDS[kind]
    in_specs = [
        pl.BlockSpec((1, tm, d), lambda j, bb, i: (bb, i, 0)),
        pl.BlockSpec((1, d), lambda j, bb, i: (0, 0)),
        pl.BlockSpec((1, 1, d), lambda j, bb, i: (bb, 0, 0)),
        pl.BlockSpec((1, 1, d), lambda j, bb, i: (bb, 0, 0)),
        pl.BlockSpec((1, d, COL_BLOCK), lambda j, bb, i: (layer, 0, j + j0), pipeline_mode=pl.Buffered(1)),
        pl.BlockSpec((4, HEAD_DIM), lambda j, bb, i: (0, 0)),
    ]
    args = [x, gain.reshape(1, d), sc, sh, w_in, qgain]
    if rope:
        in_specs += [pl.BlockSpec((tm, HEAD_DIM), lambda j, bb, i: (i, 0))] * 2
        args += [cos, sin]
    return pl.pallas_call(
        functools.partial(_inproj_kernel, kind=kind, rope=rope),
        grid=(nj, b, n // tm),
        in_specs=in_specs,
        out_specs=pl.BlockSpec((1, tm, COL_BLOCK), lambda j, bb, i: (bb, i, j)),
        out_shape=jax.ShapeDtypeStruct((b, n, nj * COL_BLOCK), BF16),
        scratch_shapes=[pltpu.VMEM((d, COL_BLOCK), BF16)],
        compiler_params=_cparams(("arbitrary", "arbitrary", "arbitrary")),
        name="inproj_" + kind + ("_rope" if rope else ""),
    )(*args)


POOL_HALO = 16


def _pool_kernel(prev_ref, cur_ref, next_ref, w_ref, s_ref, o_ref, buf_ref, *, tm, n):
    i = pl.program_id(1)
    nt = pl.num_programs(1)
    hl = POOL_HALO
    buf_ref[pl.ds(hl, tm), :] = cur_ref[0].astype(F32)
    buf_ref[pl.ds(0, hl), :] = jnp.where(i > 0, prev_ref[0].astype(F32), 0.0)
    buf_ref[pl.ds(hl + tm, hl), :] = jnp.where(i < nt - 1, next_ref[0].astype(F32), 0.0)
    t = i * tm + lax.broadcasted_iota(jnp.int32, (tm, 1), 0)
    for g, w in enumerate(POOL_WINDOWS):
        sl = slice(g * POOL_CH, (g + 1) * POOL_CH)
        acc = buf_ref[pl.ds(hl - w // 2, tm), sl]
        for off in range(-w // 2 + 1, w // 2):
            acc = acc + buf_ref[pl.ds(hl + off, tm), sl]
        cnt = (jnp.minimum(t + w // 2, n) - jnp.maximum(t - w // 2, 0)).astype(F32)
        dlt = acc / cnt - buf_ref[pl.ds(hl, tm), sl]
        y = _dot(dlt.astype(BF16), w_ref[g]) * s_ref[:, sl]
        o_ref[0, :, sl] = y.astype(BF16)


def _pool(p, pool_w, pool_scale):
    b, n, _ = p.shape
    tm = min(512, n)
    hl = POOL_HALO
    hb = tm // hl
    last = n // hl - 1
    return pl.pallas_call(
        functools.partial(_pool_kernel, tm=tm, n=n),
        grid=(b, n // tm),
        in_specs=[
            pl.BlockSpec((1, hl, POOL_WIDTH), lambda bb, i: (bb, jnp.maximum(i * hb - 1, 0), CB_POOL)),
            pl.BlockSpec((1, tm, POOL_WIDTH), lambda bb, i: (bb, i, CB_POOL)),
            pl.BlockSpec((1, hl, POOL_WIDTH), lambda bb, i: (bb, jnp.minimum((i + 1) * hb, last), CB_POOL)),
            pl.BlockSpec((len(POOL_WINDOWS), POOL_CH, POOL_CH), lambda bb, i: (0, 0, 0)),
            pl.BlockSpec((1, POOL_WIDTH), lambda bb, i: (0, 0)),
        ],
        out_specs=pl.BlockSpec((1, tm, POOL_WIDTH), lambda bb, i: (bb, i, 0)),
        out_shape=jax.ShapeDtypeStruct((b, n, POOL_WIDTH), BF16),
        scratch_shapes=[pltpu.VMEM((tm + 2 * hl, POOL_WIDTH), F32)],
        compiler_params=_cparams(("arbitrary", "arbitrary")),
        name="pool_mixer",
    )(p, p, p, pool_w, pool_scale.reshape(1, POOL_WIDTH))


NA_QROWS = 4
NA_UROWS = 12
NA_UNROLL = 2


def _na_plan(rows, kr):
    rq, ku = NA_QROWS, NA_UROWS
    if rows < ku or rows % rq:
        rq, ku = 1, kr
    starts, keys = [], []
    for r0 in range(0, rows, rq):
        rs = [min(max(r - kr // 2, 0), rows - kr) for r in range(r0, r0 + rq)]
        us = min(rs[0], rows - ku)
        starts.append(us)
        keys.append(tuple((us - r, rs_q - us) for r, rs_q in zip(range(r0, r0 + rq), rs)))
    tables = sorted(set(keys))
    table_of = np.array([tables.index(k) for k in keys], np.int32)
    return rq, ku, np.array(starts, np.int32), table_of, tables


def _na_bias(rpb, kr, ku, tables):
    col = np.arange(GRID_W)
    col_start = np.clip(col - NA_COLS // 2, 0, GRID_W - NA_COLS)
    dcol = col[None, :] - col[:, None] + (NA_COLS - 1)
    ok = (col[None, :] >= col_start[:, None]) & (col[None, :] < col_start[:, None] + NA_COLS)
    onehot = ((dcol[:, :, None] == np.arange(2 * NA_COLS - 1)) & ok[:, :, None]).astype(np.float32)
    colbias = jnp.einsum("hrc,qkc->hrqk", rpb.astype(F32), onehot, precision=lax.Precision.HIGHEST)
    colbias = jnp.where(ok[None, None], colbias, NEG_BIG)
    h = rpb.shape[0]
    masked = jnp.full((h, GRID_W, GRID_W), NEG_BIG, F32)
    out = []
    for key in tables:
        per_q = []
        for rel0, first in key:
            blocks = [colbias[:, rel0 + i + NA_ROWS - 1] if first <= i < first + kr else masked
                      for i in range(ku)]
            per_q.append(jnp.stack(blocks, axis=2))
        out.append(jnp.stack(per_q, axis=1))
    rq = len(tables[0])
    return jnp.stack(out).reshape(len(tables), h, rq * GRID_W, ku * GRID_W)


def _na_kernel(us_ref, tb_ref, q_ref, k_ref, v_ref, kc_ref, vc_ref, bias_ref, o_ref, *, nb, mq, mk):
    i = pl.program_id(2)
    kc = kc_ref[0]
    vc = vc_ref[0]

    def body(bb, carry):
        blk = i * nb + bb
        koff = pl.multiple_of(us_ref[blk] * GRID_W, GRID_W)
        qoff = pl.multiple_of(bb * mq, mq)
        q = q_ref[0, pl.ds(qoff, mq), :]
        s_loc = _dot_nt(q, k_ref[0, pl.ds(koff, mk), :]) + bias_ref[tb_ref[blk], 0]
        s_ctx = _dot_nt(q, kc)
        m = jnp.maximum(jnp.max(s_loc, axis=-1, keepdims=True), jnp.max(s_ctx, axis=-1, keepdims=True))
        p_loc = jnp.exp(s_loc - m)
        p_ctx = jnp.exp(s_ctx - m)
        den = jnp.sum(p_loc, axis=-1, keepdims=True) + jnp.sum(p_ctx, axis=-1, keepdims=True)
        o = _dot(p_loc.astype(BF16), v_ref[0, pl.ds(koff, mk), :]) + _dot(p_ctx.astype(BF16), vc)
        o_ref[0, pl.ds(qoff, mq), :] = (o / den).astype(BF16)
        return carry

    lax.fori_loop(0, nb, body, 0, unroll=min(NA_UNROLL, nb))


def _na(pq, pkv, pckv, rpb):
    b, n, _ = pq.shape
    nc = pckv.shape[1]
    rows = n // GRID_W
    kr = min(NA_ROWS, rows)
    rq, ku, starts, table_of, tables = _na_plan(rows, kr)
    bias = _na_bias(rpb, kr, ku, tables)
    nblk = rows // rq
    nb = min(4, nblk)
    assert nblk % nb == 0
    mq, mk = rq * GRID_W, ku * GRID_W
    return pl.pallas_call(
        functools.partial(_na_kernel, nb=nb, mq=mq, mk=mk),
        grid_spec=pltpu.PrefetchScalarGridSpec(
            num_scalar_prefetch=2, grid=(b, NA_HEADS, nblk // nb),
            in_specs=[
                pl.BlockSpec((1, nb * mq, HEAD_DIM), lambda bb, h, i, us, tb: (bb, i, HB_NA_Q + h)),
                pl.BlockSpec((1, n, HEAD_DIM), lambda bb, h, i, us, tb: (bb, 0, HB_NA_K + h)),
                pl.BlockSpec((1, n, HEAD_DIM), lambda bb, h, i, us, tb: (bb, 0, HB_NA_V + h)),
                pl.BlockSpec((1, nc, HEAD_DIM), lambda bb, h, i, us, tb: (bb, 0, HB_NA_K + h)),
                pl.BlockSpec((1, nc, HEAD_DIM), lambda bb, h, i, us, tb: (bb, 0, HB_NA_V + h)),
                pl.BlockSpec((len(tables), 1, mq, mk), lambda bb, h, i, us, tb: (0, h, 0, 0)),
            ],
            out_specs=pl.BlockSpec((1, nb * mq, HEAD_DIM), lambda bb, h, i, us, tb: (bb, i, h))),
        out_shape=jax.ShapeDtypeStruct((b, n, NA_WIDTH), BF16),
        compiler_params=_cparams(("arbitrary", "arbitrary", "arbitrary")),
        name="na_attn",
    )(jnp.asarray(starts), jnp.asarray(table_of), pq, pkv, pkv, pckv, pckv, bias)


GQA_ONES_ROWS = 16
GQA_CHUNKS_PER_TRIP = 4
GQA_Q_TILE = 512


def _gqa_kernel(bound_ref, q0_ref, q1_ref, q2_ref, k_ref, v_ref, kc_ref, vc_ref, o_ref,
                qt_s, vt_s, vct_s, s_s, sc_s, m_s, acc_s, *, tk, n, fixed_shift):
    i = pl.program_id(2)
    tq = q0_ref.shape[1]
    nchunk = n // tk
    hd = HEAD_DIM

    def to_t(a):
        return a.astype(F32).T.astype(BF16)

    @pl.when(i == 0)
    def _():
        def tr(c, carry):
            off = pl.multiple_of(c * tk, tk)
            vt_s[c, 0:hd, :] = to_t(v_ref[0, pl.ds(off, tk), :])
            vt_s[c, hd:, :] = jnp.ones((GQA_ONES_ROWS, tk), BF16)
            return carry
        lax.fori_loop(0, nchunk, tr, 0)
        vct_s[0:hd, :] = to_t(vc_ref[0])
        vct_s[hd:, :] = jnp.ones((GQA_ONES_ROWS, vct_s.shape[1]), BF16)

    for g, qr in enumerate((q0_ref, q1_ref, q2_ref)):
        qt_s[:, g * tq:(g + 1) * tq] = to_t(qr[0])

    def scores(c, slot):
        off = pl.multiple_of(c * tk, tk)
        s_s[slot] = _dot(k_ref[0, pl.ds(off, tk), :], qt_s[...])

    def update(s, vt, first=False):
        if fixed_shift:
            pv = _dot(vt, jnp.exp(s - bound_ref[0]).astype(BF16))
            acc_s[...] = pv if first else acc_s[...] + pv
            return
        smax = jnp.max(s, axis=0, keepdims=True)
        if first:
            m_new = smax
        else:
            m_prev = m_s[...]
            m_new = jnp.maximum(m_prev, smax)
            alpha = jnp.exp(m_prev - m_new)
        p = jnp.exp(s - m_new).astype(BF16)
        pv = _dot(vt, p)
        acc_s[...] = pv if first else alpha * acc_s[...] + pv
        m_s[...] = m_new

    scores(0, 0)
    sc_s[...] = _dot(kc_ref[0], qt_s[...])
    update(sc_s[...], vct_s[...], first=True)

    per_trip = GQA_CHUNKS_PER_TRIP if nchunk % GQA_CHUNKS_PER_TRIP == 0 else 2

    def body(ct, carry):
        c = per_trip * ct
        for u in range(per_trip):
            scores(jnp.minimum(c + u + 1, nchunk - 1), (u + 1) % 2)
            update(s_s[u % 2], vt_s[c + u])
        return carry

    lax.fori_loop(0, nchunk // per_trip, body, 0)
    acc = acc_s[...]
    o = (acc[0:hd] / acc[hd:hd + 1]).T
    for g in range(GQA_GROUP):
        o_ref[0, :, g * hd:(g + 1) * hd] = o[g * tq:(g + 1) * tq].astype(BF16)


GQA_FIXED_SHIFT_LIMIT = 40.0


def _gqa(pq, pkv, pckv, gain_q, gain_k):
    bound = 1.01 * HEAD_DIM * ATTN_SCALE * jnp.max(jnp.abs(gain_q)) * jnp.max(jnp.abs(gain_k))
    bound = bound.astype(F32).reshape(1)
    return lax.cond(bound[0] <= GQA_FIXED_SHIFT_LIMIT,
                    functools.partial(_gqa_call, fixed_shift=True),
                    functools.partial(_gqa_call, fixed_shift=False),
                    pq, pkv, pckv, bound)


def _gqa_call(pq, pkv, pckv, bound, *, fixed_shift):
    b, n, _ = pq.shape
    nc = pckv.shape[1]
    tq = min(GQA_Q_TILE, n)
    tk = min(512, n // 2)
    assert n % (2 * tk) == 0
    nq = GQA_GROUP * tq
    vr = HEAD_DIM + GQA_ONES_ROWS

    def qspec(g):
        return pl.BlockSpec((1, tq, HEAD_DIM), lambda bb, h, i: (bb, i, HB_GQA_Q + h * GQA_GROUP + g))

    return pl.pallas_call(
        functools.partial(_gqa_kernel, tk=tk, n=n, fixed_shift=fixed_shift),
        grid=(b, GQA_KV_HEADS, n // tq),
        in_specs=[
            pl.BlockSpec(memory_space=pltpu.SMEM),
            qspec(0), qspec(1), qspec(2),
            pl.BlockSpec((1, n, HEAD_DIM), lambda bb, h, i: (bb, 0, HB_GQA_K + h)),
            pl.BlockSpec((1, n, HEAD_DIM), lambda bb, h, i: (bb, 0, HB_GQA_V + h)),
            pl.BlockSpec((1, nc, HEAD_DIM), lambda bb, h, i: (bb, 0, HB_GQA_K + h)),
            pl.BlockSpec((1, nc, HEAD_DIM), lambda bb, h, i: (bb, 0, HB_GQA_V + h)),
        ],
        out_specs=pl.BlockSpec((1, tq, GQA_GROUP * HEAD_DIM), lambda bb, h, i: (bb, i, h)),
        out_shape=jax.ShapeDtypeStruct((b, n, GQA_Q_WIDTH), BF16),
        scratch_shapes=[
            pltpu.VMEM((HEAD_DIM, nq), BF16),
            pltpu.VMEM((n // tk, vr, tk), BF16),
            pltpu.VMEM((vr, nc), BF16),
            pltpu.VMEM((2, tk, nq), F32),
            pltpu.VMEM((nc, nq), F32),
            pltpu.VMEM((1, nq), F32),
            pltpu.VMEM((vr, nq), F32),
        ],
        compiler_params=_cparams(("arbitrary", "arbitrary", "arbitrary")),
        name="gqa_attn_fixed_shift" if fixed_shift else "gqa_attn_running_max",
    )(bound, pq, pq, pq, pkv, pkv, pckv, pckv)


def _ctx_attn_kernel(q_ref, k_ref, v_ref, o_ref):
    s = _dot_nt(q_ref[0], k_ref[0])
    m = jnp.max(s, axis=-1, keepdims=True)
    p = jnp.exp(s - m)
    den = jnp.sum(p, axis=-1, keepdims=True)
    o_ref[0] = (_dot(p.astype(BF16), v_ref[0]) / den).astype(BF16)


def _ctx_attn(pcq, pckv):
    b, nc, _ = pcq.shape
    nh = NA_HEADS + GQA_Q_HEADS

    def kmap(bb, h):
        g = jnp.maximum(h - NA_HEADS, 0) // GQA_GROUP
        return bb, 0, jnp.where(h < NA_HEADS, HB_NA_K + h, HB_GQA_K + g)

    def vmap_(bb, h):
        g = jnp.maximum(h - NA_HEADS, 0) // GQA_GROUP
        return bb, 0, jnp.where(h < NA_HEADS, HB_NA_V + h, HB_GQA_V + g)

    return pl.pallas_call(
        _ctx_attn_kernel,
        grid=(b, nh),
        in_specs=[
            pl.BlockSpec((1, nc, HEAD_DIM), lambda bb, h: (bb, 0, HB_NA_Q + h)),
            pl.BlockSpec((1, nc, HEAD_DIM), kmap),
            pl.BlockSpec((1, nc, HEAD_DIM), vmap_),
        ],
        out_specs=pl.BlockSpec((1, nc, HEAD_DIM), lambda bb, h: (bb, 0, h)),
        out_shape=jax.ShapeDtypeStruct((b, nc, nh * HEAD_DIM), BF16),
        compiler_params=_cparams(("arbitrary", "arbitrary")),
        name="ctx_attn",
    )(pcq, pckv, pckv)


def _merge_kernel(x_ref, ga_ref, gb_ref, gc_ref, yp_ref, yn_ref, yg_ref, wbr_ref, wout_ref, g1_ref, o_ref):
    r1 = POOL_WIDTH
    r2 = POOL_WIDTH + NA_WIDTH
    z = ga_ref[0].astype(F32) * _dot(yp_ref[0], wbr_ref[0, 0:r1, :])
    z = z + gb_ref[0].astype(F32) * _dot(yn_ref[0], wbr_ref[0, r1:r2, :])
    z = z + gc_ref[0].astype(F32) * _dot(yg_ref[0], wbr_ref[0, r2:, :])
    o_ref[0] = x_ref[0] + g1_ref[0] * _dot(z.astype(BF16), wout_ref[0])


def _merge(x, p, y_pool, y_na, na_cb, y_gqa, gqa_cb, w_br, w_out, layer, g1):
    b, n, d = x.shape
    tm = min(256, n)
    const = lambda bb, i: (layer, 0, 0)
    return pl.pallas_call(
        _merge_kernel,
        grid=(b, n // tm),
        in_specs=[
            pl.BlockSpec((1, tm, d), lambda bb, i: (bb, i, 0)),
            pl.BlockSpec((1, tm, d), lambda bb, i: (bb, i, 0)),
            pl.BlockSpec((1, tm, d), lambda bb, i: (bb, i, 1)),
            pl.BlockSpec((1, tm, d), lambda bb, i: (bb, i, 2)),
            pl.BlockSpec((1, tm, POOL_WIDTH), lambda bb, i: (bb, i, 0)),
            pl.BlockSpec((1, tm, NA_WIDTH), lambda bb, i: (bb, i, na_cb)),
            pl.BlockSpec((1, tm, GQA_Q_WIDTH), lambda bb, i: (bb, i, gqa_cb)),
            pl.BlockSpec((1,) + w_br.shape[1:], const, pipeline_mode=pl.Buffered(1)),
            pl.BlockSpec((1,) + w_out.shape[1:], const, pipeline_mode=pl.Buffered(1)),
            pl.BlockSpec((1, 1, d), lambda bb, i: (bb, 0, 0)),
        ],
        out_specs=pl.BlockSpec((1, tm, d), lambda bb, i: (bb, i, 0)),
        out_shape=jax.ShapeDtypeStruct((b, n, d), F32),
        compiler_params=_cparams(("arbitrary", "arbitrary")),
        name="branch_merge",
    )(x, p, p, p, y_pool, y_na, y_gqa, w_br, w_out, g1)


def _top2_of4(a, b, c, d):
    hi1, lo1 = jnp.maximum(a, b), jnp.minimum(a, b)
    hi2, lo2 = jnp.maximum(c, d), jnp.minimum(c, d)
    return jnp.maximum(hi1, hi2) + jnp.maximum(jnp.minimum(hi1, hi2), jnp.maximum(lo1, lo2))


def _router_kernel(x_ref, g_ref, sc_ref, sh_ref, whi_ref, wlo_ref, br_ref,
                   h_ref, e_ref, w_ref, rank_ref, cnt_ref, carry_ref):
    first = (pl.program_id(0) == 0) & (pl.program_id(1) == 0)

    @pl.when(first)
    def _():
        carry_ref[...] = jnp.zeros_like(carry_ref)

    x = x_ref[0]
    h = x * lax.rsqrt(jnp.mean(x * x, axis=-1, keepdims=True) + EPS) * g_ref[...]
    h = h * (1.0 + sc_ref[0]) + sh_ref[0]
    h_ref[0] = h
    h_hi = h.astype(BF16)
    h_lo = (h - h_hi.astype(F32)).astype(BF16)
    whi = whi_ref[...]
    logit = _dot_nt(whi, h_hi) + _dot_nt(whi, h_lo) + _dot_nt(wlo_ref[...], h_hi)
    s = jax.nn.sigmoid(logit)
    sel = s + br_ref[...]
    epg = EXPERTS_PER_GROUP
    row = lambda a, e: a[e:e + 1, :]
    gscore = [_top2_of4(*[row(sel, g * epg + j) for j in range(epg)]) for g in range(N_GROUPS)]
    g_best = jnp.zeros_like(gscore[0], dtype=jnp.int32)
    best = gscore[0]
    for g in range(1, N_GROUPS):
        upd = gscore[g] > best
        g_best = jnp.where(upd, g, g_best)
        best = jnp.where(upd, gscore[g], best)
    vs, ss = [], []
    for j in range(epg):
        v = row(sel, j)
        sv = row(s, j)
        for g in range(1, N_GROUPS):
            v = jnp.where(g_best == g, row(sel, g * epg + j), v)
            sv = jnp.where(g_best == g, row(s, g * epg + j), sv)
        vs.append(v)
        ss.append(sv)
    i1 = jnp.zeros_like(g_best)
    v1 = vs[0]
    for j in range(1, epg):
        upd = vs[j] > v1
        i1 = jnp.where(upd, j, i1)
        v1 = jnp.where(upd, vs[j], v1)
    i2 = jnp.full_like(g_best, -1)
    v2 = jnp.full_like(v1, -jnp.inf)
    for j in range(epg):
        upd = (i1 != j) & ((i2 < 0) | (vs[j] > v2))
        i2 = jnp.where(upd, j, i2)
        v2 = jnp.where(upd, vs[j], v2)
    w1 = sum(jnp.where(i1 == j, ss[j], 0.0) for j in range(epg))
    w2 = sum(jnp.where(i2 == j, ss[j], 0.0) for j in range(epg))
    tot = w1 + w2
    w_ref[0] = jnp.concatenate([w1 / tot, w2 / tot], axis=0)
    e1 = g_best * epg + i1
    e2 = g_best * epg + i2
    e_ref[0] = jnp.concatenate([e1, e2], axis=0)

    tm = x.shape[0]
    eidx = lax.broadcasted_iota(jnp.int32, (N_EXPERTS, tm), 0)
    oh1 = eidx == e1
    oh2 = eidx == e2
    oh = jnp.where(oh1 | oh2, 1.0, 0.0)
    before = lax.broadcasted_iota(jnp.int32, (tm, tm), 0) < lax.broadcasted_iota(jnp.int32, (tm, tm), 1)
    prefix = _dot(oh.astype(BF16), jnp.where(before, 1.0, 0.0).astype(BF16))
    base = carry_ref[...] + prefix
    r1 = jnp.sum(jnp.where(oh1, base, 0.0), axis=0, keepdims=True)
    r2 = jnp.sum(jnp.where(oh2, base, 0.0), axis=0, keepdims=True)
    rank_ref[0] = jnp.concatenate([r1, r2], axis=0).astype(jnp.int32)
    carry = carry_ref[...] + jnp.sum(oh, axis=1, keepdims=True)
    carry_ref[...] = carry
    cnt_ref[...] = jnp.broadcast_to(carry, cnt_ref.shape).astype(jnp.int32)


def _router(x, gain, sc, sh, wr_hi, wr_lo, b_router):
    b, n, d = x.shape
    tm = min(512, n)
    ne = wr_hi.shape[0]
    pair = pl.BlockSpec((1, 2, tm), lambda bb, i: (bb, 0, i))
    return pl.pallas_call(
        _router_kernel,
        grid=(b, n // tm),
        in_specs=[
            pl.BlockSpec((1, tm, d), lambda bb, i: (bb, i, 0)),
            pl.BlockSpec((1, d), lambda bb, i: (0, 0)),
            pl.BlockSpec((1, 1, d), lambda bb, i: (bb, 0, 0)),
            pl.BlockSpec((1, 1, d), lambda bb, i: (bb, 0, 0)),
            pl.BlockSpec((ne, d), lambda bb, i: (0, 0)),
            pl.BlockSpec((ne, d), lambda bb, i: (0, 0)),
            pl.BlockSpec((ne, 1), lambda bb, i: (0, 0)),
        ],
        out_specs=[
            pl.BlockSpec((1, tm, d), lambda bb, i: (bb, i, 0)),
            pair, pair, pair,
            pl.BlockSpec((ne, HEAD_DIM), lambda bb, i: (0, 0)),
        ],
        out_shape=[
            jax.ShapeDtypeStruct((b, n, d), F32),
            jax.ShapeDtypeStruct((b, 2, n), jnp.int32),
            jax.ShapeDtypeStruct((b, 2, n), F32),
            jax.ShapeDtypeStruct((b, 2, n), jnp.int32),
            jax.ShapeDtypeStruct((ne, HEAD_DIM), jnp.int32),
        ],
        scratch_shapes=[pltpu.VMEM((ne, 1), F32)],
        compiler_params=_cparams(("arbitrary", "arbitrary")),
        name="norm_router",
    )(x, gain.reshape(1, d), sc, sh, wr_hi, wr_lo, b_router.reshape(ne, 1))


MOE_TILE = 256
MOE_SCATTER_TILE = 512
MOE_COMBINE_TILE = 256
MOE_DMA_UNROLL = 8


def _moe_plan(e, rank, cnt):
    b, _, n = e.shape
    t = b * n
    counts = cnt[:, 0]
    ntile_e = (counts + MOE_TILE - 1) // MOE_TILE
    tile_end = jnp.cumsum(ntile_e)
    off = (tile_end - ntile_e) * MOE_TILE
    nt = 2 * t // MOE_TILE + N_EXPERTS
    tile_expert = jnp.sum(jnp.arange(nt)[:, None] >= tile_end[None, :], axis=1)
    tile_expert = jnp.minimum(tile_expert, N_EXPERTS - 1).astype(jnp.int32)
    ef = e.transpose(1, 0, 2).reshape(2, t)
    rf = rank.transpose(1, 0, 2).reshape(2, t)
    pos = rf + jnp.sum(jnp.where(ef[..., None] == jnp.arange(N_EXPERTS), off, 0), axis=-1)
    return pos.reshape(2 * t).astype(jnp.int32), tile_expert, tile_end[-1:].astype(jnp.int32), nt


def _scatter_kernel(pos_ref, h_ref, xs0_hbm, xs_hbm, sem, *, t):
    del xs0_hbm
    tm = h_ref.shape[0]
    base = pl.program_id(0) * tm

    def copies(j):
        return [pltpu.make_async_copy(h_ref.at[pl.ds(j, 1)], xs_hbm.at[pl.ds(pos_ref[k * t + base + j], 1)], sem)
                for k in range(2)]

    def start(j, c):
        for cp in copies(j):
            cp.start()
        return c

    def wait(j, c):
        for cp in copies(j):
            cp.wait()
        return c

    lax.fori_loop(0, tm, start, 0, unroll=MOE_DMA_UNROLL)
    lax.fori_loop(0, tm, wait, 0, unroll=MOE_DMA_UNROLL)


def _scatter_rows(pos, h, xs_buf):
    t, d = h.shape
    tm = min(MOE_SCATTER_TILE, t)
    any_spec = pl.BlockSpec(memory_space=pl.ANY)
    return pl.pallas_call(
        functools.partial(_scatter_kernel, t=t),
        grid_spec=pltpu.PrefetchScalarGridSpec(
            num_scalar_prefetch=1, grid=(t // tm,),
            in_specs=[pl.BlockSpec((tm, d), lambda i, p: (i, 0)), any_spec], out_specs=any_spec,
            scratch_shapes=[pltpu.SemaphoreType.DMA(())]),
        out_shape=jax.ShapeDtypeStruct(xs_buf.shape, F32),
        input_output_aliases={2: 0},
        compiler_params=pltpu.CompilerParams(dimension_semantics=("arbitrary",), has_side_effects=True),
        name="moe_scatter",
    )(pos, h, xs_buf)


def _experts_kernel(te_ref, nv_ref, xs_ref, wgu_ref, wd_ref, ys_ref):
    del te_ref

    @pl.when(pl.program_id(0) < nv_ref[0])
    def _():
        gu = _dot(xs_ref[...].astype(BF16), wgu_ref[0, 0])
        ff = gu.shape[1] // 2
        gate = gu[:, :ff]
        a = (gate * jax.nn.sigmoid(gate) * gu[:, ff:]).astype(BF16)
        ys_ref[...] = _dot(a, wd_ref[0, 0])


def _experts(xs, tile_expert, nvalid, w_gu, w_down, layer):
    nrows, d = xs.shape
    nt = nrows // MOE_TILE
    f2 = w_gu.shape[-1]
    row = lambda i, te, nv: (jnp.minimum(i, nv[0] - 1), 0)
    wmap = lambda i, te, nv: (layer, te[jnp.minimum(i, nv[0] - 1)], 0, 0)
    return pl.pallas_call(
        _experts_kernel,
        grid_spec=pltpu.PrefetchScalarGridSpec(
            num_scalar_prefetch=2, grid=(nt,),
            in_specs=[
                pl.BlockSpec((MOE_TILE, d), row),
                pl.BlockSpec((1, 1, d, f2), wmap),
                pl.BlockSpec((1, 1, f2 // 2, d), wmap),
            ],
            out_specs=pl.BlockSpec((MOE_TILE, d), row)),
        out_shape=jax.ShapeDtypeStruct((nrows, d), F32),
        compiler_params=_cparams(("arbitrary",)),
        name="moe_experts",
    )(tile_expert, nvalid, xs, w_gu, w_down)


def _combine_kernel(pos_ref, x_ref, w_ref, g2_ref, ys_hbm, o_ref, buf, sem, *, t):
    i = pl.program_id(0)
    nsteps = pl.num_programs(0)
    tm = x_ref.shape[0]

    def copies(tile, slot, j):
        tok = tile * tm + j
        return [pltpu.make_async_copy(ys_hbm.at[pl.ds(pos_ref[k * t + tok], 1)],
                                      buf.at[slot, k, pl.ds(j, 1)], sem.at[slot]) for k in range(2)]

    def issue(tile, slot):
        def body(j, c):
            for cp in copies(tile, slot, j):
                cp.start()
            return c
        lax.fori_loop(0, tm, body, 0, unroll=MOE_DMA_UNROLL)

    def wait(tile, slot):
        def body(j, c):
            for cp in copies(tile, slot, j):
                cp.wait()
            return c
        lax.fori_loop(0, tm, body, 0, unroll=MOE_DMA_UNROLL)

    @pl.when(i == 0)
    def _():
        issue(0, 0)

    @pl.when(i + 1 < nsteps)
    def _():
        issue(i + 1, (i + 1) % 2)

    slot = i % 2
    wait(i, slot)
    w = w_ref[...]
    y = w[:, 0:1] * buf[slot, 0] + w[:, 1:2] * buf[slot, 1]
    o_ref[...] = x_ref[...] + g2_ref[0] * y


def _combine(pos, x, w, g2, ys):
    b, n, d = x.shape
    t = b * n
    tm = min(MOE_COMBINE_TILE, n)
    per_b = n // tm
    out = pl.pallas_call(
        functools.partial(_combine_kernel, t=t),
        grid_spec=pltpu.PrefetchScalarGridSpec(
            num_scalar_prefetch=1, grid=(t // tm,),
            in_specs=[
                pl.BlockSpec((tm, d), lambda i, p: (i, 0)),
                pl.BlockSpec((tm, 2), lambda i, p: (i, 0)),
                pl.BlockSpec((1, 1, d), lambda i, p: (i // per_b, 0, 0)),
                pl.BlockSpec(memory_space=pl.ANY),
            ],
            out_specs=pl.BlockSpec((tm, d), lambda i, p: (i, 0)),
            scratch_shapes=[pltpu.VMEM((2, 2, tm, d), F32), pltpu.SemaphoreType.DMA((2,))]),
        out_shape=jax.ShapeDtypeStruct((t, d), F32),
        compiler_params=_cparams(("arbitrary",)),
        name="moe_combine",
    )(pos, x.reshape(t, d), w, g2, ys)
    return out.reshape(b, n, d)


def _moe(x, gain, sc, sh, g2, wr_hi, wr_lo, b_router, w_gu, w_down, layer, xs_buf):
    b, n, d = x.shape
    h, e, w, rank, cnt = _router(x, gain, sc, sh, wr_hi, wr_lo, b_router)
    pos, tile_expert, nvalid, nt = _moe_plan(e, rank, cnt)
    if xs_buf is None:
        xs_buf = jnp.zeros((nt * MOE_TILE, d), F32)
    xs = _scatter_rows(pos, h.reshape(b * n, d), xs_buf)
    ys = _experts(xs, tile_expert, nvalid, w_gu, w_down, layer)
    return _combine(pos, x, w.transpose(0, 2, 1).reshape(b * n, 2), g2, ys), xs


def _rope_tables(n):
    t = jnp.arange(n, dtype=jnp.int32)
    row = (t // GRID_W).astype(F32)
    col = (t % GRID_W).astype(F32)
    axis_dim = HEAD_DIM // 2
    inv = ROPE_THETA ** (-jnp.arange(0, axis_dim, 2, dtype=F32) / axis_dim)
    ang = jnp.concatenate([row[:, None] * inv, col[:, None] * inv], axis=-1)
    cos, sin = jnp.cos(ang), jnp.sin(ang)
    return jnp.concatenate([cos, cos], axis=-1), jnp.concatenate([-sin, sin], axis=-1)


def kernel(x, c, ctx, c_ctx, w_mod, b_mod, norm1, norm2, w_in, qk_gain, pool_w, pool_scale,
           na_rpb, w_br, w_out, w_router, b_router, w_gu, w_down):
    b, n, d = x.shape
    depth = w_mod.shape[0]
    rows = n // GRID_W
    kr = min(NA_ROWS, rows)
    assert n % GRID_W == 0 and rows % kr == 0 and b + 1 <= 8

    cos, sin = _rope_tables(n)
    mods = _modulation(jnp.concatenate([c, c_ctx[None, :]], axis=0), w_mod, b_mod)
    wr_t = w_router.T
    wr_hi = wr_t.astype(BF16)
    wr_lo = (wr_t - wr_hi.astype(F32)).astype(BF16)
    w_br_b, w_out_b = w_br.astype(BF16), w_out.astype(BF16)
    w_gu_b, w_down_b = w_gu.astype(BF16), w_down.astype(BF16)
    xs_x = xs_c = None

    for l in range(depth):
        last = l == depth - 1
        mx = mods[l, :b].reshape(b, 1, N_MOD, d)
        mc = jnp.broadcast_to(mods[l, b].reshape(1, 1, N_MOD, d), (b, 1, N_MOD, d))
        x_sh1, x_sc1, x_g1, x_sh2, x_sc2, x_g2 = [mx[:, :, k] for k in range(N_MOD)]
        c_sh1, c_sc1, c_g1, c_sh2, c_sc2, c_g2 = [mc[:, :, k] for k in range(N_MOD)]
        pool_w_l = pool_w[l].astype(BF16)

        def proj(t, sc, sh, kind, tables=(None, None)):
            return _inproj(t, norm1[l], sc, sh, w_in, l, qk_gain[l], *tables, kind=kind)

        pckv = proj(ctx, c_sc1, c_sh1, "kv")
        pq = proj(x, x_sc1, x_sh1, "q", (cos, sin))
        pkv = proj(x, x_sc1, x_sh1, "kv", (cos, sin))
        pg = proj(x, x_sc1, x_sh1, "gates")
        y_pool = _pool(pq, pool_w_l, pool_scale[l])
        y_na = _na(pq, pkv, pckv, na_rpb[l])
        y_gqa = _gqa(pq, pkv, pckv, qk_gain[l, 2], qk_gain[l, 3])
        x = _merge(x, pg, y_pool, y_na, 0, y_gqa, 0, w_br_b, w_out_b, l, x_g1)

        moe_w = (wr_hi, wr_lo, b_router, w_gu_b, w_down_b, l)
        if not last:
            pcq = proj(ctx, c_sc1, c_sh1, "q")
            pcg = proj(ctx, c_sc1, c_sh1, "gates")
            yc_pool = _pool(pcq, pool_w_l, pool_scale[l])
            yc = _ctx_attn(pcq, pckv)
            ctx = _merge(ctx, pcg, yc_pool, yc, 0, yc, 1, w_br_b, w_out_b, l, c_g1)
            ctx, xs_c = _moe(ctx, norm2[l], c_sc2, c_sh2, c_g2, *moe_w, xs_c)
        x, xs_x = _moe(x, norm2[l], x_sc2, x_sh2, x_g2, *moe_w, xs_x)
    return x
```

```python
import functools

import numpy as np
import jax
import jax.numpy as jnp
from jax import lax
from jax.experimental import pallas as pl
from jax.experimental.pallas import tpu as pltpu

F32 = jnp.float32
BF16 = jnp.bfloat16

GRID_W = 64
HEAD_DIM = 128
ROPE_THETA = 10000.0
EPS = 1e-6
POOL_WINDOWS = (2, 4, 8, 16)
POOL_CH = 128
POOL_WIDTH = len(POOL_WINDOWS) * POOL_CH
NA_HEADS = 6
NA_WIDTH = NA_HEADS * HEAD_DIM
NA_ROWS = 8
NA_COLS = 16
GQA_Q_HEADS = 6
GQA_KV_HEADS = 2
GQA_GROUP = GQA_Q_HEADS // GQA_KV_HEADS
GQA_Q_WIDTH = GQA_Q_HEADS * HEAD_DIM
N_BRANCH = 3
N_EXPERTS = 16
N_GROUPS = 4
EXPERTS_PER_GROUP = N_EXPERTS // N_GROUPS
N_MOD = 6
ATTN_SCALE = HEAD_DIM ** -0.5

COL_BLOCK = 2048
CB_POOL = 0
HB_NA_Q = POOL_WIDTH // HEAD_DIM
HB_GQA_Q = HB_NA_Q + NA_HEADS
HB_NA_K = 0
HB_NA_V = HB_NA_K + NA_HEADS
HB_GQA_K = HB_NA_V + NA_HEADS
HB_GQA_V = HB_GQA_K + GQA_KV_HEADS
PROJ_KINDS = {"gates": (0, 3), "q": (3, 1), "kv": (4, 1)}

V7X_VMEM_LIMIT = 56 * 1024 * 1024
NEG_BIG = -1e30


def _cparams(sem):
    return pltpu.CompilerParams(dimension_semantics=sem, vmem_limit_bytes=V7X_VMEM_LIMIT)


def _dot(a, b):
    return jnp.dot(a, b, preferred_element_type=F32)


def _dot_nt(a, b):
    return lax.dot_general(a, b, (((1,), (1,)), ((), ())), preferred_element_type=F32)


def _mod_kernel(ct_ref, w_ref, b_ref, o_ref, *, n_rows):
    ct = ct_ref[...]
    a = ct * jax.nn.sigmoid(ct)
    w = w_ref[0]
    rows = [jnp.sum(w * a[:, r:r + 1], axis=0, keepdims=True) for r in range(n_rows)]
    rows += [jnp.zeros_like(rows[0])] * (8 - n_rows)
    o_ref[0] = jnp.concatenate(rows, axis=0) + b_ref[0]


def _modulation(c_rows, w_mod, b_mod):
    depth, d, nm = w_mod.shape
    n_rows = c_rows.shape[0]
    ct = jnp.zeros((d, 8), F32).at[:, :n_rows].set(c_rows.T)
    tn = 1024
    return pl.pallas_call(
        functools.partial(_mod_kernel, n_rows=n_rows),
        grid=(depth, nm // tn),
        in_specs=[
            pl.BlockSpec((d, 8), lambda l, j: (0, 0)),
            pl.BlockSpec((1, d, tn), lambda l, j: (l, 0, j)),
            pl.BlockSpec((1, 1, tn), lambda l, j: (l, 0, j)),
        ],
        out_specs=pl.BlockSpec((1, 8, tn), lambda l, j: (l, 0, j)),
        out_shape=jax.ShapeDtypeStruct((depth, 8, nm), F32),
        compiler_params=_cparams(("arbitrary", "arbitrary")),
        name="adaln_mod",
    )(ct, w_mod, b_mod.reshape(depth, 1, nm))


def _head_norm(a, gain):
    return a * lax.rsqrt(jnp.mean(a * a, axis=-1, keepdims=True) + EPS) * gain


def _rope(y, cos, sin):
    return y * cos + pltpu.roll(y, HEAD_DIM // 2, 1) * sin


PROJ_CHUNK = 2 * HEAD_DIM
PROJ_ROW_SPLIT = 2


def _proj_epilogue(kind, head, a, qg_ref, rope_tables):
    if kind == "gates":
        return jax.nn.sigmoid(a)
    if kind == "q":
        if head < HB_NA_Q:
            return a
        if head < HB_GQA_Q:
            return _head_norm(a, qg_ref[0:1, :]) * ATTN_SCALE
        y = _head_norm(a, qg_ref[2:3, :])
        return (_rope(y, *rope_tables) if rope_tables else y) * ATTN_SCALE
    if head < HB_NA_V:
        return _head_norm(a, qg_ref[1:2, :])
    if HB_GQA_K <= head < HB_GQA_V:
        y = _head_norm(a, qg_ref[3:4, :])
        return _rope(y, *rope_tables) if rope_tables else y
    return a


def _inproj_kernel(*refs, kind, rope):
    if rope:
        x_ref, g_ref, sc_ref, sh_ref, w_ref, qg_ref, cos_ref, sin_ref, o_ref, wb_s = refs
    else:
        x_ref, g_ref, sc_ref, sh_ref, w_ref, qg_ref, o_ref, wb_s = refs

    @pl.when((pl.program_id(1) == 0) & (pl.program_id(2) == 0))
    def _():
        wb_s[...] = w_ref[0].astype(BF16)

    tm = x_ref.shape[1]
    nsub = PROJ_ROW_SPLIT if tm % (PROJ_ROW_SPLIT * 16) == 0 else 1
    ts = tm // nsub
    per = PROJ_CHUNK // HEAD_DIM
    for r in range(nsub):
        rows = slice(r * ts, (r + 1) * ts)
        x = x_ref[0, rows, :]
        h = x * lax.rsqrt(jnp.mean(x * x, axis=-1, keepdims=True) + EPS) * g_ref[...]
        hb = (h * (1.0 + sc_ref[0]) + sh_ref[0]).astype(BF16)
        tables = (cos_ref[rows, :], sin_ref[rows, :]) if rope else None
        for c in range(COL_BLOCK // PROJ_CHUNK):
            acc = _dot(hb, wb_s[:, c * PROJ_CHUNK:(c + 1) * PROJ_CHUNK])
            for u in range(per):
                head = c * per + u
                y = _proj_epilogue(kind, head, acc[:, u * HEAD_DIM:(u + 1) * HEAD_DIM], qg_ref, tables)
                o_ref[0, rows, head * HEAD_DIM:(head + 1) * HEAD_DIM] = y.astype(BF16)


def _inproj(x, gain, sc, sh, w_in, layer, qgain, cos, sin, *, kind):
    b, n, d = x.shape
    rope = cos is not None and kind != "gates"
    tm = min(512, n)
    j0, nj = PROJ_KINDS[kind]
    in_specs = [
        pl.BlockSpec((1, tm, d), lambda j, bb, i: (bb, i, 0)),
        pl.BlockSpec((1, d), lambda j, bb, i: (0, 0)),
        pl.BlockSpec((1, 1, d), lambda j, bb, i: (bb, 0, 0)),
        pl.BlockSpec((1, 1, d), lambda j, bb, i: (bb, 0, 0)),
        pl.BlockSpec((1, d, COL_BLOCK), lambda j, bb, i: (layer, 0, j + j0), pipeline_mode=pl.Buffered(1)),
        pl.BlockSpec((4, HEAD_DIM), lambda j, bb, i: (0, 0)),
    ]
    args = [x, gain.reshape(1, d), sc, sh, w_in, qgain]
    if rope:
        in_specs += [pl.BlockSpec((tm, HEAD_DIM), lambda j, bb, i: (i, 0))] * 2
        args += [cos, sin]
    return pl.pallas_call(
        functools.partial(_inproj_kernel, kind=kind, rope=rope),
        grid=(nj, b, n // tm),
        in_specs=in_specs,
        out_specs=pl.BlockSpec((1, tm, COL_BLOCK), lambda j, bb, i: (bb, i, j)),
        out_shape=jax.ShapeDtypeStruct((b, n, nj * COL_BLOCK), BF16),
        scratch_shapes=[pltpu.VMEM((d, COL_BLOCK), BF16)],
        compiler_params=_cparams(("arbitrary", "arbitrary", "arbitrary")),
        name="inproj_" + kind + ("_rope" if rope else ""),
    )(*args)


POOL_HALO = 16


def _pool_kernel(prev_ref, cur_ref, next_ref, w_ref, s_ref, o_ref, buf_ref, *, tm, n):
    i = pl.program_id(1)
    nt = pl.num_programs(1)
    hl = POOL_HALO
    buf_ref[pl.ds(hl, tm), :] = cur_ref[0].astype(F32)
    buf_ref[pl.ds(0, hl), :] = jnp.where(i > 0, prev_ref[0].astype(F32), 0.0)
    buf_ref[pl.ds(hl + tm, hl), :] = jnp.where(i < nt - 1, next_ref[0].astype(F32), 0.0)
    t = i * tm + lax.broadcasted_iota(jnp.int32, (tm, 1), 0)
    for g, w in enumerate(POOL_WINDOWS):
        sl = slice(g * POOL_CH, (g + 1) * POOL_CH)
        acc = buf_ref[pl.ds(hl - w // 2, tm), sl]
        for off in range(-w // 2 + 1, w // 2):
            acc = acc + buf_ref[pl.ds(hl + off, tm), sl]
        cnt = (jnp.minimum(t + w // 2, n) - jnp.maximum(t - w // 2, 0)).astype(F32)
        dlt = acc / cnt - buf_ref[pl.ds(hl, tm), sl]
        y = _dot(dlt.astype(BF16), w_ref[g]) * s_ref[:, sl]
        o_ref[0, :, sl] = y.astype(BF16)


def _pool(p, pool_w, pool_scale):
    b, n, _ = p.shape
    tm = min(512, n)
    hl = POOL_HALO
    hb = tm // hl
    last = n // hl - 1
    return pl.pallas_call(
        functools.partial(_pool_kernel, tm=tm, n=n),
        grid=(b, n // tm),
        in_specs=[
            pl.BlockSpec((1, hl, POOL_WIDTH), lambda bb, i: (bb, jnp.maximum(i * hb - 1, 0), CB_POOL)),
            pl.BlockSpec((1, tm, POOL_WIDTH), lambda bb, i: (bb, i, CB_POOL)),
            pl.BlockSpec((1, hl, POOL_WIDTH), lambda bb, i: (bb, jnp.minimum((i + 1) * hb, last), CB_POOL)),
            pl.BlockSpec((len(POOL_WINDOWS), POOL_CH, POOL_CH), lambda bb, i: (0, 0, 0)),
            pl.BlockSpec((1, POOL_WIDTH), lambda bb, i: (0, 0)),
        ],
        out_specs=pl.BlockSpec((1, tm, POOL_WIDTH), lambda bb, i: (bb, i, 0)),
        out_shape=jax.ShapeDtypeStruct((b, n, POOL_WIDTH), BF16),
        scratch_shapes=[pltpu.VMEM((tm + 2 * hl, POOL_WIDTH), F32)],
        compiler_params=_cparams(("arbitrary", "arbitrary")),
        name="pool_mixer",
    )(p, p, p, pool_w, pool_scale.reshape(1, POOL_WIDTH))


NA_QROWS = 4
NA_UROWS = 12
NA_UNROLL = 2


def _na_plan(rows, kr):
    rq, ku = NA_QROWS, NA_UROWS
    if rows < ku or rows % rq:
        rq, ku = 1, kr
    starts, keys = [], []
    for r0 in range(0, rows, rq):
        rs = [min(max(r - kr // 2, 0), rows - kr) for r in range(r0, r0 + rq)]
        us = min(rs[0], rows - ku)
        starts.append(us)
        keys.append(tuple((us - r, rs_q - us) for r, rs_q in zip(range(r0, r0 + rq), rs)))
    tables = sorted(set(keys))
    table_of = np.array([tables.index(k) for k in keys], np.int32)
    return rq, ku, np.array(starts, np.int32), table_of, tables


def _na_bias(rpb, kr, ku, tables):
    col = np.arange(GRID_W)
    col_start = np.clip(col - NA_COLS // 2, 0, GRID_W - NA_COLS)
    dcol = col[None, :] - col[:, None] + (NA_COLS - 1)
    ok = (col[None, :] >= col_start[:, None]) & (col[None, :] < col_start[:, None] + NA_COLS)
    onehot = ((dcol[:, :, None] == np.arange(2 * NA_COLS - 1)) & ok[:, :, None]).astype(np.float32)
    colbias = jnp.einsum("hrc,qkc->hrqk", rpb.astype(F32), onehot, precision=lax.Precision.HIGHEST)
    colbias = jnp.where(ok[None, None], colbias, NEG_BIG)
    h = rpb.shape[0]
    masked = jnp.full((h, GRID_W, GRID_W), NEG_BIG, F32)
    out = []
    for key in tables:
        per_q = []
        for rel0, first in key:
            blocks = [colbias[:, rel0 + i + NA_ROWS - 1] if first <= i < first + kr else masked
                      for i in range(ku)]
            per_q.append(jnp.stack(blocks, axis=2))
        out.append(jnp.stack(per_q, axis=1))
    rq = len(tables[0])
    return jnp.stack(out).reshape(len(tables), h, rq * GRID_W, ku * GRID_W)


def _na_kernel(us_ref, tb_ref, q_ref, k_ref, v_ref, kc_ref, vc_ref, bias_ref, o_ref, *, nb, mq, mk):
    i = pl.program_id(2)
    kc = kc_ref[0]
    vc = vc_ref[0]

    def body(bb, carry):
        blk = i * nb + bb
        koff = pl.multiple_of(us_ref[blk] * GRID_W, GRID_W)
        qoff = pl.multiple_of(bb * mq, mq)
        q = q_ref[0, pl.ds(qoff, mq), :]
        s_loc = _dot_nt(q, k_ref[0, pl.ds(koff, mk), :]) + bias_ref[tb_ref[blk], 0]
        s_ctx = _dot_nt(q, kc)
        m = jnp.maximum(jnp.max(s_loc, axis=-1, keepdims=True), jnp.max(s_ctx, axis=-1, keepdims=True))
        p_loc = jnp.exp(s_loc - m)
        p_ctx = jnp.exp(s_ctx - m)
        den = jnp.sum(p_loc, axis=-1, keepdims=True) + jnp.sum(p_ctx, axis=-1, keepdims=True)
        o = _dot(p_loc.astype(BF16), v_ref[0, pl.ds(koff, mk), :]) + _dot(p_ctx.astype(BF16), vc)
        o_ref[0, pl.ds(qoff, mq), :] = (o / den).astype(BF16)
        return carry

    lax.fori_loop(0, nb, body, 0, unroll=min(NA_UNROLL, nb))


def _na(pq, pkv, pckv, rpb):
    b, n, _ = pq.shape
    nc = pckv.shape[1]
    rows = n // GRID_W
    kr = min(NA_ROWS, rows)
    rq, ku, starts, table_of, tables = _na_plan(rows, kr)
    bias = _na_bias(rpb, kr, ku, tables)
    nblk = rows // rq
    nb = min(4, nblk)
    assert nblk % nb == 0
    mq, mk = rq * GRID_W, ku * GRID_W
    return pl.pallas_call(
        functools.partial(_na_kernel, nb=nb, mq=mq, mk=mk),
        grid_spec=pltpu.PrefetchScalarGridSpec(
            num_scalar_prefetch=2, grid=(b, NA_HEADS, nblk // nb),
            in_specs=[
                pl.BlockSpec((1, nb * mq, HEAD_DIM), lambda bb, h, i, us, tb: (bb, i, HB_NA_Q + h)),
                pl.BlockSpec((1, n, HEAD_DIM), lambda bb, h, i, us, tb: (bb, 0, HB_NA_K + h)),
                pl.BlockSpec((1, n, HEAD_DIM), lambda bb, h, i, us, tb: (bb, 0, HB_NA_V + h)),
                pl.BlockSpec((1, nc, HEAD_DIM), lambda bb, h, i, us, tb: (bb, 0, HB_NA_K + h)),
                pl.BlockSpec((1, nc, HEAD_DIM), lambda bb, h, i, us, tb: (bb, 0, HB_NA_V + h)),
                pl.BlockSpec((len(tables), 1, mq, mk), lambda bb, h, i, us, tb: (0, h, 0, 0)),
            ],
            out_specs=pl.BlockSpec((1, nb * mq, HEAD_DIM), lambda bb, h, i, us, tb: (bb, i, h))),
        out_shape=jax.ShapeDtypeStruct((b, n, NA_WIDTH), BF16),
        compiler_params=_cparams(("arbitrary", "arbitrary", "arbitrary")),
        name="na_attn",
    )(jnp.asarray(starts), jnp.asarray(table_of), pq, pkv, pkv, pckv, pckv, bias)


GQA_CHUNKS_PER_TRIP = 4
GQA_Q_TILE = 512


def _gqa_kernel(bound_ref, q0_ref, q1_ref, q2_ref, k_ref, v_ref, kc_ref, vc_ref, o_ref,
                qt_s, vt_s, vct_s, s_s, sc_s, m_s, den_s, acc_s, *, tk, n, fixed_shift):
    i = pl.program_id(2)
    tq = q0_ref.shape[1]
    nchunk = n // tk
    hd = HEAD_DIM

    def to_t(a):
        return a.astype(F32).T.astype(BF16)

    @pl.when(i == 0)
    def _():
        def tr(c, carry):
            off = pl.multiple_of(c * tk, tk)
            vt_s[c] = to_t(v_ref[0, pl.ds(off, tk), :])
            return carry
        lax.fori_loop(0, nchunk, tr, 0)
        vct_s[...] = to_t(vc_ref[0])

    for g, qr in enumerate((q0_ref, q1_ref, q2_ref)):
        qt_s[:, g * tq:(g + 1) * tq] = to_t(qr[0])

    def scores(c, slot):
        off = pl.multiple_of(c * tk, tk)
        s_s[slot] = _dot(k_ref[0, pl.ds(off, tk), :], qt_s[...])

    def col_sums(p):
        return jnp.sum(p.reshape(p.shape[0] // 8, 8, p.shape[1]), axis=0)

    def update(s, vt, first=False):
        if fixed_shift:
            p = jnp.exp(s - bound_ref[0])
            pv = _dot(vt, p.astype(BF16))
            den_s[...] = col_sums(p) if first else den_s[...] + col_sums(p)
            acc_s[...] = pv if first else acc_s[...] + pv
            return
        smax = jnp.max(s, axis=0, keepdims=True)
        if first:
            m_new = smax
        else:
            m_prev = m_s[...]
            m_new = jnp.maximum(m_prev, smax)
            alpha = jnp.exp(m_prev - m_new)
        p = jnp.exp(s - m_new)
        pv = _dot(vt, p.astype(BF16))
        den_s[...] = col_sums(p) if first else alpha * den_s[...] + col_sums(p)
        acc_s[...] = pv if first else alpha * acc_s[...] + pv
        m_s[...] = m_new

    scores(0, 0)
    sc_s[...] = _dot(kc_ref[0], qt_s[...])
    update(sc_s[...], vct_s[...], first=True)

    per_trip = GQA_CHUNKS_PER_TRIP if nchunk % GQA_CHUNKS_PER_TRIP == 0 else 2

    def body(ct, carry):
        c = per_trip * ct
        for u in range(per_trip):
            scores(jnp.minimum(c + u + 1, nchunk - 1), (u + 1) % 2)
            update(s_s[u % 2], vt_s[c + u])
        return carry

    lax.fori_loop(0, nchunk // per_trip, body, 0)
    o = (acc_s[...] / jnp.sum(den_s[...], axis=0, keepdims=True)).T
    for g in range(GQA_GROUP):
        o_ref[0, :, g * hd:(g + 1) * hd] = o[g * tq:(g + 1) * tq].astype(BF16)


GQA_FIXED_SHIFT_LIMIT = 40.0


def _gqa(pq, pkv, pckv, gain_q, gain_k):
    bound = 1.01 * HEAD_DIM * ATTN_SCALE * jnp.max(jnp.abs(gain_q)) * jnp.max(jnp.abs(gain_k))
    bound = bound.astype(F32).reshape(1)
    return lax.cond(bound[0] <= GQA_FIXED_SHIFT_LIMIT,
                    functools.partial(_gqa_call, fixed_shift=True),
                    functools.partial(_gqa_call, fixed_shift=False),
                    pq, pkv, pckv, bound)


def _gqa_call(pq, pkv, pckv, bound, *, fixed_shift):
    b, n, _ = pq.shape
    nc = pckv.shape[1]
    tq = min(GQA_Q_TILE, n)
    tk = min(512, n // 2)
    assert n % (2 * tk) == 0
    nq = GQA_GROUP * tq

    def qspec(g):
        return pl.BlockSpec((1, tq, HEAD_DIM), lambda bb, h, i: (bb, i, HB_GQA_Q + h * GQA_GROUP + g))

    return pl.pallas_call(
        functools.partial(_gqa_kernel, tk=tk, n=n, fixed_shift=fixed_shift),
        grid=(b, GQA_KV_HEADS, n // tq),
        in_specs=[
            pl.BlockSpec(memory_space=pltpu.SMEM),
            qspec(0), qspec(1), qspec(2),
            pl.BlockSpec((1, n, HEAD_DIM), lambda bb, h, i: (bb, 0, HB_GQA_K + h)),
            pl.BlockSpec((1, n, HEAD_DIM), lambda bb, h, i: (bb, 0, HB_GQA_V + h)),
            pl.BlockSpec((1, nc, HEAD_DIM), lambda bb, h, i: (bb, 0, HB_GQA_K + h)),
            pl.BlockSpec((1, nc, HEAD_DIM), lambda bb, h, i: (bb, 0, HB_GQA_V + h)),
        ],
        out_specs=pl.BlockSpec((1, tq, GQA_GROUP * HEAD_DIM), lambda bb, h, i: (bb, i, h)),
        out_shape=jax.ShapeDtypeStruct((b, n, GQA_Q_WIDTH), BF16),
        scratch_shapes=[
            pltpu.VMEM((HEAD_DIM, nq), BF16),
            pltpu.VMEM((n // tk, HEAD_DIM, tk), BF16),
            pltpu.VMEM((HEAD_DIM, nc), BF16),
            pltpu.VMEM((2, tk, nq), F32),
            pltpu.VMEM((nc, nq), F32),
            pltpu.VMEM((1, nq), F32),
            pltpu.VMEM((8, nq), F32),
            pltpu.VMEM((HEAD_DIM, nq), F32),
        ],
        compiler_params=_cparams(("arbitrary", "arbitrary", "arbitrary")),
        name="gqa_attn_fixed_shift" if fixed_shift else "gqa_attn_running_max",
    )(bound, pq, pq, pq, pkv, pkv, pckv, pckv)


def _ctx_attn_kernel(q_ref, k_ref, v_ref, o_ref):
    s = _dot_nt(q_ref[0], k_ref[0])
    m = jnp.max(s, axis=-1, keepdims=True)
    p = jnp.exp(s - m)
    den = jnp.sum(p, axis=-1, keepdims=True)
    o_ref[0] = (_dot(p.astype(BF16), v_ref[0]) / den).astype(BF16)


def _ctx_attn(pcq, pckv):
    b, nc, _ = pcq.shape
    nh = NA_HEADS + GQA_Q_HEADS

    def kmap(bb, h):
        g = jnp.maximum(h - NA_HEADS, 0) // GQA_GROUP
        return bb, 0, jnp.where(h < NA_HEADS, HB_NA_K + h, HB_GQA_K + g)

    def vmap_(bb, h):
        g = jnp.maximum(h - NA_HEADS, 0) // GQA_GROUP
        return bb, 0, jnp.where(h < NA_HEADS, HB_NA_V + h, HB_GQA_V + g)

    return pl.pallas_call(
        _ctx_attn_kernel,
        grid=(b, nh),
        in_specs=[
            pl.BlockSpec((1, nc, HEAD_DIM), lambda bb, h: (bb, 0, HB_NA_Q + h)),
            pl.BlockSpec((1, nc, HEAD_DIM), kmap),
            pl.BlockSpec((1, nc, HEAD_DIM), vmap_),
        ],
        out_specs=pl.BlockSpec((1, nc, HEAD_DIM), lambda bb, h: (bb, 0, h)),
        out_shape=jax.ShapeDtypeStruct((b, nc, nh * HEAD_DIM), BF16),
        compiler_params=_cparams(("arbitrary", "arbitrary")),
        name="ctx_attn",
    )(pcq, pckv, pckv)


def _merge_kernel(x_ref, ga_ref, gb_ref, gc_ref, yp_ref, yn_ref, yg_ref, wbr_ref, wout_ref, g1_ref, o_ref):
    r1 = POOL_WIDTH
    r2 = POOL_WIDTH + NA_WIDTH
    z = ga_ref[0].astype(F32) * _dot(yp_ref[0], wbr_ref[0, 0:r1, :])
    z = z + gb_ref[0].astype(F32) * _dot(yn_ref[0], wbr_ref[0, r1:r2, :])
    z = z + gc_ref[0].astype(F32) * _dot(yg_ref[0], wbr_ref[0, r2:, :])
    o_ref[0] = x_ref[0] + g1_ref[0] * _dot(z.astype(BF16), wout_ref[0])


def _merge(x, p, y_pool, y_na, na_cb, y_gqa, gqa_cb, w_br, w_out, layer, g1):
    b, n, d = x.shape
    tm = min(256, n)
    const = lambda bb, i: (layer, 0, 0)
    return pl.pallas_call(
        _merge_kernel,
        grid=(b, n // tm),
        in_specs=[
            pl.BlockSpec((1, tm, d), lambda bb, i: (bb, i, 0)),
            pl.BlockSpec((1, tm, d), lambda bb, i: (bb, i, 0)),
            pl.BlockSpec((1, tm, d), lambda bb, i: (bb, i, 1)),
            pl.BlockSpec((1, tm, d), lambda bb, i: (bb, i, 2)),
            pl.BlockSpec((1, tm, POOL_WIDTH), lambda bb, i: (bb, i, 0)),
            pl.BlockSpec((1, tm, NA_WIDTH), lambda bb, i: (bb, i, na_cb)),
            pl.BlockSpec((1, tm, GQA_Q_WIDTH), lambda bb, i: (bb, i, gqa_cb)),
            pl.BlockSpec((1,) + w_br.shape[1:], const, pipeline_mode=pl.Buffered(1)),
            pl.BlockSpec((1,) + w_out.shape[1:], const, pipeline_mode=pl.Buffered(1)),
            pl.BlockSpec((1, 1, d), lambda bb, i: (bb, 0, 0)),
        ],
        out_specs=pl.BlockSpec((1, tm, d), lambda bb, i: (bb, i, 0)),
        out_shape=jax.ShapeDtypeStruct((b, n, d), F32),
        compiler_params=_cparams(("arbitrary", "arbitrary")),
        name="branch_merge",
    )(x, p, p, p, y_pool, y_na, y_gqa, w_br, w_out, g1)


def _top2_of4(a, b, c, d):
    hi1, lo1 = jnp.maximum(a, b), jnp.minimum(a, b)
    hi2, lo2 = jnp.maximum(c, d), jnp.minimum(c, d)
    return jnp.maximum(hi1, hi2) + jnp.maximum(jnp.minimum(hi1, hi2), jnp.maximum(lo1, lo2))


def _router_kernel(x_ref, g_ref, sc_ref, sh_ref, whi_ref, wlo_ref, br_ref,
                   h_ref, e_ref, w_ref, rank_ref, cnt_ref, carry_ref):
    first = (pl.program_id(0) == 0) & (pl.program_id(1) == 0)

    @pl.when(first)
    def _():
        carry_ref[...] = jnp.zeros_like(carry_ref)

    x = x_ref[0]
    h = x * lax.rsqrt(jnp.mean(x * x, axis=-1, keepdims=True) + EPS) * g_ref[...]
    h = h * (1.0 + sc_ref[0]) + sh_ref[0]
    h_ref[0] = h
    h_hi = h.astype(BF16)
    h_lo = (h - h_hi.astype(F32)).astype(BF16)
    whi = whi_ref[...]
    logit = _dot_nt(whi, h_hi) + _dot_nt(whi, h_lo) + _dot_nt(wlo_ref[...], h_hi)
    s = jax.nn.sigmoid(logit)
    sel = s + br_ref[...]
    epg = EXPERTS_PER_GROUP
    row = lambda a, e: a[e:e + 1, :]
    gscore = [_top2_of4(*[row(sel, g * epg + j) for j in range(epg)]) for g in range(N_GROUPS)]
    g_best = jnp.zeros_like(gscore[0], dtype=jnp.int32)
    best = gscore[0]
    for g in range(1, N_GROUPS):
        upd = gscore[g] > best
        g_best = jnp.where(upd, g, g_best)
        best = jnp.where(upd, gscore[g], best)
    vs, ss = [], []
    for j in range(epg):
        v = row(sel, j)
        sv = row(s, j)
        for g in range(1, N_GROUPS):
            v = jnp.where(g_best == g, row(sel, g * epg + j), v)
            sv = jnp.where(g_best == g, row(s, g * epg + j), sv)
        vs.append(v)
        ss.append(sv)
    i1 = jnp.zeros_like(g_best)
    v1 = vs[0]
    for j in range(1, epg):
        upd = vs[j] > v1
        i1 = jnp.where(upd, j, i1)
        v1 = jnp.where(upd, vs[j], v1)
    i2 = jnp.full_like(g_best, -1)
    v2 = jnp.full_like(v1, -jnp.inf)
    for j in range(epg):
        upd = (i1 != j) & ((i2 < 0) | (vs[j] > v2))
        i2 = jnp.where(upd, j, i2)
        v2 = jnp.where(upd, vs[j], v2)
    w1 = sum(jnp.where(i1 == j, ss[j], 0.0) for j in range(epg))
    w2 = sum(jnp.where(i2 == j, ss[j], 0.0) for j in range(epg))
    tot = w1 + w2
    w_ref[0] = jnp.concatenate([w1 / tot, w2 / tot], axis=0)
    e1 = g_best * epg + i1
    e2 = g_best * epg + i2
    e_ref[0] = jnp.concatenate([e1, e2], axis=0)

    tm = x.shape[0]
    eidx = lax.broadcasted_iota(jnp.int32, (N_EXPERTS, tm), 0)
    oh1 = eidx == e1
    oh2 = eidx == e2
    oh = jnp.where(oh1 | oh2, 1.0, 0.0)
    before = lax.broadcasted_iota(jnp.int32, (tm, tm), 0) < lax.broadcasted_iota(jnp.int32, (tm, tm), 1)
    prefix = _dot(oh.astype(BF16), jnp.where(before, 1.0, 0.0).astype(BF16))
    base = carry_ref[...] + prefix
    r1 = jnp.sum(jnp.where(oh1, base, 0.0), axis=0, keepdims=True)
    r2 = jnp.sum(jnp.where(oh2, base, 0.0), axis=0, keepdims=True)
    rank_ref[0] = jnp.concatenate([r1, r2], axis=0).astype(jnp.int32)
    carry = carry_ref[...] + jnp.sum(oh, axis=1, keepdims=True)
    carry_ref[...] = carry
    cnt_ref[...] = jnp.broadcast_to(carry, cnt_ref.shape).astype(jnp.int32)


def _router(x, gain, sc, sh, wr_hi, wr_lo, b_router):
    b, n, d = x.shape
    tm = min(512, n)
    ne = wr_hi.shape[0]
    pair = pl.BlockSpec((1, 2, tm), lambda bb, i: (bb, 0, i))
    return pl.pallas_call(
        _router_kernel,
        grid=(b, n // tm),
        in_specs=[
            pl.BlockSpec((1, tm, d), lambda bb, i: (bb, i, 0)),
            pl.BlockSpec((1, d), lambda bb, i: (0, 0)),
            pl.BlockSpec((1, 1, d), lambda bb, i: (bb, 0, 0)),
            pl.BlockSpec((1, 1, d), lambda bb, i: (bb, 0, 0)),
            pl.BlockSpec((ne, d), lambda bb, i: (0, 0)),
            pl.BlockSpec((ne, d), lambda bb, i: (0, 0)),
            pl.BlockSpec((ne, 1), lambda bb, i: (0, 0)),
        ],
        out_specs=[
            pl.BlockSpec((1, tm, d), lambda bb, i: (bb, i, 0)),
            pair, pair, pair,
            pl.BlockSpec((ne, HEAD_DIM), lambda bb, i: (0, 0)),
        ],
        out_shape=[
            jax.ShapeDtypeStruct((b, n, d), F32),
            jax.ShapeDtypeStruct((b, 2, n), jnp.int32),
            jax.ShapeDtypeStruct((b, 2, n), F32),
            jax.ShapeDtypeStruct((b, 2, n), jnp.int32),
            jax.ShapeDtypeStruct((ne, HEAD_DIM), jnp.int32),
        ],
        scratch_shapes=[pltpu.VMEM((ne, 1), F32)],
        compiler_params=_cparams(("arbitrary", "arbitrary")),
        name="norm_router",
    )(x, gain.reshape(1, d), sc, sh, wr_hi, wr_lo, b_router.reshape(ne, 1))


MOE_TILE = 256
MOE_SCATTER_TILE = 512
MOE_COMBINE_TILE = 256
MOE_DMA_UNROLL = 8


def _moe_plan(e, rank, cnt):
    b, _, n = e.shape
    t = b * n
    counts = cnt[:, 0]
    ntile_e = (counts + MOE_TILE - 1) // MOE_TILE
    tile_end = jnp.cumsum(ntile_e)
    off = (tile_end - ntile_e) * MOE_TILE
    nt = 2 * t // MOE_TILE + N_EXPERTS
    tile_expert = jnp.sum(jnp.arange(nt)[:, None] >= tile_end[None, :], axis=1)
    tile_expert = jnp.minimum(tile_expert, N_EXPERTS - 1).astype(jnp.int32)
    ef = e.transpose(1, 0, 2).reshape(2, t)
    rf = rank.transpose(1, 0, 2).reshape(2, t)
    pos = rf + jnp.sum(jnp.where(ef[..., None] == jnp.arange(N_EXPERTS), off, 0), axis=-1)
    return pos.reshape(2 * t).astype(jnp.int32), tile_expert, tile_end[-1:].astype(jnp.int32), nt


def _scatter_kernel(pos_ref, h_ref, xs0_hbm, xs_hbm, sem, *, t):
    del xs0_hbm
    tm = h_ref.shape[0]
    base = pl.program_id(0) * tm

    def copies(j):
        return [pltpu.make_async_copy(h_ref.at[pl.ds(j, 1)], xs_hbm.at[pl.ds(pos_ref[k * t + base + j], 1)], sem)
                for k in range(2)]

    def start(j, c):
        for cp in copies(j):
            cp.start()
        return c

    def wait(j, c):
        for cp in copies(j):
            cp.wait()
        return c

    lax.fori_loop(0, tm, start, 0, unroll=MOE_DMA_UNROLL)
    lax.fori_loop(0, tm, wait, 0, unroll=MOE_DMA_UNROLL)


def _scatter_rows(pos, h, xs_buf):
    t, d = h.shape
    tm = min(MOE_SCATTER_TILE, t)
    any_spec = pl.BlockSpec(memory_space=pl.ANY)
    return pl.pallas_call(
        functools.partial(_scatter_kernel, t=t),
        grid_spec=pltpu.PrefetchScalarGridSpec(
            num_scalar_prefetch=1, grid=(t // tm,),
            in_specs=[pl.BlockSpec((tm, d), lambda i, p: (i, 0)), any_spec], out_specs=any_spec,
            scratch_shapes=[pltpu.SemaphoreType.DMA(())]),
        out_shape=jax.ShapeDtypeStruct(xs_buf.shape, F32),
        input_output_aliases={2: 0},
        compiler_params=pltpu.CompilerParams(dimension_semantics=("arbitrary",), has_side_effects=True),
        name="moe_scatter",
    )(pos, h, xs_buf)


def _experts_kernel(te_ref, nv_ref, xs_ref, wgu_ref, wd_ref, ys_ref):
    del te_ref

    @pl.when(pl.program_id(0) < nv_ref[0])
    def _():
        gu = _dot(xs_ref[...].astype(BF16), wgu_ref[0, 0])
        ff = gu.shape[1] // 2
        gate = gu[:, :ff]
        a = (gate * jax.nn.sigmoid(gate) * gu[:, ff:]).astype(BF16)
        ys_ref[...] = _dot(a, wd_ref[0, 0])


def _experts(xs, tile_expert, nvalid, w_gu, w_down, layer):
    nrows, d = xs.shape
    nt = nrows // MOE_TILE
    f2 = w_gu.shape[-1]
    row = lambda i, te, nv: (jnp.minimum(i, nv[0] - 1), 0)
    wmap = lambda i, te, nv: (layer, te[jnp.minimum(i, nv[0] - 1)], 0, 0)
    return pl.pallas_call(
        _experts_kernel,
        grid_spec=pltpu.PrefetchScalarGridSpec(
            num_scalar_prefetch=2, grid=(nt,),
            in_specs=[
                pl.BlockSpec((MOE_TILE, d), row),
                pl.BlockSpec((1, 1, d, f2), wmap),
                pl.BlockSpec((1, 1, f2 // 2, d), wmap),
            ],
            out_specs=pl.BlockSpec((MOE_TILE, d), row)),
        out_shape=jax.ShapeDtypeStruct((nrows, d), F32),
        compiler_params=_cparams(("arbitrary",)),
        name="moe_experts",
    )(tile_expert, nvalid, xs, w_gu, w_down)


def _combine_kernel(pos_ref, x_ref, w_ref, g2_ref, ys_hbm, o_ref, buf, sem, *, t):
    i = pl.program_id(0)
    nsteps = pl.num_programs(0)
    tm = x_ref.shape[0]

    def copies(tile, slot, j):
        tok = tile * tm + j
        return [pltpu.make_async_copy(ys_hbm.at[pl.ds(pos_ref[k * t + tok], 1)],
                                      buf.at[slot, k, pl.ds(j, 1)], sem.at[slot]) for k in range(2)]

    def issue(tile, slot):
        def body(j, c):
            for cp in copies(tile, slot, j):
                cp.start()
            return c
        lax.fori_loop(0, tm, body, 0, unroll=MOE_DMA_UNROLL)

    def wait(tile, slot):
        def body(j, c):
            for cp in copies(tile, slot, j):
                cp.wait()
            return c
        lax.fori_loop(0, tm, body, 0, unroll=MOE_DMA_UNROLL)

    @pl.when(i == 0)
    def _():
        issue(0, 0)

    @pl.when(i + 1 < nsteps)
    def _():
        issue(i + 1, (i + 1) % 2)

    slot = i % 2
    wait(i, slot)
    w = w_ref[...]
    y = w[:, 0:1] * buf[slot, 0] + w[:, 1:2] * buf[slot, 1]
    o_ref[...] = x_ref[...] + g2_ref[0] * y


def _combine(pos, x, w, g2, ys):
    b, n, d = x.shape
    t = b * n
    tm = min(MOE_COMBINE_TILE, n)
    per_b = n // tm
    out = pl.pallas_call(
        functools.partial(_combine_kernel, t=t),
        grid_spec=pltpu.PrefetchScalarGridSpec(
            num_scalar_prefetch=1, grid=(t // tm,),
            in_specs=[
                pl.BlockSpec((tm, d), lambda i, p: (i, 0)),
                pl.BlockSpec((tm, 2), lambda i, p: (i, 0)),
                pl.BlockSpec((1, 1, d), lambda i, p: (i // per_b, 0, 0)),
                pl.BlockSpec(memory_space=pl.ANY),
            ],
            out_specs=pl.BlockSpec((tm, d), lambda i, p: (i, 0)),
            scratch_shapes=[pltpu.VMEM((2, 2, tm, d), F32), pltpu.SemaphoreType.DMA((2,))]),
        out_shape=jax.ShapeDtypeStruct((t, d), F32),
        compiler_params=_cparams(("arbitrary",)),
        name="moe_combine",
    )(pos, x.reshape(t, d), w, g2, ys)
    return out.reshape(b, n, d)


def _moe(x, gain, sc, sh, g2, wr_hi, wr_lo, b_router, w_gu, w_down, layer, xs_buf):
    b, n, d = x.shape
    h, e, w, rank, cnt = _router(x, gain, sc, sh, wr_hi, wr_lo, b_router)
    pos, tile_expert, nvalid, nt = _moe_plan(e, rank, cnt)
    if xs_buf is None:
        xs_buf = jnp.zeros((nt * MOE_TILE, d), F32)
    xs = _scatter_rows(pos, h.reshape(b * n, d), xs_buf)
    ys = _experts(xs, tile_expert, nvalid, w_gu, w_down, layer)
    return _combine(pos, x, w.transpose(0, 2, 1).reshape(b * n, 2), g2, ys), xs


def _rope_tables(n):
    t = jnp.arange(n, dtype=jnp.int32)
    row = (t // GRID_W).astype(F32)
    col = (t % GRID_W).astype(F32)
    axis_dim = HEAD_DIM // 2
    inv = ROPE_THETA ** (-jnp.arange(0, axis_dim, 2, dtype=F32) / axis_dim)
    ang = jnp.concatenate([row[:, None] * inv, col[:, None] * inv], axis=-1)
    cos, sin = jnp.cos(ang), jnp.sin(ang)
    return jnp.concatenate([cos, cos], axis=-1), jnp.concatenate([-sin, sin], axis=-1)


def kernel(x, c, ctx, c_ctx, w_mod, b_mod, norm1, norm2, w_in, qk_gain, pool_w, pool_scale,
           na_rpb, w_br, w_out, w_router, b_router, w_gu, w_down):
    b, n, d = x.shape
    depth = w_mod.shape[0]
    rows = n // GRID_W
    kr = min(NA_ROWS, rows)
    assert n % GRID_W == 0 and rows % kr == 0 and b + 1 <= 8

    cos, sin = _rope_tables(n)
    mods = _modulation(jnp.concatenate([c, c_ctx[None, :]], axis=0), w_mod, b_mod)
    wr_t = w_router.T
    wr_hi = wr_t.astype(BF16)
    wr_lo = (wr_t - wr_hi.astype(F32)).astype(BF16)
    w_br_b, w_out_b = w_br.astype(BF16), w_out.astype(BF16)
    w_gu_b, w_down_b = w_gu.astype(BF16), w_down.astype(BF16)
    xs_x = xs_c = None

    for l in range(depth):
        last = l == depth - 1
        mx = mods[l, :b].reshape(b, 1, N_MOD, d)
        mc = jnp.broadcast_to(mods[l, b].reshape(1, 1, N_MOD, d), (b, 1, N_MOD, d))
        x_sh1, x_sc1, x_g1, x_sh2, x_sc2, x_g2 = [mx[:, :, k] for k in range(N_MOD)]
        c_sh1, c_sc1, c_g1, c_sh2, c_sc2, c_g2 = [mc[:, :, k] for k in range(N_MOD)]
        pool_w_l = pool_w[l].astype(BF16)

        def proj(t, sc, sh, kind, tables=(None, None)):
            return _inproj(t, norm1[l], sc, sh, w_in, l, qk_gain[l], *tables, kind=kind)

        pckv = proj(ctx, c_sc1, c_sh1, "kv")
        pq = proj(x, x_sc1, x_sh1, "q", (cos, sin))
        pkv = proj(x, x_sc1, x_sh1, "kv", (cos, sin))
        pg = proj(x, x_sc1, x_sh1, "gates")
        y_pool = _pool(pq, pool_w_l, pool_scale[l])
        y_na = _na(pq, pkv, pckv, na_rpb[l])
        y_gqa = _gqa(pq, pkv, pckv, qk_gain[l, 2], qk_gain[l, 3])
        x = _merge(x, pg, y_pool, y_na, 0, y_gqa, 0, w_br_b, w_out_b, l, x_g1)

        moe_w = (wr_hi, wr_lo, b_router, w_gu_b, w_down_b, l)
        if not last:
            pcq = proj(ctx, c_sc1, c_sh1, "q")
            pcg = proj(ctx, c_sc1, c_sh1, "gates")
            yc_pool = _pool(pcq, pool_w_l, pool_scale[l])
            yc = _ctx_attn(pcq, pckv)
            ctx = _merge(ctx, pcg, yc_pool, yc, 0, yc, 1, w_br_b, w_out_b, l, c_g1)
            ctx, xs_c = _moe(ctx, norm2[l], c_sc2, c_sh2, c_g2, *moe_w, xs_c)
        x, xs_x = _moe(x, norm2[l], x_sc2, x_sh2, x_g2, *moe_w, xs_x)
    return x
```

```python
import functools

import numpy as np
import jax
import jax.numpy as jnp
from jax import lax
from jax.experimental import pallas as pl
from jax.experimental.pallas import tpu as pltpu

F32 = jnp.float32
BF16 = jnp.bfloat16

GRID_W = 64
HEAD_DIM = 128
ROPE_THETA = 10000.0
EPS = 1e-6
POOL_WINDOWS = (2, 4, 8, 16)
POOL_CH = 128
POOL_WIDTH = len(POOL_WINDOWS) * POOL_CH
NA_HEADS = 6
NA_WIDTH = NA_HEADS * HEAD_DIM
NA_ROWS = 8
NA_COLS = 16
GQA_Q_HEADS = 6
GQA_KV_HEADS = 2
GQA_GROUP = GQA_Q_HEADS // GQA_KV_HEADS
GQA_Q_WIDTH = GQA_Q_HEADS * HEAD_DIM
N_BRANCH = 3
N_EXPERTS = 16
N_GROUPS = 4
EXPERTS_PER_GROUP = N_EXPERTS // N_GROUPS
N_MOD = 6
ATTN_SCALE = HEAD_DIM ** -0.5

COL_BLOCK = 2048
CB_POOL = 0
HB_NA_Q = POOL_WIDTH // HEAD_DIM
HB_GQA_Q = HB_NA_Q + NA_HEADS
HB_NA_K = 0
HB_NA_V = HB_NA_K + NA_HEADS
HB_GQA_K = HB_NA_V + NA_HEADS
HB_GQA_V = HB_GQA_K + GQA_KV_HEADS
PROJ_KINDS = {"gates": (0, 3), "q": (3, 1), "kv": (4, 1)}

V7X_VMEM_LIMIT = 56 * 1024 * 1024
NEG_BIG = -1e30


def _cparams(sem):
    return pltpu.CompilerParams(dimension_semantics=sem, vmem_limit_bytes=V7X_VMEM_LIMIT)


def _dot(a, b):
    return jnp.dot(a, b, preferred_element_type=F32)


def _dot_nt(a, b):
    return lax.dot_general(a, b, (((1,), (1,)), ((), ())), preferred_element_type=F32)


def _mod_kernel(ct_ref, w_ref, b_ref, o_ref, *, n_rows):
    ct = ct_ref[...]
    a = ct * jax.nn.sigmoid(ct)
    w = w_ref[0]
    rows = [jnp.sum(w * a[:, r:r + 1], axis=0, keepdims=True) for r in range(n_rows)]
    rows += [jnp.zeros_like(rows[0])] * (8 - n_rows)
    o_ref[0] = jnp.concatenate(rows, axis=0) + b_ref[0]


def _modulation(c_rows, w_mod, b_mod):
    depth, d, nm = w_mod.shape
    n_rows = c_rows.shape[0]
    ct = jnp.zeros((d, 8), F32).at[:, :n_rows].set(c_rows.T)
    tn = 1024
    return pl.pallas_call(
        functools.partial(_mod_kernel, n_rows=n_rows),
        grid=(depth, nm // tn),
        in_specs=[
            pl.BlockSpec((d, 8), lambda l, j: (0, 0)),
            pl.BlockSpec((1, d, tn), lambda l, j: (l, 0, j)),
            pl.BlockSpec((1, 1, tn), lambda l, j: (l, 0, j)),
        ],
        out_specs=pl.BlockSpec((1, 8, tn), lambda l, j: (l, 0, j)),
        out_shape=jax.ShapeDtypeStruct((depth, 8, nm), F32),
        compiler_params=_cparams(("arbitrary", "arbitrary")),
        name="adaln_mod",
    )(ct, w_mod, b_mod.reshape(depth, 1, nm))


def _head_norm(a, gain):
    return a * lax.rsqrt(jnp.mean(a * a, axis=-1, keepdims=True) + EPS) * gain


def _rope(y, cos, sin):
    return y * cos + pltpu.roll(y, HEAD_DIM // 2, 1) * sin


PROJ_CHUNK = 2 * HEAD_DIM
PROJ_ROW_SPLIT = 2


def _proj_epilogue(kind, head, a, qg_ref, rope_tables):
    if kind == "gates":
        return jax.nn.sigmoid(a)
    if kind == "q":
        if head < HB_NA_Q:
            return a
        if head < HB_GQA_Q:
            return _head_norm(a, qg_ref[0:1, :]) * ATTN_SCALE
        y = _head_norm(a, qg_ref[2:3, :])
        return (_rope(y, *rope_tables) if rope_tables else y) * ATTN_SCALE
    if head < HB_NA_V:
        return _head_norm(a, qg_ref[1:2, :])
    if HB_GQA_K <= head < HB_GQA_V:
        y = _head_norm(a, qg_ref[3:4, :])
        return _rope(y, *rope_tables) if rope_tables else y
    return a


def _inproj_kernel(*refs, kind, rope):
    if rope:
        x_ref, g_ref, sc_ref, sh_ref, w_ref, qg_ref, cos_ref, sin_ref, o_ref, wb_s = refs
    else:
        x_ref, g_ref, sc_ref, sh_ref, w_ref, qg_ref, o_ref, wb_s = refs

    @pl.when((pl.program_id(1) == 0) & (pl.program_id(2) == 0))
    def _():
        wb_s[...] = w_ref[0].astype(BF16)

    tm = x_ref.shape[1]
    nsub = PROJ_ROW_SPLIT if tm % (PROJ_ROW_SPLIT * 16) == 0 else 1
    ts = tm // nsub
    per = PROJ_CHUNK // HEAD_DIM
    for r in range(nsub):
        rows = slice(r * ts, (r + 1) * ts)
        x = x_ref[0, rows, :]
        h = x * lax.rsqrt(jnp.mean(x * x, axis=-1, keepdims=True) + EPS) * g_ref[...]
        hb = (h * (1.0 + sc_ref[0]) + sh_ref[0]).astype(BF16)
        tables = (cos_ref[rows, :], sin_ref[rows, :]) if rope else None
        for c in range(COL_BLOCK // PROJ_CHUNK):
            acc = _dot(hb, wb_s[:, c * PROJ_CHUNK:(c + 1) * PROJ_CHUNK])
            for u in range(per):
                head = c * per + u
                y = _proj_epilogue(kind, head, acc[:, u * HEAD_DIM:(u + 1) * HEAD_DIM], qg_ref, tables)
                o_ref[0, rows, head * HEAD_DIM:(head + 1) * HEAD_DIM] = y.astype(BF16)


def _inproj(x, gain, sc, sh, w_in, layer, qgain, cos, sin, *, kind):
    b, n, d = x.shape
    rope = cos is not None and kind != "gates"
    tm = min(512, n)
    j0, nj = PROJ_KINDS[kind]
    in_specs = [
        pl.BlockSpec((1, tm, d), lambda j, bb, i: (bb, i, 0)),
        pl.BlockSpec((1, d), lambda j, bb, i: (0, 0)),
        pl.BlockSpec((1, 1, d), lambda j, bb, i: (bb, 0, 0)),
        pl.BlockSpec((1, 1, d), lambda j, bb, i: (bb, 0, 0)),
        pl.BlockSpec((1, d, COL_BLOCK), lambda j, bb, i: (layer, 0, j + j0), pipeline_mode=pl.Buffered(1)),
        pl.BlockSpec((4, HEAD_DIM), lambda j, bb, i: (0, 0)),
    ]
    args = [x, gain.reshape(1, d), sc, sh, w_in, qgain]
    if rope:
        in_specs += [pl.BlockSpec((tm, HEAD_DIM), lambda j, bb, i: (i, 0))] * 2
        args += [cos, sin]
    return pl.pallas_call(
        functools.partial(_inproj_kernel, kind=kind, rope=rope),
        grid=(nj, b, n // tm),
        in_specs=in_specs,
        out_specs=pl.BlockSpec((1, tm, COL_BLOCK), lambda j, bb, i: (bb, i, j)),
        out_shape=jax.ShapeDtypeStruct((b, n, nj * COL_BLOCK), BF16),
        scratch_shapes=[pltpu.VMEM((d, COL_BLOCK), BF16)],
        compiler_params=_cparams(("arbitrary", "arbitrary", "arbitrary")),
        name="inproj_" + kind + ("_rope" if rope else ""),
    )(*args)


POOL_HALO = 16


def _pool_kernel(prev_ref, cur_ref, next_ref, w_ref, s_ref, o_ref, buf_ref, *, tm, n):
    i = pl.program_id(1)
    nt = pl.num_programs(1)
    hl = POOL_HALO
    buf_ref[pl.ds(hl, tm), :] = cur_ref[0].astype(F32)
    buf_ref[pl.ds(0, hl), :] = jnp.where(i > 0, prev_ref[0].astype(F32), 0.0)
    buf_ref[pl.ds(hl + tm, hl), :] = jnp.where(i < nt - 1, next_ref[0].astype(F32), 0.0)
    t = i * tm + lax.broadcasted_iota(jnp.int32, (tm, 1), 0)
    for g, w in enumerate(POOL_WINDOWS):
        sl = slice(g * POOL_CH, (g + 1) * POOL_CH)
        acc = buf_ref[pl.ds(hl - w // 2, tm), sl]
        for off in range(-w // 2 + 1, w // 2):
            acc = acc + buf_ref[pl.ds(hl + off, tm), sl]
        cnt = (jnp.minimum(t + w // 2, n) - jnp.maximum(t - w // 2, 0)).astype(F32)
        dlt = acc / cnt - buf_ref[pl.ds(hl, tm), sl]
        y = _dot(dlt.astype(BF16), w_ref[g]) * s_ref[:, sl]
        o_ref[0, :, sl] = y.astype(BF16)


def _pool(p, pool_w, pool_scale):
    b, n, _ = p.shape
    tm = min(512, n)
    hl = POOL_HALO
    hb = tm // hl
    last = n // hl - 1
    return pl.pallas_call(
        functools.partial(_pool_kernel, tm=tm, n=n),
        grid=(b, n // tm),
        in_specs=[
            pl.BlockSpec((1, hl, POOL_WIDTH), lambda bb, i: (bb, jnp.maximum(i * hb - 1, 0), CB_POOL)),
            pl.BlockSpec((1, tm, POOL_WIDTH), lambda bb, i: (bb, i, CB_POOL)),
            pl.BlockSpec((1, hl, POOL_WIDTH), lambda bb, i: (bb, jnp.minimum((i + 1) * hb, last), CB_POOL)),
            pl.BlockSpec((len(POOL_WINDOWS), POOL_CH, POOL_CH), lambda bb, i: (0, 0, 0)),
            pl.BlockSpec((1, POOL_WIDTH), lambda bb, i: (0, 0)),
        ],
        out_specs=pl.BlockSpec((1, tm, POOL_WIDTH), lambda bb, i: (bb, i, 0)),
        out_shape=jax.ShapeDtypeStruct((b, n, POOL_WIDTH), BF16),
        scratch_shapes=[pltpu.VMEM((tm + 2 * hl, POOL_WIDTH), F32)],
        compiler_params=_cparams(("arbitrary", "arbitrary")),
        name="pool_mixer",
    )(p, p, p, pool_w, pool_scale.reshape(1, POOL_WIDTH))


NA_QROWS = 4
NA_UROWS = 12
NA_BLOCKS_PER_STEP = 8


def _na_plan(rows, kr):
    rq, ku = NA_QROWS, NA_UROWS
    if rows < ku or rows % rq:
        rq, ku = 1, kr
    starts, keys = [], []
    for r0 in range(0, rows, rq):
        rs = [min(max(r - kr // 2, 0), rows - kr) for r in range(r0, r0 + rq)]
        us = min(rs[0], rows - ku)
        starts.append(us)
        keys.append(tuple((us - r, rs_q - us) for r, rs_q in zip(range(r0, r0 + rq), rs)))
    tables = sorted(set(keys))
    table_of = np.array([tables.index(k) for k in keys], np.int32)
    return rq, ku, np.array(starts, np.int32), table_of, tables


def _na_bias(rpb, kr, ku, tables):
    col = np.arange(GRID_W)
    col_start = np.clip(col - NA_COLS // 2, 0, GRID_W - NA_COLS)
    dcol = col[None, :] - col[:, None] + (NA_COLS - 1)
    ok = (col[None, :] >= col_start[:, None]) & (col[None, :] < col_start[:, None] + NA_COLS)
    onehot = ((dcol[:, :, None] == np.arange(2 * NA_COLS - 1)) & ok[:, :, None]).astype(np.float32)
    colbias = jnp.einsum("hrc,qkc->hrqk", rpb.astype(F32), onehot, precision=lax.Precision.HIGHEST)
    colbias = jnp.where(ok[None, None], colbias, NEG_BIG)
    h = rpb.shape[0]
    masked = jnp.full((h, GRID_W, GRID_W), NEG_BIG, F32)
    out = []
    for key in tables:
        per_q = []
        for rel0, first in key:
            blocks = [colbias[:, rel0 + i + NA_ROWS - 1] if first <= i < first + kr else masked
                      for i in range(ku)]
            per_q.append(jnp.stack(blocks, axis=2))
        out.append(jnp.stack(per_q, axis=1))
    rq = len(tables[0])
    return jnp.stack(out).reshape(len(tables), h, rq * GRID_W, ku * GRID_W)


def _na_kernel(us_ref, tb_ref, q_ref, k_ref, v_ref, kc_ref, vc_ref, bias_ref, o_ref, sl_s, sc_s,
               *, nb, mq, mk):
    i = pl.program_id(2)
    kc = kc_ref[0]
    vc = vc_ref[0]

    def koff(bb):
        return pl.multiple_of(us_ref[i * nb + bb] * GRID_W, GRID_W)

    def scores(bb, slot):
        q = q_ref[0, bb * mq:(bb + 1) * mq, :]
        sl_s[slot] = _dot_nt(q, k_ref[0, pl.ds(koff(bb), mk), :]) + bias_ref[tb_ref[i * nb + bb], 0]
        sc_s[slot] = _dot_nt(q, kc)

    def finish(bb, slot):
        s_loc = sl_s[slot]
        s_ctx = sc_s[slot]
        m = jnp.maximum(jnp.max(s_loc, axis=-1, keepdims=True), jnp.max(s_ctx, axis=-1, keepdims=True))
        p_loc = jnp.exp(s_loc - m)
        p_ctx = jnp.exp(s_ctx - m)
        den = jnp.sum(p_loc, axis=-1, keepdims=True) + jnp.sum(p_ctx, axis=-1, keepdims=True)
        o = _dot(p_loc.astype(BF16), v_ref[0, pl.ds(koff(bb), mk), :]) + _dot(p_ctx.astype(BF16), vc)
        o_ref[0, bb * mq:(bb + 1) * mq, :] = (o / den).astype(BF16)

    scores(0, 0)
    for bb in range(nb):
        if bb + 1 < nb:
            scores(bb + 1, (bb + 1) % 2)
        finish(bb, bb % 2)


def _na(pq, pkv, pckv, rpb):
    b, n, _ = pq.shape
    nc = pckv.shape[1]
    rows = n // GRID_W
    kr = min(NA_ROWS, rows)
    rq, ku, starts, table_of, tables = _na_plan(rows, kr)
    bias = _na_bias(rpb, kr, ku, tables)
    nblk = rows // rq
    nb = min(NA_BLOCKS_PER_STEP, nblk)
    assert nblk % nb == 0
    mq, mk = rq * GRID_W, ku * GRID_W
    return pl.pallas_call(
        functools.partial(_na_kernel, nb=nb, mq=mq, mk=mk),
        grid_spec=pltpu.PrefetchScalarGridSpec(
            num_scalar_prefetch=2, grid=(b, NA_HEADS, nblk // nb),
            in_specs=[
                pl.BlockSpec((1, nb * mq, HEAD_DIM), lambda bb, h, i, us, tb: (bb, i, HB_NA_Q + h)),
                pl.BlockSpec((1, n, HEAD_DIM), lambda bb, h, i, us, tb: (bb, 0, HB_NA_K + h)),
                pl.BlockSpec((1, n, HEAD_DIM), lambda bb, h, i, us, tb: (bb, 0, HB_NA_V + h)),
                pl.BlockSpec((1, nc, HEAD_DIM), lambda bb, h, i, us, tb: (bb, 0, HB_NA_K + h)),
                pl.BlockSpec((1, nc, HEAD_DIM), lambda bb, h, i, us, tb: (bb, 0, HB_NA_V + h)),
                pl.BlockSpec((len(tables), 1, mq, mk), lambda bb, h, i, us, tb: (0, h, 0, 0)),
            ],
            out_specs=pl.BlockSpec((1, nb * mq, HEAD_DIM), lambda bb, h, i, us, tb: (bb, i, h)),
            scratch_shapes=[pltpu.VMEM((2, mq, mk), F32), pltpu.VMEM((2, mq, nc), F32)]),
        out_shape=jax.ShapeDtypeStruct((b, n, NA_WIDTH), BF16),
        compiler_params=_cparams(("arbitrary", "arbitrary", "arbitrary")),
        name="na_attn",
    )(jnp.asarray(starts), jnp.asarray(table_of), pq, pkv, pkv, pckv, pckv, bias)


GQA_CHUNKS_PER_TRIP = 4
GQA_Q_TILE = 512


def _gqa_kernel(bound_ref, q0_ref, q1_ref, q2_ref, k_ref, v_ref, kc_ref, vc_ref, o_ref,
                qt_s, vt_s, vct_s, s_s, sc_s, m_s, den_s, acc_s, *, tk, n, fixed_shift):
    i = pl.program_id(2)
    tq = q0_ref.shape[1]
    nchunk = n // tk
    hd = HEAD_DIM

    def to_t(a):
        return a.astype(F32).T.astype(BF16)

    @pl.when(i == 0)
    def _():
        def tr(c, carry):
            off = pl.multiple_of(c * tk, tk)
            vt_s[c] = to_t(v_ref[0, pl.ds(off, tk), :])
            return carry
        lax.fori_loop(0, nchunk, tr, 0)
        vct_s[...] = to_t(vc_ref[0])

    for g, qr in enumerate((q0_ref, q1_ref, q2_ref)):
        qt_s[:, g * tq:(g + 1) * tq] = to_t(qr[0])

    def scores(c, slot):
        off = pl.multiple_of(c * tk, tk)
        s_s[slot] = _dot(k_ref[0, pl.ds(off, tk), :], qt_s[...])

    def col_sums(p):
        return jnp.sum(p.reshape(p.shape[0] // 8, 8, p.shape[1]), axis=0)

    def update(s, vt, first=False):
        if fixed_shift:
            p = jnp.exp(s - bound_ref[0])
            pv = _dot(vt, p.astype(BF16))
            den_s[...] = col_sums(p) if first else den_s[...] + col_sums(p)
            acc_s[...] = pv if first else acc_s[...] + pv
            return
        smax = jnp.max(s, axis=0, keepdims=True)
        if first:
            m_new = smax
        else:
            m_prev = m_s[...]
            m_new = jnp.maximum(m_prev, smax)
            alpha = jnp.exp(m_prev - m_new)
        p = jnp.exp(s - m_new)
        pv = _dot(vt, p.astype(BF16))
        den_s[...] = col_sums(p) if first else alpha * den_s[...] + col_sums(p)
        acc_s[...] = pv if first else alpha * acc_s[...] + pv
        m_s[...] = m_new

    scores(0, 0)
    sc_s[...] = _dot(kc_ref[0], qt_s[...])
    update(sc_s[...], vct_s[...], first=True)

    per_trip = GQA_CHUNKS_PER_TRIP if nchunk % GQA_CHUNKS_PER_TRIP == 0 else 2

    def body(ct, carry):
        c = per_trip * ct
        for u in range(per_trip):
            scores(jnp.minimum(c + u + 1, nchunk - 1), (u + 1) % 2)
            update(s_s[u % 2], vt_s[c + u])
        return carry

    lax.fori_loop(0, nchunk // per_trip, body, 0)
    o = (acc_s[...] / jnp.sum(den_s[...], axis=0, keepdims=True)).T
    for g in range(GQA_GROUP):
        o_ref[0, :, g * hd:(g + 1) * hd] = o[g * tq:(g + 1) * tq].astype(BF16)


GQA_FIXED_SHIFT_LIMIT = 40.0


def _gqa(pq, pkv, pckv, gain_q, gain_k):
    bound = 1.01 * HEAD_DIM * ATTN_SCALE * jnp.max(jnp.abs(gain_q)) * jnp.max(jnp.abs(gain_k))
    bound = bound.astype(F32).reshape(1)
    return lax.cond(bound[0] <= GQA_FIXED_SHIFT_LIMIT,
                    functools.partial(_gqa_call, fixed_shift=True),
                    functools.partial(_gqa_call, fixed_shift=False),
                    pq, pkv, pckv, bound)


def _gqa_call(pq, pkv, pckv, bound, *, fixed_shift):
    b, n, _ = pq.shape
    nc = pckv.shape[1]
    tq = min(GQA_Q_TILE, n)
    tk = min(512, n // 2)
    assert n % (2 * tk) == 0
    nq = GQA_GROUP * tq

    def qspec(g):
        return pl.BlockSpec((1, tq, HEAD_DIM), lambda bb, h, i: (bb, i, HB_GQA_Q + h * GQA_GROUP + g))

    return pl.pallas_call(
        functools.partial(_gqa_kernel, tk=tk, n=n, fixed_shift=fixed_shift),
        grid=(b, GQA_KV_HEADS, n // tq),
        in_specs=[
            pl.BlockSpec(memory_space=pltpu.SMEM),
            qspec(0), qspec(1), qspec(2),
            pl.BlockSpec((1, n, HEAD_DIM), lambda bb, h, i: (bb, 0, HB_GQA_K + h)),
            pl.BlockSpec((1, n, HEAD_DIM), lambda bb, h, i: (bb, 0, HB_GQA_V + h)),
            pl.BlockSpec((1, nc, HEAD_DIM), lambda bb, h, i: (bb, 0, HB_GQA_K + h)),
            pl.BlockSpec((1, nc, HEAD_DIM), lambda bb, h, i: (bb, 0, HB_GQA_V + h)),
        ],
        out_specs=pl.BlockSpec((1, tq, GQA_GROUP * HEAD_DIM), lambda bb, h, i: (bb, i, h)),
        out_shape=jax.ShapeDtypeStruct((b, n, GQA_Q_WIDTH), BF16),
        scratch_shapes=[
            pltpu.VMEM((HEAD_DIM, nq), BF16),
            pltpu.VMEM((n // tk, HEAD_DIM, tk), BF16),
            pltpu.VMEM((HEAD_DIM, nc), BF16),
            pltpu.VMEM((2, tk, nq), F32),
            pltpu.VMEM((nc, nq), F32),
            pltpu.VMEM((1, nq), F32),
            pltpu.VMEM((8, nq), F32),
            pltpu.VMEM((HEAD_DIM, nq), F32),
        ],
        compiler_params=_cparams(("arbitrary", "arbitrary", "arbitrary")),
        name="gqa_attn_fixed_shift" if fixed_shift else "gqa_attn_running_max",
    )(bound, pq, pq, pq, pkv, pkv, pckv, pckv)


def _ctx_attn_kernel(q_ref, k_ref, v_ref, o_ref):
    s = _dot_nt(q_ref[0], k_ref[0])
    m = jnp.max(s, axis=-1, keepdims=True)
    p = jnp.exp(s - m)
    den = jnp.sum(p, axis=-1, keepdims=True)
    o_ref[0] = (_dot(p.astype(BF16), v_ref[0]) / den).astype(BF16)


def _ctx_attn(pcq, pckv):
    b, nc, _ = pcq.shape
    nh = NA_HEADS + GQA_Q_HEADS

    def kmap(bb, h):
        g = jnp.maximum(h - NA_HEADS, 0) // GQA_GROUP
        return bb, 0, jnp.where(h < NA_HEADS, HB_NA_K + h, HB_GQA_K + g)

    def vmap_(bb, h):
        g = jnp.maximum(h - NA_HEADS, 0) // GQA_GROUP
        return bb, 0, jnp.where(h < NA_HEADS, HB_NA_V + h, HB_GQA_V + g)

    return pl.pallas_call(
        _ctx_attn_kernel,
        grid=(b, nh),
        in_specs=[
            pl.BlockSpec((1, nc, HEAD_DIM), lambda bb, h: (bb, 0, HB_NA_Q + h)),
            pl.BlockSpec((1, nc, HEAD_DIM), kmap),
            pl.BlockSpec((1, nc, HEAD_DIM), vmap_),
        ],
        out_specs=pl.BlockSpec((1, nc, HEAD_DIM), lambda bb, h: (bb, 0, h)),
        out_shape=jax.ShapeDtypeStruct((b, nc, nh * HEAD_DIM), BF16),
        compiler_params=_cparams(("arbitrary", "arbitrary")),
        name="ctx_attn",
    )(pcq, pckv, pckv)


def _merge_kernel(x_ref, ga_ref, gb_ref, gc_ref, yp_ref, yn_ref, yg_ref, wbr_ref, wout_ref, g1_ref, o_ref):
    r1 = POOL_WIDTH
    r2 = POOL_WIDTH + NA_WIDTH
    z = ga_ref[0].astype(F32) * _dot(yp_ref[0], wbr_ref[0, 0:r1, :])
    z = z + gb_ref[0].astype(F32) * _dot(yn_ref[0], wbr_ref[0, r1:r2, :])
    z = z + gc_ref[0].astype(F32) * _dot(yg_ref[0], wbr_ref[0, r2:, :])
    o_ref[0] = x_ref[0] + g1_ref[0] * _dot(z.astype(BF16), wout_ref[0])


def _merge(x, p, y_pool, y_na, na_cb, y_gqa, gqa_cb, w_br, w_out, layer, g1):
    b, n, d = x.shape
    tm = min(256, n)
    const = lambda bb, i: (layer, 0, 0)
    return pl.pallas_call(
        _merge_kernel,
        grid=(b, n // tm),
        in_specs=[
            pl.BlockSpec((1, tm, d), lambda bb, i: (bb, i, 0)),
            pl.BlockSpec((1, tm, d), lambda bb, i: (bb, i, 0)),
            pl.BlockSpec((1, tm, d), lambda bb, i: (bb, i, 1)),
            pl.BlockSpec((1, tm, d), lambda bb, i: (bb, i, 2)),
            pl.BlockSpec((1, tm, POOL_WIDTH), lambda bb, i: (bb, i, 0)),
            pl.BlockSpec((1, tm, NA_WIDTH), lambda bb, i: (bb, i, na_cb)),
            pl.BlockSpec((1, tm, GQA_Q_WIDTH), lambda bb, i: (bb, i, gqa_cb)),
            pl.BlockSpec((1,) + w_br.shape[1:], const, pipeline_mode=pl.Buffered(1)),
            pl.BlockSpec((1,) + w_out.shape[1:], const, pipeline_mode=pl.Buffered(1)),
            pl.BlockSpec((1, 1, d), lambda bb, i: (bb, 0, 0)),
        ],
        out_specs=pl.BlockSpec((1, tm, d), lambda bb, i: (bb, i, 0)),
        out_shape=jax.ShapeDtypeStruct((b, n, d), F32),
        compiler_params=_cparams(("arbitrary", "arbitrary")),
        name="branch_merge",
    )(x, p, p, p, y_pool, y_na, y_gqa, w_br, w_out, g1)


def _top2_of4(a, b, c, d):
    hi1, lo1 = jnp.maximum(a, b), jnp.minimum(a, b)
    hi2, lo2 = jnp.maximum(c, d), jnp.minimum(c, d)
    return jnp.maximum(hi1, hi2) + jnp.maximum(jnp.minimum(hi1, hi2), jnp.maximum(lo1, lo2))


def _router_kernel(x_ref, g_ref, sc_ref, sh_ref, whi_ref, wlo_ref, br_ref,
                   h_ref, e_ref, w_ref, rank_ref, cnt_ref, carry_ref):
    first = (pl.program_id(0) == 0) & (pl.program_id(1) == 0)

    @pl.when(first)
    def _():
        carry_ref[...] = jnp.zeros_like(carry_ref)

    x = x_ref[0]
    h = x * lax.rsqrt(jnp.mean(x * x, axis=-1, keepdims=True) + EPS) * g_ref[...]
    h = h * (1.0 + sc_ref[0]) + sh_ref[0]
    h_ref[0] = h
    h_hi = h.astype(BF16)
    h_lo = (h - h_hi.astype(F32)).astype(BF16)
    whi = whi_ref[...]
    logit = _dot_nt(whi, h_hi) + _dot_nt(whi, h_lo) + _dot_nt(wlo_ref[...], h_hi)
    s = jax.nn.sigmoid(logit)
    sel = s + br_ref[...]
    epg = EXPERTS_PER_GROUP
    row = lambda a, e: a[e:e + 1, :]
    gscore = [_top2_of4(*[row(sel, g * epg + j) for j in range(epg)]) for g in range(N_GROUPS)]
    g_best = jnp.zeros_like(gscore[0], dtype=jnp.int32)
    best = gscore[0]
    for g in range(1, N_GROUPS):
        upd = gscore[g] > best
        g_best = jnp.where(upd, g, g_best)
        best = jnp.where(upd, gscore[g], best)
    vs, ss = [], []
    for j in range(epg):
        v = row(sel, j)
        sv = row(s, j)
        for g in range(1, N_GROUPS):
            v = jnp.where(g_best == g, row(sel, g * epg + j), v)
            sv = jnp.where(g_best == g, row(s, g * epg + j), sv)
        vs.append(v)
        ss.append(sv)
    i1 = jnp.zeros_like(g_best)
    v1 = vs[0]
    for j in range(1, epg):
        upd = vs[j] > v1
        i1 = jnp.where(upd, j, i1)
        v1 = jnp.where(upd, vs[j], v1)
    i2 = jnp.full_like(g_best, -1)
    v2 = jnp.full_like(v1, -jnp.inf)
    for j in range(epg):
        upd = (i1 != j) & ((i2 < 0) | (vs[j] > v2))
        i2 = jnp.where(upd, j, i2)
        v2 = jnp.where(upd, vs[j], v2)
    w1 = sum(jnp.where(i1 == j, ss[j], 0.0) for j in range(epg))
    w2 = sum(jnp.where(i2 == j, ss[j], 0.0) for j in range(epg))
    tot = w1 + w2
    w_ref[0] = jnp.concatenate([w1 / tot, w2 / tot], axis=0)
    e1 = g_best * epg + i1
    e2 = g_best * epg + i2
    e_ref[0] = jnp.concatenate([e1, e2], axis=0)

    tm = x.shape[0]
    eidx = lax.broadcasted_iota(jnp.int32, (N_EXPERTS, tm), 0)
    oh1 = eidx == e1
    oh2 = eidx == e2
    oh = jnp.where(oh1 | oh2, 1.0, 0.0)
    before = lax.broadcasted_iota(jnp.int32, (tm, tm), 0) < lax.broadcasted_iota(jnp.int32, (tm, tm), 1)
    prefix = _dot(oh.astype(BF16), jnp.where(before, 1.0, 0.0).astype(BF16))
    base = carry_ref[...] + prefix
    r1 = jnp.sum(jnp.where(oh1, base, 0.0), axis=0, keepdims=True)
    r2 = jnp.sum(jnp.where(oh2, base, 0.0), axis=0, keepdims=True)
    rank_ref[0] = jnp.concatenate([r1, r2], axis=0).astype(jnp.int32)
    carry = carry_ref[...] + jnp.sum(oh, axis=1, keepdims=True)
    carry_ref[...] = carry
    cnt_ref[...] = jnp.broadcast_to(carry, cnt_ref.shape).astype(jnp.int32)


def _router(x, gain, sc, sh, wr_hi, wr_lo, b_router):
    b, n, d = x.shape
    tm = min(512, n)
    ne = wr_hi.shape[0]
    pair = pl.BlockSpec((1, 2, tm), lambda bb, i: (bb, 0, i))
    return pl.pallas_call(
        _router_kernel,
        grid=(b, n // tm),
        in_specs=[
            pl.BlockSpec((1, tm, d), lambda bb, i: (bb, i, 0)),
            pl.BlockSpec((1, d), lambda bb, i: (0, 0)),
            pl.BlockSpec((1, 1, d), lambda bb, i: (bb, 0, 0)),
            pl.BlockSpec((1, 1, d), lambda bb, i: (bb, 0, 0)),
            pl.BlockSpec((ne, d), lambda bb, i: (0, 0)),
            pl.BlockSpec((ne, d), lambda bb, i: (0, 0)),
            pl.BlockSpec((ne, 1), lambda bb, i: (0, 0)),
        ],
        out_specs=[
            pl.BlockSpec((1, tm, d), lambda bb, i: (bb, i, 0)),
            pair, pair, pair,
            pl.BlockSpec((ne, HEAD_DIM), lambda bb, i: (0, 0)),
        ],
        out_shape=[
            jax.ShapeDtypeStruct((b, n, d), F32),
            jax.ShapeDtypeStruct((b, 2, n), jnp.int32),
            jax.ShapeDtypeStruct((b, 2, n), F32),
            jax.ShapeDtypeStruct((b, 2, n), jnp.int32),
            jax.ShapeDtypeStruct((ne, HEAD_DIM), jnp.int32),
        ],
        scratch_shapes=[pltpu.VMEM((ne, 1), F32)],
        compiler_params=_cparams(("arbitrary", "arbitrary")),
        name="norm_router",
    )(x, gain.reshape(1, d), sc, sh, wr_hi, wr_lo, b_router.reshape(ne, 1))


MOE_TILE = 256
MOE_SCATTER_TILE = 512
MOE_COMBINE_TILE = 256
MOE_DMA_UNROLL = 8


def _moe_plan(e, rank, cnt):
    b, _, n = e.shape
    t = b * n
    counts = cnt[:, 0]
    ntile_e = (counts + MOE_TILE - 1) // MOE_TILE
    tile_end = jnp.cumsum(ntile_e)
    off = (tile_end - ntile_e) * MOE_TILE
    nt = 2 * t // MOE_TILE + N_EXPERTS
    tile_expert = jnp.sum(jnp.arange(nt)[:, None] >= tile_end[None, :], axis=1)
    tile_expert = jnp.minimum(tile_expert, N_EXPERTS - 1).astype(jnp.int32)
    ef = e.transpose(1, 0, 2).reshape(2, t)
    rf = rank.transpose(1, 0, 2).reshape(2, t)
    pos = rf + jnp.sum(jnp.where(ef[..., None] == jnp.arange(N_EXPERTS), off, 0), axis=-1)
    return pos.reshape(2 * t).astype(jnp.int32), tile_expert, tile_end[-1:].astype(jnp.int32), nt


def _scatter_kernel(pos_ref, h_ref, xs0_hbm, xs_hbm, sem, *, t):
    del xs0_hbm
    tm = h_ref.shape[0]
    base = pl.program_id(0) * tm

    def copies(j):
        return [pltpu.make_async_copy(h_ref.at[pl.ds(j, 1)], xs_hbm.at[pl.ds(pos_ref[k * t + base + j], 1)], sem)
                for k in range(2)]

    def start(j, c):
        for cp in copies(j):
            cp.start()
        return c

    def wait(j, c):
        for cp in copies(j):
            cp.wait()
        return c

    lax.fori_loop(0, tm, start, 0, unroll=MOE_DMA_UNROLL)
    lax.fori_loop(0, tm, wait, 0, unroll=MOE_DMA_UNROLL)


def _scatter_rows(pos, h, xs_buf):
    t, d = h.shape
    tm = min(MOE_SCATTER_TILE, t)
    any_spec = pl.BlockSpec(memory_space=pl.ANY)
    return pl.pallas_call(
        functools.partial(_scatter_kernel, t=t),
        grid_spec=pltpu.PrefetchScalarGridSpec(
            num_scalar_prefetch=1, grid=(t // tm,),
            in_specs=[pl.BlockSpec((tm, d), lambda i, p: (i, 0)), any_spec], out_specs=any_spec,
            scratch_shapes=[pltpu.SemaphoreType.DMA(())]),
        out_shape=jax.ShapeDtypeStruct(xs_buf.shape, F32),
        input_output_aliases={2: 0},
        compiler_params=pltpu.CompilerParams(dimension_semantics=("arbitrary",), has_side_effects=True),
        name="moe_scatter",
    )(pos, h, xs_buf)


def _experts_kernel(te_ref, nv_ref, xs_ref, wgu_ref, wd_ref, ys_ref):
    del te_ref

    @pl.when(pl.program_id(0) < nv_ref[0])
    def _():
        gu = _dot(xs_ref[...].astype(BF16), wgu_ref[0, 0])
        ff = gu.shape[1] // 2
        gate = gu[:, :ff]
        a = (gate * jax.nn.sigmoid(gate) * gu[:, ff:]).astype(BF16)
        ys_ref[...] = _dot(a, wd_ref[0, 0])


def _experts(xs, tile_expert, nvalid, w_gu, w_down, layer):
    nrows, d = xs.shape
    nt = nrows // MOE_TILE
    f2 = w_gu.shape[-1]
    row = lambda i, te, nv: (jnp.minimum(i, nv[0] - 1), 0)
    wmap = lambda i, te, nv: (layer, te[jnp.minimum(i, nv[0] - 1)], 0, 0)
    return pl.pallas_call(
        _experts_kernel,
        grid_spec=pltpu.PrefetchScalarGridSpec(
            num_scalar_prefetch=2, grid=(nt,),
            in_specs=[
                pl.BlockSpec((MOE_TILE, d), row),
                pl.BlockSpec((1, 1, d, f2), wmap),
                pl.BlockSpec((1, 1, f2 // 2, d), wmap),
            ],
            out_specs=pl.BlockSpec((MOE_TILE, d), row)),
        out_shape=jax.ShapeDtypeStruct((nrows, d), F32),
        compiler_params=_cparams(("arbitrary",)),
        name="moe_experts",
    )(tile_expert, nvalid, xs, w_gu, w_down)


def _combine_kernel(pos_ref, x_ref, w_ref, g2_ref, ys_hbm, o_ref, buf, sem, *, t):
    i = pl.program_id(0)
    nsteps = pl.num_programs(0)
    tm = x_ref.shape[0]

    def copies(tile, slot, j):
        tok = tile * tm + j
        return [pltpu.make_async_copy(ys_hbm.at[pl.ds(pos_ref[k * t + tok], 1)],
                                      buf.at[slot, k, pl.ds(j, 1)], sem.at[slot]) for k in range(2)]

    def issue(tile, slot):
        def body(j, c):
            for cp in copies(tile, slot, j):
                cp.start()
            return c
        lax.fori_loop(0, tm, body, 0, unroll=MOE_DMA_UNROLL)

    def wait(tile, slot):
        def body(j, c):
            for cp in copies(tile, slot, j):
                cp.wait()
            return c
        lax.fori_loop(0, tm, body, 0, unroll=MOE_DMA_UNROLL)

    @pl.when(i == 0)
    def _():
        issue(0, 0)

    @pl.when(i + 1 < nsteps)
    def _():
        issue(i + 1, (i + 1) % 2)

    slot = i % 2
    wait(i, slot)
    w = w_ref[...]
    y = w[:, 0:1] * buf[slot, 0] + w[:, 1:2] * buf[slot, 1]
    o_ref[...] = x_ref[...] + g2_ref[0] * y


def _combine(pos, x, w, g2, ys):
    b, n, d = x.shape
    t = b * n
    tm = min(MOE_COMBINE_TILE, n)
    per_b = n // tm
    out = pl.pallas_call(
        functools.partial(_combine_kernel, t=t),
        grid_spec=pltpu.PrefetchScalarGridSpec(
            num_scalar_prefetch=1, grid=(t // tm,),
            in_specs=[
                pl.BlockSpec((tm, d), lambda i, p: (i, 0)),
                pl.BlockSpec((tm, 2), lambda i, p: (i, 0)),
                pl.BlockSpec((1, 1, d), lambda i, p: (i // per_b, 0, 0)),
                pl.BlockSpec(memory_space=pl.ANY),
            ],
            out_specs=pl.BlockSpec((tm, d), lambda i, p: (i, 0)),
            scratch_shapes=[pltpu.VMEM((2, 2, tm, d), F32), pltpu.SemaphoreType.DMA((2,))]),
        out_shape=jax.ShapeDtypeStruct((t, d), F32),
        compiler_params=_cparams(("arbitrary",)),
        name="moe_combine",
    )(pos, x.reshape(t, d), w, g2, ys)
    return out.reshape(b, n, d)


def _moe(x, gain, sc, sh, g2, wr_hi, wr_lo, b_router, w_gu, w_down, layer, xs_buf):
    b, n, d = x.shape
    h, e, w, rank, cnt = _router(x, gain, sc, sh, wr_hi, wr_lo, b_router)
    pos, tile_expert, nvalid, nt = _moe_plan(e, rank, cnt)
    if xs_buf is None:
        xs_buf = jnp.zeros((nt * MOE_TILE, d), F32)
    xs = _scatter_rows(pos, h.reshape(b * n, d), xs_buf)
    ys = _experts(xs, tile_expert, nvalid, w_gu, w_down, layer)
    return _combine(pos, x, w.transpose(0, 2, 1).reshape(b * n, 2), g2, ys), xs


def _rope_tables(n):
    t = jnp.arange(n, dtype=jnp.int32)
    row = (t // GRID_W).astype(F32)
    col = (t % GRID_W).astype(F32)
    axis_dim = HEAD_DIM // 2
    inv = ROPE_THETA ** (-jnp.arange(0, axis_dim, 2, dtype=F32) / axis_dim)
    ang = jnp.concatenate([row[:, None] * inv, col[:, None] * inv], axis=-1)
    cos, sin = jnp.cos(ang), jnp.sin(ang)
    return jnp.concatenate([cos, cos], axis=-1), jnp.concatenate([-sin, sin], axis=-1)


def kernel(x, c, ctx, c_ctx, w_mod, b_mod, norm1, norm2, w_in, qk_gain, pool_w, pool_scale,
           na_rpb, w_br, w_out, w_router, b_router, w_gu, w_down):
    b, n, d = x.shape
    depth = w_mod.shape[0]
    rows = n // GRID_W
    kr = min(NA_ROWS, rows)
    assert n % GRID_W == 0 and rows % kr == 0 and b + 1 <= 8

    cos, sin = _rope_tables(n)
    mods = _modulation(jnp.concatenate([c, c_ctx[None, :]], axis=0), w_mod, b_mod)
    wr_t = w_router.T
    wr_hi = wr_t.astype(BF16)
    wr_lo = (wr_t - wr_hi.astype(F32)).astype(BF16)
    w_br_b, w_out_b = w_br.astype(BF16), w_out.astype(BF16)
    w_gu_b, w_down_b = w_gu.astype(BF16), w_down.astype(BF16)
    xs_x = xs_c = None

    for l in range(depth):
        last = l == depth - 1
        mx = mods[l, :b].reshape(b, 1, N_MOD, d)
        mc = jnp.broadcast_to(mods[l, b].reshape(1, 1, N_MOD, d), (b, 1, N_MOD, d))
        x_sh1, x_sc1, x_g1, x_sh2, x_sc2, x_g2 = [mx[:, :, k] for k in range(N_MOD)]
        c_sh1, c_sc1, c_g1, c_sh2, c_sc2, c_g2 = [mc[:, :, k] for k in range(N_MOD)]
        pool_w_l = pool_w[l].astype(BF16)

        def proj(t, sc, sh, kind, tables=(None, None)):
            return _inproj(t, norm1[l], sc, sh, w_in, l, qk_gain[l], *tables, kind=kind)

        pckv = proj(ctx, c_sc1, c_sh1, "kv")
        pq = proj(x, x_sc1, x_sh1, "q", (cos, sin))
        pkv = proj(x, x_sc1, x_sh1, "kv", (cos, sin))
        pg = proj(x, x_sc1, x_sh1, "gates")
        y_pool = _pool(pq, pool_w_l, pool_scale[l])
        y_na = _na(pq, pkv, pckv, na_rpb[l])
        y_gqa = _gqa(pq, pkv, pckv, qk_gain[l, 2], qk_gain[l, 3])
        x = _merge(x, pg, y_pool, y_na, 0, y_gqa, 0, w_br_b, w_out_b, l, x_g1)

        moe_w = (wr_hi, wr_lo, b_router, w_gu_b, w_down_b, l)
        if not last:
            pcq = proj(ctx, c_sc1, c_sh1, "q")
            pcg = proj(ctx, c_sc1, c_sh1, "gates")
            yc_pool = _pool(pcq, pool_w_l, pool_scale[l])
            yc = _ctx_attn(pcq, pckv)
            ctx = _merge(ctx, pcg, yc_pool, yc, 0, yc, 1, w_br_b, w_out_b, l, c_g1)
            ctx, xs_c = _moe(ctx, norm2[l], c_sc2, c_sh2, c_g2, *moe_w, xs_c)
        x, xs_x = _moe(x, norm2[l], x_sc2, x_sh2, x_g2, *moe_w, xs_x)
    return x
```

```python
import functools

import numpy as np
import jax
import jax.numpy as jnp
from jax import lax
from jax.experimental import pallas as pl
from jax.experimental.pallas import tpu as pltpu

F32 = jnp.float32
BF16 = jnp.bfloat16

GRID_W = 64
HEAD_DIM = 128
ROPE_THETA = 10000.0
EPS = 1e-6
POOL_WINDOWS = (2, 4, 8, 16)
POOL_CH = 128
POOL_WIDTH = len(POOL_WINDOWS) * POOL_CH
NA_HEADS = 6
NA_WIDTH = NA_HEADS * HEAD_DIM
NA_ROWS = 8
NA_COLS = 16
GQA_Q_HEADS = 6
GQA_KV_HEADS = 2
GQA_GROUP = GQA_Q_HEADS // GQA_KV_HEADS
GQA_Q_WIDTH = GQA_Q_HEADS * HEAD_DIM
N_BRANCH = 3
N_EXPERTS = 16
N_GROUPS = 4
EXPERTS_PER_GROUP = N_EXPERTS // N_GROUPS
N_MOD = 6
ATTN_SCALE = HEAD_DIM ** -0.5

COL_BLOCK = 2048
CB_POOL = 0
HB_NA_Q = POOL_WIDTH // HEAD_DIM
HB_GQA_Q = HB_NA_Q + NA_HEADS
HB_NA_K = 0
HB_NA_V = HB_NA_K + NA_HEADS
HB_GQA_K = HB_NA_V + NA_HEADS
HB_GQA_V = HB_GQA_K + GQA_KV_HEADS
PROJ_KINDS = {"gates": (0, 3), "q": (3, 1), "kv": (4, 1)}

V7X_VMEM_LIMIT = 56 * 1024 * 1024
NEG_BIG = -1e30


def _cparams(sem):
    return pltpu.CompilerParams(dimension_semantics=sem, vmem_limit_bytes=V7X_VMEM_LIMIT)


def _dot(a, b):
    return jnp.dot(a, b, preferred_element_type=F32)


def _dot_nt(a, b):
    return lax.dot_general(a, b, (((1,), (1,)), ((), ())), preferred_element_type=F32)


def _mod_kernel(ct_ref, w_ref, b_ref, o_ref, *, n_rows):
    ct = ct_ref[...]
    a = ct * jax.nn.sigmoid(ct)
    w = w_ref[0]
    rows = [jnp.sum(w * a[:, r:r + 1], axis=0, keepdims=True) for r in range(n_rows)]
    rows += [jnp.zeros_like(rows[0])] * (8 - n_rows)
    o_ref[0] = jnp.concatenate(rows, axis=0) + b_ref[0]


def _modulation(c_rows, w_mod, b_mod):
    depth, d, nm = w_mod.shape
    n_rows = c_rows.shape[0]
    ct = jnp.zeros((d, 8), F32).at[:, :n_rows].set(c_rows.T)
    tn = 1024
    return pl.pallas_call(
        functools.partial(_mod_kernel, n_rows=n_rows),
        grid=(depth, nm // tn),
        in_specs=[
            pl.BlockSpec((d, 8), lambda l, j: (0, 0)),
            pl.BlockSpec((1, d, tn), lambda l, j: (l, 0, j)),
            pl.BlockSpec((1, 1, tn), lambda l, j: (l, 0, j)),
        ],
        out_specs=pl.BlockSpec((1, 8, tn), lambda l, j: (l, 0, j)),
        out_shape=jax.ShapeDtypeStruct((depth, 8, nm), F32),
        compiler_params=_cparams(("arbitrary", "arbitrary")),
        name="adaln_mod",
    )(ct, w_mod, b_mod.reshape(depth, 1, nm))


def _head_norm(a, gain):
    return a * lax.rsqrt(jnp.mean(a * a, axis=-1, keepdims=True) + EPS) * gain


def _rope(y, cos, sin):
    return y * cos + pltpu.roll(y, HEAD_DIM // 2, 1) * sin


PROJ_CHUNK = 2 * HEAD_DIM
PROJ_ROW_SPLIT = 4
PROJ_ROW_TILE = 1024


def _proj_epilogue(kind, head, a, qg_ref, rope_tables):
    if kind == "gates":
        return jax.nn.sigmoid(a)
    if kind == "q":
        if head < HB_NA_Q:
            return a
        if head < HB_GQA_Q:
            return _head_norm(a, qg_ref[0:1, :]) * ATTN_SCALE
        y = _head_norm(a, qg_ref[2:3, :])
        return (_rope(y, *rope_tables) if rope_tables else y) * ATTN_SCALE
    if head < HB_NA_V:
        return _head_norm(a, qg_ref[1:2, :])
    if HB_GQA_K <= head < HB_GQA_V:
        y = _head_norm(a, qg_ref[3:4, :])
        return _rope(y, *rope_tables) if rope_tables else y
    return a


def _inproj_kernel(*refs, kind, rope):
    if rope:
        x_ref, g_ref, sc_ref, sh_ref, w_ref, qg_ref, cos_ref, sin_ref, o_ref, wb_s = refs
    else:
        x_ref, g_ref, sc_ref, sh_ref, w_ref, qg_ref, o_ref, wb_s = refs

    @pl.when((pl.program_id(1) == 0) & (pl.program_id(2) == 0))
    def _():
        wb_s[...] = w_ref[0].astype(BF16)

    tm = x_ref.shape[1]
    nsub = PROJ_ROW_SPLIT if tm % (PROJ_ROW_SPLIT * 16) == 0 else 1
    ts = tm // nsub
    per = PROJ_CHUNK // HEAD_DIM
    for r in range(nsub):
        rows = slice(r * ts, (r + 1) * ts)
        x = x_ref[0, rows, :]
        h = x * lax.rsqrt(jnp.mean(x * x, axis=-1, keepdims=True) + EPS) * g_ref[...]
        hb = (h * (1.0 + sc_ref[0]) + sh_ref[0]).astype(BF16)
        tables = (cos_ref[rows, :], sin_ref[rows, :]) if rope else None
        for c in range(COL_BLOCK // PROJ_CHUNK):
            acc = _dot(hb, wb_s[:, c * PROJ_CHUNK:(c + 1) * PROJ_CHUNK])
            for u in range(per):
                head = c * per + u
                y = _proj_epilogue(kind, head, acc[:, u * HEAD_DIM:(u + 1) * HEAD_DIM], qg_ref, tables)
                o_ref[0, rows, head * HEAD_DIM:(head + 1) * HEAD_DIM] = y.astype(BF16)


def _inproj(x, gain, sc, sh, w_in, layer, qgain, cos, sin, *, kind):
    b, n, d = x.shape
    rope = cos is not None and kind != "gates"
    tm = min(PROJ_ROW_TILE, n)
    j0, nj = PROJ_KINDS[kind]
    in_specs = [
        pl.BlockSpec((1, tm, d), lambda j, bb, i: (bb, i, 0)),
        pl.BlockSpec((1, d), lambda j, bb, i: (0, 0)),
        pl.BlockSpec((1, 1, d), lambda j, bb, i: (bb, 0, 0)),
        pl.BlockSpec((1, 1, d), lambda j, bb, i: (bb, 0, 0)),
        pl.BlockSpec((1, d, COL_BLOCK), lambda j, bb, i: (layer, 0, j + j0), pipeline_mode=pl.Buffered(1)),
        pl.BlockSpec((4, HEAD_DIM), lambda j, bb, i: (0, 0)),
    ]
    args = [x, gain.reshape(1, d), sc, sh, w_in, qgain]
    if rope:
        in_specs += [pl.BlockSpec((tm, HEAD_DIM), lambda j, bb, i: (i, 0))] * 2
        args += [cos, sin]
    return pl.pallas_call(
        functools.partial(_inproj_kernel, kind=kind, rope=rope),
        grid=(nj, b, n // tm),
        in_specs=in_specs,
        out_specs=pl.BlockSpec((1, tm, COL_BLOCK), lambda j, bb, i: (bb, i, j)),
        out_shape=jax.ShapeDtypeStruct((b, n, nj * COL_BLOCK), BF16),
        scratch_shapes=[pltpu.VMEM((d, COL_BLOCK), BF16)],
        compiler_params=_cparams(("arbitrary", "arbitrary", "arbitrary")),
        name="inproj_" + kind + ("_rope" if rope else ""),
    )(*args)


POOL_HALO = 16


def _pool_kernel(prev_ref, cur_ref, next_ref, w_ref, s_ref, o_ref, buf_ref, *, tm, n):
    i = pl.program_id(1)
    nt = pl.num_programs(1)
    hl = POOL_HALO
    buf_ref[pl.ds(hl, tm), :] = cur_ref[0].astype(F32)
    buf_ref[pl.ds(0, hl), :] = jnp.where(i > 0, prev_ref[0].astype(F32), 0.0)
    buf_ref[pl.ds(hl + tm, hl), :] = jnp.where(i < nt - 1, next_ref[0].astype(F32), 0.0)
    t = i * tm + lax.broadcasted_iota(jnp.int32, (tm, 1), 0)
    for g, w in enumerate(POOL_WINDOWS):
        sl = slice(g * POOL_CH, (g + 1) * POOL_CH)
        acc = buf_ref[pl.ds(hl - w // 2, tm), sl]
        for off in range(-w // 2 + 1, w // 2):
            acc = acc + buf_ref[pl.ds(hl + off, tm), sl]
        cnt = (jnp.minimum(t + w // 2, n) - jnp.maximum(t - w // 2, 0)).astype(F32)
        dlt = acc / cnt - buf_ref[pl.ds(hl, tm), sl]
        y = _dot(dlt.astype(BF16), w_ref[g]) * s_ref[:, sl]
        o_ref[0, :, sl] = y.astype(BF16)


def _pool(p, pool_w, pool_scale):
    b, n, _ = p.shape
    tm = min(512, n)
    hl = POOL_HALO
    hb = tm // hl
    last = n // hl - 1
    return pl.pallas_call(
        functools.partial(_pool_kernel, tm=tm, n=n),
        grid=(b, n // tm),
        in_specs=[
            pl.BlockSpec((1, hl, POOL_WIDTH), lambda bb, i: (bb, jnp.maximum(i * hb - 1, 0), CB_POOL)),
            pl.BlockSpec((1, tm, POOL_WIDTH), lambda bb, i: (bb, i, CB_POOL)),
            pl.BlockSpec((1, hl, POOL_WIDTH), lambda bb, i: (bb, jnp.minimum((i + 1) * hb, last), CB_POOL)),
            pl.BlockSpec((len(POOL_WINDOWS), POOL_CH, POOL_CH), lambda bb, i: (0, 0, 0)),
            pl.BlockSpec((1, POOL_WIDTH), lambda bb, i: (0, 0)),
        ],
        out_specs=pl.BlockSpec((1, tm, POOL_WIDTH), lambda bb, i: (bb, i, 0)),
        out_shape=jax.ShapeDtypeStruct((b, n, POOL_WIDTH), BF16),
        scratch_shapes=[pltpu.VMEM((tm + 2 * hl, POOL_WIDTH), F32)],
        compiler_params=_cparams(("arbitrary", "arbitrary")),
        name="pool_mixer",
    )(p, p, p, pool_w, pool_scale.reshape(1, POOL_WIDTH))


NA_QROWS = 4
NA_UROWS = 12
NA_BLOCKS_PER_STEP = 8


def _na_plan(rows, kr):
    rq, ku = NA_QROWS, NA_UROWS
    if rows < ku or rows % rq:
        rq, ku = 1, kr
    starts, keys = [], []
    for r0 in range(0, rows, rq):
        rs = [min(max(r - kr // 2, 0), rows - kr) for r in range(r0, r0 + rq)]
        us = min(rs[0], rows - ku)
        starts.append(us)
        keys.append(tuple((us - r, rs_q - us) for r, rs_q in zip(range(r0, r0 + rq), rs)))
    tables = sorted(set(keys))
    table_of = np.array([tables.index(k) for k in keys], np.int32)
    return rq, ku, np.array(starts, np.int32), table_of, tables


def _na_bias(rpb, kr, ku, tables):
    col = np.arange(GRID_W)
    col_start = np.clip(col - NA_COLS // 2, 0, GRID_W - NA_COLS)
    dcol = col[None, :] - col[:, None] + (NA_COLS - 1)
    ok = (col[None, :] >= col_start[:, None]) & (col[None, :] < col_start[:, None] + NA_COLS)
    onehot = ((dcol[:, :, None] == np.arange(2 * NA_COLS - 1)) & ok[:, :, None]).astype(np.float32)
    colbias = jnp.einsum("hrc,qkc->hrqk", rpb.astype(F32), onehot, precision=lax.Precision.HIGHEST)
    colbias = jnp.where(ok[None, None], colbias, NEG_BIG)
    h = rpb.shape[0]
    masked = jnp.full((h, GRID_W, GRID_W), NEG_BIG, F32)
    out = []
    for key in tables:
        per_q = []
        for rel0, first in key:
            blocks = [colbias[:, rel0 + i + NA_ROWS - 1] if first <= i < first + kr else masked
                      for i in range(ku)]
            per_q.append(jnp.stack(blocks, axis=2))
        out.append(jnp.stack(per_q, axis=1))
    rq = len(tables[0])
    return jnp.stack(out).reshape(len(tables), h, rq * GRID_W, ku * GRID_W)


def _na_kernel(us_ref, tb_ref, q_ref, k_ref, v_ref, kc_ref, vc_ref, bias_ref, o_ref, sl_s, sc_s,
               *, nb, mq, mk):
    i = pl.program_id(2)
    kc = kc_ref[0]
    vc = vc_ref[0]

    def koff(bb):
        return pl.multiple_of(us_ref[i * nb + bb] * GRID_W, GRID_W)

    def scores(bb, slot):
        q = q_ref[0, bb * mq:(bb + 1) * mq, :]
        sl_s[slot] = _dot_nt(q, k_ref[0, pl.ds(koff(bb), mk), :]) + bias_ref[tb_ref[i * nb + bb], 0]
        sc_s[slot] = _dot_nt(q, kc)

    def finish(bb, slot):
        s_loc = sl_s[slot]
        s_ctx = sc_s[slot]
        m = jnp.maximum(jnp.max(s_loc, axis=-1, keepdims=True), jnp.max(s_ctx, axis=-1, keepdims=True))
        p_loc = jnp.exp(s_loc - m)
        p_ctx = jnp.exp(s_ctx - m)
        den = jnp.sum(p_loc, axis=-1, keepdims=True) + jnp.sum(p_ctx, axis=-1, keepdims=True)
        o = _dot(p_loc.astype(BF16), v_ref[0, pl.ds(koff(bb), mk), :]) + _dot(p_ctx.astype(BF16), vc)
        o_ref[0, bb * mq:(bb + 1) * mq, :] = (o / den).astype(BF16)

    scores(0, 0)
    for bb in range(nb):
        if bb + 1 < nb:
            scores(bb + 1, (bb + 1) % 2)
        finish(bb, bb % 2)


def _na(pq, pkv, pckv, rpb):
    b, n, _ = pq.shape
    nc = pckv.shape[1]
    rows = n // GRID_W
    kr = min(NA_ROWS, rows)
    rq, ku, starts, table_of, tables = _na_plan(rows, kr)
    bias = _na_bias(rpb, kr, ku, tables)
    nblk = rows // rq
    nb = min(NA_BLOCKS_PER_STEP, nblk)
    assert nblk % nb == 0
    mq, mk = rq * GRID_W, ku * GRID_W
    return pl.pallas_call(
        functools.partial(_na_kernel, nb=nb, mq=mq, mk=mk),
        grid_spec=pltpu.PrefetchScalarGridSpec(
            num_scalar_prefetch=2, grid=(b, NA_HEADS, nblk // nb),
            in_specs=[
                pl.BlockSpec((1, nb * mq, HEAD_DIM), lambda bb, h, i, us, tb: (bb, i, HB_NA_Q + h)),
                pl.BlockSpec((1, n, HEAD_DIM), lambda bb, h, i, us, tb: (bb, 0, HB_NA_K + h)),
                pl.BlockSpec((1, n, HEAD_DIM), lambda bb, h, i, us, tb: (bb, 0, HB_NA_V + h)),
                pl.BlockSpec((1, nc, HEAD_DIM), lambda bb, h, i, us, tb: (bb, 0, HB_NA_K + h)),
                pl.BlockSpec((1, nc, HEAD_DIM), lambda bb, h, i, us, tb: (bb, 0, HB_NA_V + h)),
                pl.BlockSpec((len(tables), 1, mq, mk), lambda bb, h, i, us, tb: (0, h, 0, 0)),
            ],
            out_specs=pl.BlockSpec((1, nb * mq, HEAD_DIM), lambda bb, h, i, us, tb: (bb, i, h)),
            scratch_shapes=[pltpu.VMEM((2, mq, mk), F32), pltpu.VMEM((2, mq, nc), F32)]),
        out_shape=jax.ShapeDtypeStruct((b, n, NA_WIDTH), BF16),
        compiler_params=_cparams(("arbitrary", "arbitrary", "arbitrary")),
        name="na_attn",
    )(jnp.asarray(starts), jnp.asarray(table_of), pq, pkv, pkv, pckv, pckv, bias)


GQA_CHUNKS_PER_TRIP = 4
GQA_Q_TILE = 512


def _gqa_kernel(bound_ref, q0_ref, q1_ref, q2_ref, k_ref, v_ref, kc_ref, vc_ref, o_ref,
                qt_s, vt_s, vct_s, s_s, sc_s, m_s, den_s, acc_s, *, tk, n, fixed_shift):
    i = pl.program_id(2)
    tq = q0_ref.shape[1]
    nchunk = n // tk
    hd = HEAD_DIM

    def to_t(a):
        return a.astype(F32).T.astype(BF16)

    @pl.when(i == 0)
    def _():
        def tr(c, carry):
            off = pl.multiple_of(c * tk, tk)
            vt_s[c] = to_t(v_ref[0, pl.ds(off, tk), :])
            return carry
        lax.fori_loop(0, nchunk, tr, 0)
        vct_s[...] = to_t(vc_ref[0])

    for g, qr in enumerate((q0_ref, q1_ref, q2_ref)):
        qt_s[:, g * tq:(g + 1) * tq] = to_t(qr[0])

    def scores(c, slot):
        off = pl.multiple_of(c * tk, tk)
        s_s[slot] = _dot(k_ref[0, pl.ds(off, tk), :], qt_s[...])

    def col_sums(p):
        return jnp.sum(p.reshape(p.shape[0] // 8, 8, p.shape[1]), axis=0)

    def update(s, vt, first=False):
        if fixed_shift:
            p = jnp.exp(s - bound_ref[0])
            pv = _dot(vt, p.astype(BF16))
            den_s[...] = col_sums(p) if first else den_s[...] + col_sums(p)
            acc_s[...] = pv if first else acc_s[...] + pv
            return
        smax = jnp.max(s, axis=0, keepdims=True)
        if first:
            m_new = smax
        else:
            m_prev = m_s[...]
            m_new = jnp.maximum(m_prev, smax)
            alpha = jnp.exp(m_prev - m_new)
        p = jnp.exp(s - m_new)
        pv = _dot(vt, p.astype(BF16))
        den_s[...] = col_sums(p) if first else alpha * den_s[...] + col_sums(p)
        acc_s[...] = pv if first else alpha * acc_s[...] + pv
        m_s[...] = m_new

    scores(0, 0)
    sc_s[...] = _dot(kc_ref[0], qt_s[...])
    update(sc_s[...], vct_s[...], first=True)

    per_trip = GQA_CHUNKS_PER_TRIP if nchunk % GQA_CHUNKS_PER_TRIP == 0 else 2

    def body(ct, carry):
        c = per_trip * ct
        for u in range(per_trip):
            scores(jnp.minimum(c + u + 1, nchunk - 1), (u + 1) % 2)
            update(s_s[u % 2], vt_s[c + u])
        return carry

    lax.fori_loop(0, nchunk // per_trip, body, 0)
    o = (acc_s[...] / jnp.sum(den_s[...], axis=0, keepdims=True)).T
    for g in range(GQA_GROUP):
        o_ref[0, :, g * hd:(g + 1) * hd] = o[g * tq:(g + 1) * tq].astype(BF16)


GQA_FIXED_SHIFT_LIMIT = 40.0


def _gqa(pq, pkv, pckv, gain_q, gain_k):
    bound = 1.01 * HEAD_DIM * ATTN_SCALE * jnp.max(jnp.abs(gain_q)) * jnp.max(jnp.abs(gain_k))
    bound = bound.astype(F32).reshape(1)
    return lax.cond(bound[0] <= GQA_FIXED_SHIFT_LIMIT,
                    functools.partial(_gqa_call, fixed_shift=True),
                    functools.partial(_gqa_call, fixed_shift=False),
                    pq, pkv, pckv, bound)


def _gqa_call(pq, pkv, pckv, bound, *, fixed_shift):
    b, n, _ = pq.shape
    nc = pckv.shape[1]
    tq = min(GQA_Q_TILE, n)
    tk = min(512, n // 2)
    assert n % (2 * tk) == 0
    nq = GQA_GROUP * tq

    def qspec(g):
        return pl.BlockSpec((1, tq, HEAD_DIM), lambda bb, h, i: (bb, i, HB_GQA_Q + h * GQA_GROUP + g))

    return pl.pallas_call(
        functools.partial(_gqa_kernel, tk=tk, n=n, fixed_shift=fixed_shift),
        grid=(b, GQA_KV_HEADS, n // tq),
        in_specs=[
            pl.BlockSpec(memory_space=pltpu.SMEM),
            qspec(0), qspec(1), qspec(2),
            pl.BlockSpec((1, n, HEAD_DIM), lambda bb, h, i: (bb, 0, HB_GQA_K + h)),
            pl.BlockSpec((1, n, HEAD_DIM), lambda bb, h, i: (bb, 0, HB_GQA_V + h)),
            pl.BlockSpec((1, nc, HEAD_DIM), lambda bb, h, i: (bb, 0, HB_GQA_K + h)),
            pl.BlockSpec((1, nc, HEAD_DIM), lambda bb, h, i: (bb, 0, HB_GQA_V + h)),
        ],
        out_specs=pl.BlockSpec((1, tq, GQA_GROUP * HEAD_DIM), lambda bb, h, i: (bb, i, h)),
        out_shape=jax.ShapeDtypeStruct((b, n, GQA_Q_WIDTH), BF16),
        scratch_shapes=[
            pltpu.VMEM((HEAD_DIM, nq), BF16),
            pltpu.VMEM((n // tk, HEAD_DIM, tk), BF16),
            pltpu.VMEM((HEAD_DIM, nc), BF16),
            pltpu.VMEM((2, tk, nq), F32),
            pltpu.VMEM((nc, nq), F32),
            pltpu.VMEM((1, nq), F32),
            pltpu.VMEM((8, nq), F32),
            pltpu.VMEM((HEAD_DIM, nq), F32),
        ],
        compiler_params=_cparams(("arbitrary", "arbitrary", "arbitrary")),
        name="gqa_attn_fixed_shift" if fixed_shift else "gqa_attn_running_max",
    )(bound, pq, pq, pq, pkv, pkv, pckv, pckv)


def _ctx_attn_kernel(q_ref, k_ref, v_ref, o_ref):
    s = _dot_nt(q_ref[0], k_ref[0])
    m = jnp.max(s, axis=-1, keepdims=True)
    p = jnp.exp(s - m)
    den = jnp.sum(p, axis=-1, keepdims=True)
    o_ref[0] = (_dot(p.astype(BF16), v_ref[0]) / den).astype(BF16)


def _ctx_attn(pcq, pckv):
    b, nc, _ = pcq.shape
    nh = NA_HEADS + GQA_Q_HEADS

    def kmap(bb, h):
        g = jnp.maximum(h - NA_HEADS, 0) // GQA_GROUP
        return bb, 0, jnp.where(h < NA_HEADS, HB_NA_K + h, HB_GQA_K + g)

    def vmap_(bb, h):
        g = jnp.maximum(h - NA_HEADS, 0) // GQA_GROUP
        return bb, 0, jnp.where(h < NA_HEADS, HB_NA_V + h, HB_GQA_V + g)

    return pl.pallas_call(
        _ctx_attn_kernel,
        grid=(b, nh),
        in_specs=[
            pl.BlockSpec((1, nc, HEAD_DIM), lambda bb, h: (bb, 0, HB_NA_Q + h)),
            pl.BlockSpec((1, nc, HEAD_DIM), kmap),
            pl.BlockSpec((1, nc, HEAD_DIM), vmap_),
        ],
        out_specs=pl.BlockSpec((1, nc, HEAD_DIM), lambda bb, h: (bb, 0, h)),
        out_shape=jax.ShapeDtypeStruct((b, nc, nh * HEAD_DIM), BF16),
        compiler_params=_cparams(("arbitrary", "arbitrary")),
        name="ctx_attn",
    )(pcq, pckv, pckv)


def _merge_kernel(x_ref, ga_ref, gb_ref, gc_ref, yp_ref, yn_ref, yg_ref, wbr_ref, wout_ref, g1_ref, o_ref):
    r1 = POOL_WIDTH
    r2 = POOL_WIDTH + NA_WIDTH
    z = ga_ref[0].astype(F32) * _dot(yp_ref[0], wbr_ref[0, 0:r1, :])
    z = z + gb_ref[0].astype(F32) * _dot(yn_ref[0], wbr_ref[0, r1:r2, :])
    z = z + gc_ref[0].astype(F32) * _dot(yg_ref[0], wbr_ref[0, r2:, :])
    o_ref[0] = x_ref[0] + g1_ref[0] * _dot(z.astype(BF16), wout_ref[0])


def _merge(x, p, y_pool, y_na, na_cb, y_gqa, gqa_cb, w_br, w_out, layer, g1):
    b, n, d = x.shape
    tm = min(256, n)
    const = lambda bb, i: (layer, 0, 0)
    return pl.pallas_call(
        _merge_kernel,
        grid=(b, n // tm),
        in_specs=[
            pl.BlockSpec((1, tm, d), lambda bb, i: (bb, i, 0)),
            pl.BlockSpec((1, tm, d), lambda bb, i: (bb, i, 0)),
            pl.BlockSpec((1, tm, d), lambda bb, i: (bb, i, 1)),
            pl.BlockSpec((1, tm, d), lambda bb, i: (bb, i, 2)),
            pl.BlockSpec((1, tm, POOL_WIDTH), lambda bb, i: (bb, i, 0)),
            pl.BlockSpec((1, tm, NA_WIDTH), lambda bb, i: (bb, i, na_cb)),
            pl.BlockSpec((1, tm, GQA_Q_WIDTH), lambda bb, i: (bb, i, gqa_cb)),
            pl.BlockSpec((1,) + w_br.shape[1:], const, pipeline_mode=pl.Buffered(1)),
            pl.BlockSpec((1,) + w_out.shape[1:], const, pipeline_mode=pl.Buffered(1)),
            pl.BlockSpec((1, 1, d), lambda bb, i: (bb, 0, 0)),
        ],
        out_specs=pl.BlockSpec((1, tm, d), lambda bb, i: (bb, i, 0)),
        out_shape=jax.ShapeDtypeStruct((b, n, d), F32),
        compiler_params=_cparams(("arbitrary", "arbitrary")),
        name="branch_merge",
    )(x, p, p, p, y_pool, y_na, y_gqa, w_br, w_out, g1)


def _top2_of4(a, b, c, d):
    hi1, lo1 = jnp.maximum(a, b), jnp.minimum(a, b)
    hi2, lo2 = jnp.maximum(c, d), jnp.minimum(c, d)
    return jnp.maximum(hi1, hi2) + jnp.maximum(jnp.minimum(hi1, hi2), jnp.maximum(lo1, lo2))


def _router_kernel(x_ref, g_ref, sc_ref, sh_ref, whi_ref, wlo_ref, br_ref,
                   h_ref, e_ref, w_ref, rank_ref, cnt_ref, carry_ref):
    first = (pl.program_id(0) == 0) & (pl.program_id(1) == 0)

    @pl.when(first)
    def _():
        carry_ref[...] = jnp.zeros_like(carry_ref)

    x = x_ref[0]
    h = x * lax.rsqrt(jnp.mean(x * x, axis=-1, keepdims=True) + EPS) * g_ref[...]
    h = h * (1.0 + sc_ref[0]) + sh_ref[0]
    h_ref[0] = h
    h_hi = h.astype(BF16)
    h_lo = (h - h_hi.astype(F32)).astype(BF16)
    whi = whi_ref[...]
    logit = _dot_nt(whi, h_hi) + _dot_nt(whi, h_lo) + _dot_nt(wlo_ref[...], h_hi)
    s = jax.nn.sigmoid(logit)
    sel = s + br_ref[...]
    epg = EXPERTS_PER_GROUP
    row = lambda a, e: a[e:e + 1, :]
    gscore = [_top2_of4(*[row(sel, g * epg + j) for j in range(epg)]) for g in range(N_GROUPS)]
    g_best = jnp.zeros_like(gscore[0], dtype=jnp.int32)
    best = gscore[0]
    for g in range(1, N_GROUPS):
        upd = gscore[g] > best
        g_best = jnp.where(upd, g, g_best)
        best = jnp.where(upd, gscore[g], best)
    vs, ss = [], []
    for j in range(epg):
        v = row(sel, j)
        sv = row(s, j)
        for g in range(1, N_GROUPS):
            v = jnp.where(g_best == g, row(sel, g * epg + j), v)
            sv = jnp.where(g_best == g, row(s, g * epg + j), sv)
        vs.append(v)
        ss.append(sv)
    i1 = jnp.zeros_like(g_best)
    v1 = vs[0]
    for j in range(1, epg):
        upd = vs[j] > v1
        i1 = jnp.where(upd, j, i1)
        v1 = jnp.where(upd, vs[j], v1)
    i2 = jnp.full_like(g_best, -1)
    v2 = jnp.full_like(v1, -jnp.inf)
    for j in range(epg):
        upd = (i1 != j) & ((i2 < 0) | (vs[j] > v2))
        i2 = jnp.where(upd, j, i2)
        v2 = jnp.where(upd, vs[j], v2)
    w1 = sum(jnp.where(i1 == j, ss[j], 0.0) for j in range(epg))
    w2 = sum(jnp.where(i2 == j, ss[j], 0.0) for j in range(epg))
    tot = w1 + w2
    w_ref[0] = jnp.concatenate([w1 / tot, w2 / tot], axis=0)
    e1 = g_best * epg + i1
    e2 = g_best * epg + i2
    e_ref[0] = jnp.concatenate([e1, e2], axis=0)

    tm = x.shape[0]
    eidx = lax.broadcasted_iota(jnp.int32, (N_EXPERTS, tm), 0)
    oh1 = eidx == e1
    oh2 = eidx == e2
    oh = jnp.where(oh1 | oh2, 1.0, 0.0)
    before = lax.broadcasted_iota(jnp.int32, (tm, tm), 0) < lax.broadcasted_iota(jnp.int32, (tm, tm), 1)
    prefix = _dot(oh.astype(BF16), jnp.where(before, 1.0, 0.0).astype(BF16))
    base = carry_ref[...] + prefix
    r1 = jnp.sum(jnp.where(oh1, base, 0.0), axis=0, keepdims=True)
    r2 = jnp.sum(jnp.where(oh2, base, 0.0), axis=0, keepdims=True)
    rank_ref[0] = jnp.concatenate([r1, r2], axis=0).astype(jnp.int32)
    carry = carry_ref[...] + jnp.sum(oh, axis=1, keepdims=True)
    carry_ref[...] = carry
    cnt_ref[...] = jnp.broadcast_to(carry, cnt_ref.shape).astype(jnp.int32)


def _router(x, gain, sc, sh, wr_hi, wr_lo, b_router):
    b, n, d = x.shape
    tm = min(512, n)
    ne = wr_hi.shape[0]
    pair = pl.BlockSpec((1, 2, tm), lambda bb, i: (bb, 0, i))
    return pl.pallas_call(
        _router_kernel,
        grid=(b, n // tm),
        in_specs=[
            pl.BlockSpec((1, tm, d), lambda bb, i: (bb, i, 0)),
            pl.BlockSpec((1, d), lambda bb, i: (0, 0)),
            pl.BlockSpec((1, 1, d), lambda bb, i: (bb, 0, 0)),
            pl.BlockSpec((1, 1, d), lambda bb, i: (bb, 0, 0)),
            pl.BlockSpec((ne, d), lambda bb, i: (0, 0)),
            pl.BlockSpec((ne, d), lambda bb, i: (0, 0)),
            pl.BlockSpec((ne, 1), lambda bb, i: (0, 0)),
        ],
        out_specs=[
            pl.BlockSpec((1, tm, d), lambda bb, i: (bb, i, 0)),
            pair, pair, pair,
            pl.BlockSpec((ne, HEAD_DIM), lambda bb, i: (0, 0)),
        ],
        out_shape=[
            jax.ShapeDtypeStruct((b, n, d), F32),
            jax.ShapeDtypeStruct((b, 2, n), jnp.int32),
            jax.ShapeDtypeStruct((b, 2, n), F32),
            jax.ShapeDtypeStruct((b, 2, n), jnp.int32),
            jax.ShapeDtypeStruct((ne, HEAD_DIM), jnp.int32),
        ],
        scratch_shapes=[pltpu.VMEM((ne, 1), F32)],
        compiler_params=_cparams(("arbitrary", "arbitrary")),
        name="norm_router",
    )(x, gain.reshape(1, d), sc, sh, wr_hi, wr_lo, b_router.reshape(ne, 1))


MOE_TILE = 256
MOE_SCATTER_TILE = 1024
MOE_COMBINE_TILE = 256
MOE_DMA_UNROLL = 8


def _moe_plan(e, rank, cnt):
    b, _, n = e.shape
    t = b * n
    counts = cnt[:, 0]
    ntile_e = (counts + MOE_TILE - 1) // MOE_TILE
    tile_end = jnp.cumsum(ntile_e)
    off = (tile_end - ntile_e) * MOE_TILE
    nt = 2 * t // MOE_TILE + N_EXPERTS
    tile_expert = jnp.sum(jnp.arange(nt)[:, None] >= tile_end[None, :], axis=1)
    tile_expert = jnp.minimum(tile_expert, N_EXPERTS - 1).astype(jnp.int32)
    ef = e.transpose(1, 0, 2).reshape(2, t)
    rf = rank.transpose(1, 0, 2).reshape(2, t)
    pos = rf + jnp.sum(jnp.where(ef[..., None] == jnp.arange(N_EXPERTS), off, 0), axis=-1)
    return pos.reshape(2 * t).astype(jnp.int32), tile_expert, tile_end[-1:].astype(jnp.int32), nt


def _scatter_kernel(pos_ref, h_ref, xs0_hbm, xs_hbm, sem, *, t):
    del xs0_hbm
    tm = h_ref.shape[0]
    base = pl.program_id(0) * tm

    def copies(j):
        return [pltpu.make_async_copy(h_ref.at[pl.ds(j, 1)], xs_hbm.at[pl.ds(pos_ref[k * t + base + j], 1)], sem)
                for k in range(2)]

    def start(j, c):
        for cp in copies(j):
            cp.start()
        return c

    def wait(j, c):
        for cp in copies(j):
            cp.wait()
        return c

    lax.fori_loop(0, tm, start, 0, unroll=MOE_DMA_UNROLL)
    lax.fori_loop(0, tm, wait, 0, unroll=MOE_DMA_UNROLL)


def _scatter_rows(pos, h, xs_buf):
    t, d = h.shape
    tm = min(MOE_SCATTER_TILE, t)
    any_spec = pl.BlockSpec(memory_space=pl.ANY)
    return pl.pallas_call(
        functools.partial(_scatter_kernel, t=t),
        grid_spec=pltpu.PrefetchScalarGridSpec(
            num_scalar_prefetch=1, grid=(t // tm,),
            in_specs=[pl.BlockSpec((tm, d), lambda i, p: (i, 0)), any_spec], out_specs=any_spec,
            scratch_shapes=[pltpu.SemaphoreType.DMA(())]),
        out_shape=jax.ShapeDtypeStruct(xs_buf.shape, F32),
        input_output_aliases={2: 0},
        compiler_params=pltpu.CompilerParams(dimension_semantics=("arbitrary",), has_side_effects=True,
                                             vmem_limit_bytes=V7X_VMEM_LIMIT),
        name="moe_scatter",
    )(pos, h, xs_buf)


def _experts_kernel(te_ref, nv_ref, xs_ref, wgu_ref, wd_ref, ys_ref):
    del te_ref

    @pl.when(pl.program_id(0) < nv_ref[0])
    def _():
        gu = _dot(xs_ref[...].astype(BF16), wgu_ref[0, 0])
        ff = gu.shape[1] // 2
        gate = gu[:, :ff]
        a = (gate * jax.nn.sigmoid(gate) * gu[:, ff:]).astype(BF16)
        ys_ref[...] = _dot(a, wd_ref[0, 0])


def _experts(xs, tile_expert, nvalid, w_gu, w_down, layer):
    nrows, d = xs.shape
    nt = nrows // MOE_TILE
    f2 = w_gu.shape[-1]
    row = lambda i, te, nv: (jnp.minimum(i, nv[0] - 1), 0)
    wmap = lambda i, te, nv: (layer, te[jnp.minimum(i, nv[0] - 1)], 0, 0)
    return pl.pallas_call(
        _experts_kernel,
        grid_spec=pltpu.PrefetchScalarGridSpec(
            num_scalar_prefetch=2, grid=(nt,),
            in_specs=[
                pl.BlockSpec((MOE_TILE, d), row),
                pl.BlockSpec((1, 1, d, f2), wmap),
                pl.BlockSpec((1, 1, f2 // 2, d), wmap),
            ],
            out_specs=pl.BlockSpec((MOE_TILE, d), row)),
        out_shape=jax.ShapeDtypeStruct((nrows, d), F32),
        compiler_params=_cparams(("arbitrary",)),
        name="moe_experts",
    )(tile_expert, nvalid, xs, w_gu, w_down)


def _combine_kernel(pos_ref, x_ref, w_ref, g2_ref, ys_hbm, o_ref, buf, sem, *, t):
    i = pl.program_id(0)
    nsteps = pl.num_programs(0)
    tm = x_ref.shape[0]

    def copies(tile, slot, j):
        tok = tile * tm + j
        return [pltpu.make_async_copy(ys_hbm.at[pl.ds(pos_ref[k * t + tok], 1)],
                                      buf.at[slot, k, pl.ds(j, 1)], sem.at[slot]) for k in range(2)]

    def issue(tile, slot):
        def body(j, c):
            for cp in copies(tile, slot, j):
                cp.start()
            return c
        lax.fori_loop(0, tm, body, 0, unroll=MOE_DMA_UNROLL)

    def wait(tile, slot):
        def body(j, c):
            for cp in copies(tile, slot, j):
                cp.wait()
            return c
        lax.fori_loop(0, tm, body, 0, unroll=MOE_DMA_UNROLL)

    @pl.when(i == 0)
    def _():
        issue(0, 0)

    @pl.when(i + 1 < nsteps)
    def _():
        issue(i + 1, (i + 1) % 2)

    slot = i % 2
    wait(i, slot)
    w = w_ref[...]
    y = w[:, 0:1] * buf[slot, 0] + w[:, 1:2] * buf[slot, 1]
    o_ref[...] = x_ref[...] + g2_ref[0] * y


def _combine(pos, x, w, g2, ys):
    b, n, d = x.shape
    t = b * n
    tm = min(MOE_COMBINE_TILE, n)
    per_b = n // tm
    out = pl.pallas_call(
        functools.partial(_combine_kernel, t=t),
        grid_spec=pltpu.PrefetchScalarGridSpec(
            num_scalar_prefetch=1, grid=(t // tm,),
            in_specs=[
                pl.BlockSpec((tm, d), lambda i, p: (i, 0)),
                pl.BlockSpec((tm, 2), lambda i, p: (i, 0)),
                pl.BlockSpec((1, 1, d), lambda i, p: (i // per_b, 0, 0)),
                pl.BlockSpec(memory_space=pl.ANY),
            ],
            out_specs=pl.BlockSpec((tm, d), lambda i, p: (i, 0)),
            scratch_shapes=[pltpu.VMEM((2, 2, tm, d), F32), pltpu.SemaphoreType.DMA((2,))]),
        out_shape=jax.ShapeDtypeStruct((t, d), F32),
        compiler_params=_cparams(("arbitrary",)),
        name="moe_combine",
    )(pos, x.reshape(t, d), w, g2, ys)
    return out.reshape(b, n, d)


def _moe(x, gain, sc, sh, g2, wr_hi, wr_lo, b_router, w_gu, w_down, layer, xs_buf):
    b, n, d = x.shape
    h, e, w, rank, cnt = _router(x, gain, sc, sh, wr_hi, wr_lo, b_router)
    pos, tile_expert, nvalid, nt = _moe_plan(e, rank, cnt)
    if xs_buf is None:
        xs_buf = jnp.zeros((nt * MOE_TILE, d), F32)
    xs = _scatter_rows(pos, h.reshape(b * n, d), xs_buf)
    ys = _experts(xs, tile_expert, nvalid, w_gu, w_down, layer)
    return _combine(pos, x, w.transpose(0, 2, 1).reshape(b * n, 2), g2, ys), xs


def _rope_tables(n):
    t = jnp.arange(n, dtype=jnp.int32)
    row = (t // GRID_W).astype(F32)
    col = (t % GRID_W).astype(F32)
    axis_dim = HEAD_DIM // 2
    inv = ROPE_THETA ** (-jnp.arange(0, axis_dim, 2, dtype=F32) / axis_dim)
    ang = jnp.concatenate([row[:, None] * inv, col[:, None] * inv], axis=-1)
    cos, sin = jnp.cos(ang), jnp.sin(ang)
    return jnp.concatenate([cos, cos], axis=-1), jnp.concatenate([-sin, sin], axis=-1)


def kernel(x, c, ctx, c_ctx, w_mod, b_mod, norm1, norm2, w_in, qk_gain, pool_w, pool_scale,
           na_rpb, w_br, w_out, w_router, b_router, w_gu, w_down):
    b, n, d = x.shape
    depth = w_mod.shape[0]
    rows = n // GRID_W
    kr = min(NA_ROWS, rows)
    assert n % GRID_W == 0 and rows % kr == 0 and b + 1 <= 8

    cos, sin = _rope_tables(n)
    mods = _modulation(jnp.concatenate([c, c_ctx[None, :]], axis=0), w_mod, b_mod)
    wr_t = w_router.T
    wr_hi = wr_t.astype(BF16)
    wr_lo = (wr_t - wr_hi.astype(F32)).astype(BF16)
    w_br_b, w_out_b = w_br.astype(BF16), w_out.astype(BF16)
    w_gu_b, w_down_b = w_gu.astype(BF16), w_down.astype(BF16)
    xs_x = xs_c = None

    for l in range(depth):
        last = l == depth - 1
        mx = mods[l, :b].reshape(b, 1, N_MOD, d)
        mc = jnp.broadcast_to(mods[l, b].reshape(1, 1, N_MOD, d), (b, 1, N_MOD, d))
        x_sh1, x_sc1, x_g1, x_sh2, x_sc2, x_g2 = [mx[:, :, k] for k in range(N_MOD)]
        c_sh1, c_sc1, c_g1, c_sh2, c_sc2, c_g2 = [mc[:, :, k] for k in range(N_MOD)]
        pool_w_l = pool_w[l].astype(BF16)

        def proj(t, sc, sh, kind, tables=(None, None)):
            return _inproj(t, norm1[l], sc, sh, w_in, l, qk_gain[l], *tables, kind=kind)

        pckv = proj(ctx, c_sc1, c_sh1, "kv")
        pq = proj(x, x_sc1, x_sh1, "q", (cos, sin))
        pkv = proj(x, x_sc1, x_sh1, "kv", (cos, sin))
        pg = proj(x, x_sc1, x_sh1, "gates")
        y_pool = _pool(pq, pool_w_l, pool_scale[l])
        y_na = _na(pq, pkv, pckv, na_rpb[l])
        y_gqa = _gqa(pq, pkv, pckv, qk_gain[l, 2], qk_gain[l, 3])
        x = _merge(x, pg, y_pool, y_na, 0, y_gqa, 0, w_br_b, w_out_b, l, x_g1)

        moe_w = (wr_hi, wr_lo, b_router, w_gu_b, w_down_b, l)
        if not last:
            pcq = proj(ctx, c_sc1, c_sh1, "q")
            pcg = proj(ctx, c_sc1, c_sh1, "gates")
            yc_pool = _pool(pcq, pool_w_l, pool_scale[l])
            yc = _ctx_attn(pcq, pckv)
            ctx = _merge(ctx, pcg, yc_pool, yc, 0, yc, 1, w_br_b, w_out_b, l, c_g1)
            ctx, xs_c = _moe(ctx, norm2[l], c_sc2, c_sh2, c_g2, *moe_w, xs_c)
        x, xs_x = _moe(x, norm2[l], x_sc2, x_sh2, x_g2, *moe_w, xs_x)
    return x
```

```python
import functools

import numpy as np
import jax
import jax.numpy as jnp
from jax import lax
from jax.experimental import pallas as pl
from jax.experimental.pallas import tpu as pltpu

F32 = jnp.float32
BF16 = jnp.bfloat16

GRID_W = 64
HEAD_DIM = 128
ROPE_THETA = 10000.0
EPS = 1e-6
POOL_WINDOWS = (2, 4, 8, 16)
POOL_CH = 128
POOL_WIDTH = len(POOL_WINDOWS) * POOL_CH
NA_HEADS = 6
NA_WIDTH = NA_HEADS * HEAD_DIM
NA_ROWS = 8
NA_COLS = 16
GQA_Q_HEADS = 6
GQA_KV_HEADS = 2
GQA_GROUP = GQA_Q_HEADS // GQA_KV_HEADS
GQA_Q_WIDTH = GQA_Q_HEADS * HEAD_DIM
N_BRANCH = 3
N_EXPERTS = 16
N_GROUPS = 4
EXPERTS_PER_GROUP = N_EXPERTS // N_GROUPS
N_MOD = 6
ATTN_SCALE = HEAD_DIM ** -0.5

COL_BLOCK = 2048
CB_POOL = 0
HB_NA_Q = POOL_WIDTH // HEAD_DIM
HB_GQA_Q = HB_NA_Q + NA_HEADS
HB_NA_K = 0
HB_NA_V = HB_NA_K + NA_HEADS
HB_GQA_K = HB_NA_V + NA_HEADS
HB_GQA_V = HB_GQA_K + GQA_KV_HEADS
PROJ_KINDS = {"gates": (0, 3), "q": (3, 1), "kv": (4, 1)}

V7X_VMEM_LIMIT = 56 * 1024 * 1024
NEG_BIG = -1e30


def _cparams(sem):
    return pltpu.CompilerParams(dimension_semantics=sem, vmem_limit_bytes=V7X_VMEM_LIMIT)


def _dot(a, b):
    return jnp.dot(a, b, preferred_element_type=F32)


def _dot_nt(a, b):
    return lax.dot_general(a, b, (((1,), (1,)), ((), ())), preferred_element_type=F32)


def _mod_kernel(ct_ref, w_ref, b_ref, o_ref, *, n_rows):
    ct = ct_ref[...]
    a = ct * jax.nn.sigmoid(ct)
    w = w_ref[0]
    rows = [jnp.sum(w * a[:, r:r + 1], axis=0, keepdims=True) for r in range(n_rows)]
    rows += [jnp.zeros_like(rows[0])] * (8 - n_rows)
    o_ref[0] = jnp.concatenate(rows, axis=0) + b_ref[0]


def _modulation(c_rows, w_mod, b_mod):
    depth, d, nm = w_mod.shape
    n_rows = c_rows.shape[0]
    ct = jnp.zeros((d, 8), F32).at[:, :n_rows].set(c_rows.T)
    tn = 1024
    return pl.pallas_call(
        functools.partial(_mod_kernel, n_rows=n_rows),
        grid=(depth, nm // tn),
        in_specs=[
            pl.BlockSpec((d, 8), lambda l, j: (0, 0)),
            pl.BlockSpec((1, d, tn), lambda l, j: (l, 0, j)),
            pl.BlockSpec((1, 1, tn), lambda l, j: (l, 0, j)),
        ],
        out_specs=pl.BlockSpec((1, 8, tn), lambda l, j: (l, 0, j)),
        out_shape=jax.ShapeDtypeStruct((depth, 8, nm), F32),
        compiler_params=_cparams(("arbitrary", "arbitrary")),
        name="adaln_mod",
    )(ct, w_mod, b_mod.reshape(depth, 1, nm))


def _head_norm(a, gain):
    return a * lax.rsqrt(jnp.mean(a * a, axis=-1, keepdims=True) + EPS) * gain


def _rope(y, cos, sin):
    return y * cos + pltpu.roll(y, HEAD_DIM // 2, 1) * sin


PROJ_CHUNK = 2 * HEAD_DIM
PROJ_ROW_SPLIT = 4
PROJ_ROW_TILE = 1024
PROJ_MIN_SUB_ROWS = 256


def _proj_epilogue(kind, head, a, qg_ref, rope_tables):
    if kind == "gates":
        return jax.nn.sigmoid(a)
    if kind == "q":
        if head < HB_NA_Q:
            return a
        if head < HB_GQA_Q:
            return _head_norm(a, qg_ref[0:1, :]) * ATTN_SCALE
        y = _head_norm(a, qg_ref[2:3, :])
        return (_rope(y, *rope_tables) if rope_tables else y) * ATTN_SCALE
    if head < HB_NA_V:
        return _head_norm(a, qg_ref[1:2, :])
    if HB_GQA_K <= head < HB_GQA_V:
        y = _head_norm(a, qg_ref[3:4, :])
        return _rope(y, *rope_tables) if rope_tables else y
    return a


def _inproj_kernel(*refs, kind, rope):
    if rope:
        x_ref, g_ref, sc_ref, sh_ref, w_ref, qg_ref, cos_ref, sin_ref, o_ref, wb_s = refs
    else:
        x_ref, g_ref, sc_ref, sh_ref, w_ref, qg_ref, o_ref, wb_s = refs

    @pl.when((pl.program_id(1) == 0) & (pl.program_id(2) == 0))
    def _():
        wb_s[...] = w_ref[0].astype(BF16)

    tm = x_ref.shape[1]
    nsub = PROJ_ROW_SPLIT if tm % (PROJ_ROW_SPLIT * PROJ_MIN_SUB_ROWS) == 0 else 1
    ts = tm // nsub
    per = PROJ_CHUNK // HEAD_DIM
    for r in range(nsub):
        rows = slice(r * ts, (r + 1) * ts)
        x = x_ref[0, rows, :]
        h = x * lax.rsqrt(jnp.mean(x * x, axis=-1, keepdims=True) + EPS) * g_ref[...]
        hb = (h * (1.0 + sc_ref[0]) + sh_ref[0]).astype(BF16)
        tables = (cos_ref[rows, :], sin_ref[rows, :]) if rope else None
        for c in range(COL_BLOCK // PROJ_CHUNK):
            acc = _dot(hb, wb_s[:, c * PROJ_CHUNK:(c + 1) * PROJ_CHUNK])
            for u in range(per):
                head = c * per + u
                y = _proj_epilogue(kind, head, acc[:, u * HEAD_DIM:(u + 1) * HEAD_DIM], qg_ref, tables)
                o_ref[0, rows, head * HEAD_DIM:(head + 1) * HEAD_DIM] = y.astype(BF16)


def _inproj(x, gain, sc, sh, w_in, layer, qgain, cos, sin, *, kind):
    b, n, d = x.shape
    rope = cos is not None and kind != "gates"
    tm = min(PROJ_ROW_TILE, n)
    j0, nj = PROJ_KINDS[kind]
    in_specs = [
        pl.BlockSpec((1, tm, d), lambda j, bb, i: (bb, i, 0)),
        pl.BlockSpec((1, d), lambda j, bb, i: (0, 0)),
        pl.BlockSpec((1, 1, d), lambda j, bb, i: (bb, 0, 0)),
        pl.BlockSpec((1, 1, d), lambda j, bb, i: (bb, 0, 0)),
        pl.BlockSpec((1, d, COL_BLOCK), lambda j, bb, i: (layer, 0, j + j0), pipeline_mode=pl.Buffered(1)),
        pl.BlockSpec((4, HEAD_DIM), lambda j, bb, i: (0, 0)),
    ]
    args = [x, gain.reshape(1, d), sc, sh, w_in, qgain]
    if rope:
        in_specs += [pl.BlockSpec((tm, HEAD_DIM), lambda j, bb, i: (i, 0))] * 2
        args += [cos, sin]
    return pl.pallas_call(
        functools.partial(_inproj_kernel, kind=kind, rope=rope),
        grid=(nj, b, n // tm),
        in_specs=in_specs,
        out_specs=pl.BlockSpec((1, tm, COL_BLOCK), lambda j, bb, i: (bb, i, j)),
        out_shape=jax.ShapeDtypeStruct((b, n, nj * COL_BLOCK), BF16),
        scratch_shapes=[pltpu.VMEM((d, COL_BLOCK), BF16)],
        compiler_params=_cparams(("arbitrary", "arbitrary", "arbitrary")),
        name="inproj_" + kind + ("_rope" if rope else ""),
    )(*args)


POOL_HALO = 16


def _pool_kernel(prev_ref, cur_ref, next_ref, w_ref, s_ref, o_ref, buf_ref, *, tm, n):
    i = pl.program_id(1)
    nt = pl.num_programs(1)
    hl = POOL_HALO
    buf_ref[pl.ds(hl, tm), :] = cur_ref[0].astype(F32)
    buf_ref[pl.ds(0, hl), :] = jnp.where(i > 0, prev_ref[0].astype(F32), 0.0)
    buf_ref[pl.ds(hl + tm, hl), :] = jnp.where(i < nt - 1, next_ref[0].astype(F32), 0.0)
    t = i * tm + lax.broadcasted_iota(jnp.int32, (tm, 1), 0)
    for g, w in enumerate(POOL_WINDOWS):
        sl = slice(g * POOL_CH, (g + 1) * POOL_CH)
        acc = buf_ref[pl.ds(hl - w // 2, tm), sl]
        for off in range(-w // 2 + 1, w // 2):
            acc = acc + buf_ref[pl.ds(hl + off, tm), sl]
        cnt = (jnp.minimum(t + w // 2, n) - jnp.maximum(t - w // 2, 0)).astype(F32)
        dlt = acc / cnt - buf_ref[pl.ds(hl, tm), sl]
        y = _dot(dlt.astype(BF16), w_ref[g]) * s_ref[:, sl]
        o_ref[0, :, sl] = y.astype(BF16)


def _pool(p, pool_w, pool_scale):
    b, n, _ = p.shape
    tm = min(512, n)
    hl = POOL_HALO
    hb = tm // hl
    last = n // hl - 1
    return pl.pallas_call(
        functools.partial(_pool_kernel, tm=tm, n=n),
        grid=(b, n // tm),
        in_specs=[
            pl.BlockSpec((1, hl, POOL_WIDTH), lambda bb, i: (bb, jnp.maximum(i * hb - 1, 0), CB_POOL)),
            pl.BlockSpec((1, tm, POOL_WIDTH), lambda bb, i: (bb, i, CB_POOL)),
            pl.BlockSpec((1, hl, POOL_WIDTH), lambda bb, i: (bb, jnp.minimum((i + 1) * hb, last), CB_POOL)),
            pl.BlockSpec((len(POOL_WINDOWS), POOL_CH, POOL_CH), lambda bb, i: (0, 0, 0)),
            pl.BlockSpec((1, POOL_WIDTH), lambda bb, i: (0, 0)),
        ],
        out_specs=pl.BlockSpec((1, tm, POOL_WIDTH), lambda bb, i: (bb, i, 0)),
        out_shape=jax.ShapeDtypeStruct((b, n, POOL_WIDTH), BF16),
        scratch_shapes=[pltpu.VMEM((tm + 2 * hl, POOL_WIDTH), F32)],
        compiler_params=_cparams(("arbitrary", "arbitrary")),
        name="pool_mixer",
    )(p, p, p, pool_w, pool_scale.reshape(1, POOL_WIDTH))


NA_QROWS = 4
NA_UROWS = 12
NA_BLOCKS_PER_STEP = 8


def _na_plan(rows, kr):
    rq, ku = NA_QROWS, NA_UROWS
    if rows < ku or rows % rq:
        rq, ku = 1, kr
    starts, keys = [], []
    for r0 in range(0, rows, rq):
        rs = [min(max(r - kr // 2, 0), rows - kr) for r in range(r0, r0 + rq)]
        us = min(rs[0], rows - ku)
        starts.append(us)
        keys.append(tuple((us - r, rs_q - us) for r, rs_q in zip(range(r0, r0 + rq), rs)))
    tables = sorted(set(keys))
    table_of = np.array([tables.index(k) for k in keys], np.int32)
    return rq, ku, np.array(starts, np.int32), table_of, tables


def _na_bias(rpb, kr, ku, tables):
    col = np.arange(GRID_W)
    col_start = np.clip(col - NA_COLS // 2, 0, GRID_W - NA_COLS)
    dcol = col[None, :] - col[:, None] + (NA_COLS - 1)
    ok = (col[None, :] >= col_start[:, None]) & (col[None, :] < col_start[:, None] + NA_COLS)
    onehot = ((dcol[:, :, None] == np.arange(2 * NA_COLS - 1)) & ok[:, :, None]).astype(np.float32)
    nt, rq = len(tables), len(tables[0])
    rowsel = np.zeros((nt, rq, ku, 2 * NA_ROWS - 1), np.float32)
    for t, key in enumerate(tables):
        for q, (rel0, first) in enumerate(key):
            for i in range(first, first + kr):
                rowsel[t, q, i, rel0 + i + NA_ROWS - 1] = 1.0
    keep = (rowsel.sum(-1) > 0)[:, None, :, None, :, None] & ok[None, None, None, :, None, :]
    bias = jnp.einsum("hrc,tqir,xyc->thqxiy", rpb.astype(F32), rowsel, onehot,
                      precision=lax.Precision.HIGHEST)
    bias = jnp.where(keep, bias, NEG_BIG)
    return bias.reshape(nt, rpb.shape[0], rq * GRID_W, ku * GRID_W)


def _na_kernel(us_ref, tb_ref, q_ref, k_ref, v_ref, kc_ref, vc_ref, bias_ref, o_ref, sl_s, sc_s,
               *, nb, mq, mk):
    i = pl.program_id(2)
    kc = kc_ref[0]
    vc = vc_ref[0]

    def koff(bb):
        return pl.multiple_of(us_ref[i * nb + bb] * GRID_W, GRID_W)

    def scores(bb, slot):
        q = q_ref[0, bb * mq:(bb + 1) * mq, :]
        sl_s[slot] = _dot_nt(q, k_ref[0, pl.ds(koff(bb), mk), :]) + bias_ref[tb_ref[i * nb + bb], 0]
        sc_s[slot] = _dot_nt(q, kc)

    def finish(bb, slot):
        s_loc = sl_s[slot]
        s_ctx = sc_s[slot]
        m = jnp.maximum(jnp.max(s_loc, axis=-1, keepdims=True), jnp.max(s_ctx, axis=-1, keepdims=True))
        p_loc = jnp.exp(s_loc - m)
        p_ctx = jnp.exp(s_ctx - m)
        den = jnp.sum(p_loc, axis=-1, keepdims=True) + jnp.sum(p_ctx, axis=-1, keepdims=True)
        o = _dot(p_loc.astype(BF16), v_ref[0, pl.ds(koff(bb), mk), :]) + _dot(p_ctx.astype(BF16), vc)
        o_ref[0, bb * mq:(bb + 1) * mq, :] = (o / den).astype(BF16)

    scores(0, 0)
    for bb in range(nb):
        if bb + 1 < nb:
            scores(bb + 1, (bb + 1) % 2)
        finish(bb, bb % 2)


def _na(pq, pkv, pckv, rpb):
    b, n, _ = pq.shape
    nc = pckv.shape[1]
    rows = n // GRID_W
    kr = min(NA_ROWS, rows)
    rq, ku, starts, table_of, tables = _na_plan(rows, kr)
    bias = _na_bias(rpb, kr, ku, tables)
    nblk = rows // rq
    nb = min(NA_BLOCKS_PER_STEP, nblk)
    assert nblk % nb == 0
    mq, mk = rq * GRID_W, ku * GRID_W
    return pl.pallas_call(
        functools.partial(_na_kernel, nb=nb, mq=mq, mk=mk),
        grid_spec=pltpu.PrefetchScalarGridSpec(
            num_scalar_prefetch=2, grid=(b, NA_HEADS, nblk // nb),
            in_specs=[
                pl.BlockSpec((1, nb * mq, HEAD_DIM), lambda bb, h, i, us, tb: (bb, i, HB_NA_Q + h)),
                pl.BlockSpec((1, n, HEAD_DIM), lambda bb, h, i, us, tb: (bb, 0, HB_NA_K + h)),
                pl.BlockSpec((1, n, HEAD_DIM), lambda bb, h, i, us, tb: (bb, 0, HB_NA_V + h)),
                pl.BlockSpec((1, nc, HEAD_DIM), lambda bb, h, i, us, tb: (bb, 0, HB_NA_K + h)),
                pl.BlockSpec((1, nc, HEAD_DIM), lambda bb, h, i, us, tb: (bb, 0, HB_NA_V + h)),
                pl.BlockSpec((len(tables), 1, mq, mk), lambda bb, h, i, us, tb: (0, h, 0, 0)),
            ],
            out_specs=pl.BlockSpec((1, nb * mq, HEAD_DIM), lambda bb, h, i, us, tb: (bb, i, h)),
            scratch_shapes=[pltpu.VMEM((2, mq, mk), F32), pltpu.VMEM((2, mq, nc), F32)]),
        out_shape=jax.ShapeDtypeStruct((b, n, NA_WIDTH), BF16),
        compiler_params=_cparams(("arbitrary", "arbitrary", "arbitrary")),
        name="na_attn",
    )(jnp.asarray(starts), jnp.asarray(table_of), pq, pkv, pkv, pckv, pckv, bias)


GQA_CHUNKS_PER_TRIP = 4
GQA_Q_TILE = 512


def _gqa_kernel(bound_ref, q0_ref, q1_ref, q2_ref, k_ref, v_ref, kc_ref, vc_ref, o_ref,
                qt_s, vt_s, vct_s, s_s, sc_s, m_s, den_s, acc_s, *, tk, n, fixed_shift):
    i = pl.program_id(2)
    tq = q0_ref.shape[1]
    nchunk = n // tk
    hd = HEAD_DIM

    def to_t(a):
        return a.astype(F32).T.astype(BF16)

    @pl.when(i == 0)
    def _():
        def tr(c, carry):
            off = pl.multiple_of(c * tk, tk)
            vt_s[c] = to_t(v_ref[0, pl.ds(off, tk), :])
            return carry
        lax.fori_loop(0, nchunk, tr, 0)
        vct_s[...] = to_t(vc_ref[0])

    for g, qr in enumerate((q0_ref, q1_ref, q2_ref)):
        qt_s[:, g * tq:(g + 1) * tq] = to_t(qr[0])

    def scores(c, slot):
        off = pl.multiple_of(c * tk, tk)
        s_s[slot] = _dot(k_ref[0, pl.ds(off, tk), :], qt_s[...])

    def col_sums(p):
        return jnp.sum(p.reshape(p.shape[0] // 8, 8, p.shape[1]), axis=0)

    def update(s, vt, first=False):
        if fixed_shift:
            p = jnp.exp(s - bound_ref[0])
            pv = _dot(vt, p.astype(BF16))
            den_s[...] = col_sums(p) if first else den_s[...] + col_sums(p)
            acc_s[...] = pv if first else acc_s[...] + pv
            return
        smax = jnp.max(s, axis=0, keepdims=True)
        if first:
            m_new = smax
        else:
            m_prev = m_s[...]
            m_new = jnp.maximum(m_prev, smax)
            alpha = jnp.exp(m_prev - m_new)
        p = jnp.exp(s - m_new)
        pv = _dot(vt, p.astype(BF16))
        den_s[...] = col_sums(p) if first else alpha * den_s[...] + col_sums(p)
        acc_s[...] = pv if first else alpha * acc_s[...] + pv
        m_s[...] = m_new

    scores(0, 0)
    sc_s[...] = _dot(kc_ref[0], qt_s[...])
    update(sc_s[...], vct_s[...], first=True)

    per_trip = GQA_CHUNKS_PER_TRIP if nchunk % GQA_CHUNKS_PER_TRIP == 0 else 2

    def body(ct, carry):
        c = per_trip * ct
        for u in range(per_trip):
            scores(jnp.minimum(c + u + 1, nchunk - 1), (u + 1) % 2)
            update(s_s[u % 2], vt_s[c + u])
        return carry

    lax.fori_loop(0, nchunk // per_trip, body, 0)
    o = (acc_s[...] / jnp.sum(den_s[...], axis=0, keepdims=True)).T
    for g in range(GQA_GROUP):
        o_ref[0, :, g * hd:(g + 1) * hd] = o[g * tq:(g + 1) * tq].astype(BF16)


GQA_FIXED_SHIFT_LIMIT = 40.0


def _gqa(pq, pkv, pckv, gain_q, gain_k):
    bound = 1.01 * HEAD_DIM * ATTN_SCALE * jnp.max(jnp.abs(gain_q)) * jnp.max(jnp.abs(gain_k))
    bound = bound.astype(F32).reshape(1)
    return lax.cond(bound[0] <= GQA_FIXED_SHIFT_LIMIT,
                    functools.partial(_gqa_call, fixed_shift=True),
                    functools.partial(_gqa_call, fixed_shift=False),
                    pq, pkv, pckv, bound)


def _gqa_call(pq, pkv, pckv, bound, *, fixed_shift):
    b, n, _ = pq.shape
    nc = pckv.shape[1]
    tq = min(GQA_Q_TILE, n)
    tk = min(512, n // 2)
    assert n % (2 * tk) == 0
    nq = GQA_GROUP * tq

    def qspec(g):
        return pl.BlockSpec((1, tq, HEAD_DIM), lambda bb, h, i: (bb, i, HB_GQA_Q + h * GQA_GROUP + g))

    return pl.pallas_call(
        functools.partial(_gqa_kernel, tk=tk, n=n, fixed_shift=fixed_shift),
        grid=(b, GQA_KV_HEADS, n // tq),
        in_specs=[
            pl.BlockSpec(memory_space=pltpu.SMEM),
            qspec(0), qspec(1), qspec(2),
            pl.BlockSpec((1, n, HEAD_DIM), lambda bb, h, i: (bb, 0, HB_GQA_K + h)),
            pl.BlockSpec((1, n, HEAD_DIM), lambda bb, h, i: (bb, 0, HB_GQA_V + h)),
            pl.BlockSpec((1, nc, HEAD_DIM), lambda bb, h, i: (bb, 0, HB_GQA_K + h)),
            pl.BlockSpec((1, nc, HEAD_DIM), lambda bb, h, i: (bb, 0, HB_GQA_V + h)),
        ],
        out_specs=pl.BlockSpec((1, tq, GQA_GROUP * HEAD_DIM), lambda bb, h, i: (bb, i, h)),
        out_shape=jax.ShapeDtypeStruct((b, n, GQA_Q_WIDTH), BF16),
        scratch_shapes=[
            pltpu.VMEM((HEAD_DIM, nq), BF16),
            pltpu.VMEM((n // tk, HEAD_DIM, tk), BF16),
            pltpu.VMEM((HEAD_DIM, nc), BF16),
            pltpu.VMEM((2, tk, nq), F32),
            pltpu.VMEM((nc, nq), F32),
            pltpu.VMEM((1, nq), F32),
            pltpu.VMEM((8, nq), F32),
            pltpu.VMEM((HEAD_DIM, nq), F32),
        ],
        compiler_params=_cparams(("arbitrary", "arbitrary", "arbitrary")),
        name="gqa_attn_fixed_shift" if fixed_shift else "gqa_attn_running_max",
    )(bound, pq, pq, pq, pkv, pkv, pckv, pckv)


def _ctx_attn_kernel(q_ref, k_ref, v_ref, o_ref):
    s = _dot_nt(q_ref[0], k_ref[0])
    m = jnp.max(s, axis=-1, keepdims=True)
    p = jnp.exp(s - m)
    den = jnp.sum(p, axis=-1, keepdims=True)
    o_ref[0] = (_dot(p.astype(BF16), v_ref[0]) / den).astype(BF16)


def _ctx_attn(pcq, pckv):
    b, nc, _ = pcq.shape
    nh = NA_HEADS + GQA_Q_HEADS

    def kmap(bb, h):
        g = jnp.maximum(h - NA_HEADS, 0) // GQA_GROUP
        return bb, 0, jnp.where(h < NA_HEADS, HB_NA_K + h, HB_GQA_K + g)

    def vmap_(bb, h):
        g = jnp.maximum(h - NA_HEADS, 0) // GQA_GROUP
        return bb, 0, jnp.where(h < NA_HEADS, HB_NA_V + h, HB_GQA_V + g)

    return pl.pallas_call(
        _ctx_attn_kernel,
        grid=(b, nh),
        in_specs=[
            pl.BlockSpec((1, nc, HEAD_DIM), lambda bb, h: (bb, 0, HB_NA_Q + h)),
            pl.BlockSpec((1, nc, HEAD_DIM), kmap),
            pl.BlockSpec((1, nc, HEAD_DIM), vmap_),
        ],
        out_specs=pl.BlockSpec((1, nc, HEAD_DIM), lambda bb, h: (bb, 0, h)),
        out_shape=jax.ShapeDtypeStruct((b, nc, nh * HEAD_DIM), BF16),
        compiler_params=_cparams(("arbitrary", "arbitrary")),
        name="ctx_attn",
    )(pcq, pckv, pckv)


def _merge_kernel(x_ref, ga_ref, gb_ref, gc_ref, yp_ref, yn_ref, yg_ref, wbr_ref, wout_ref, g1_ref, o_ref):
    r1 = POOL_WIDTH
    r2 = POOL_WIDTH + NA_WIDTH
    z = ga_ref[0].astype(F32) * _dot(yp_ref[0], wbr_ref[0, 0:r1, :])
    z = z + gb_ref[0].astype(F32) * _dot(yn_ref[0], wbr_ref[0, r1:r2, :])
    z = z + gc_ref[0].astype(F32) * _dot(yg_ref[0], wbr_ref[0, r2:, :])
    o_ref[0] = x_ref[0] + g1_ref[0] * _dot(z.astype(BF16), wout_ref[0])


def _merge(x, p, y_pool, y_na, na_cb, y_gqa, gqa_cb, w_br, w_out, layer, g1):
    b, n, d = x.shape
    tm = min(256, n)
    const = lambda bb, i: (layer, 0, 0)
    return pl.pallas_call(
        _merge_kernel,
        grid=(b, n // tm),
        in_specs=[
            pl.BlockSpec((1, tm, d), lambda bb, i: (bb, i, 0)),
            pl.BlockSpec((1, tm, d), lambda bb, i: (bb, i, 0)),
            pl.BlockSpec((1, tm, d), lambda bb, i: (bb, i, 1)),
            pl.BlockSpec((1, tm, d), lambda bb, i: (bb, i, 2)),
            pl.BlockSpec((1, tm, POOL_WIDTH), lambda bb, i: (bb, i, 0)),
            pl.BlockSpec((1, tm, NA_WIDTH), lambda bb, i: (bb, i, na_cb)),
            pl.BlockSpec((1, tm, GQA_Q_WIDTH), lambda bb, i: (bb, i, gqa_cb)),
            pl.BlockSpec((1,) + w_br.shape[1:], const, pipeline_mode=pl.Buffered(1)),
            pl.BlockSpec((1,) + w_out.shape[1:], const, pipeline_mode=pl.Buffered(1)),
            pl.BlockSpec((1, 1, d), lambda bb, i: (bb, 0, 0)),
        ],
        out_specs=pl.BlockSpec((1, tm, d), lambda bb, i: (bb, i, 0)),
        out_shape=jax.ShapeDtypeStruct((b, n, d), F32),
        compiler_params=_cparams(("arbitrary", "arbitrary")),
        name="branch_merge",
    )(x, p, p, p, y_pool, y_na, y_gqa, w_br, w_out, g1)


def _top2_of4(a, b, c, d):
    hi1, lo1 = jnp.maximum(a, b), jnp.minimum(a, b)
    hi2, lo2 = jnp.maximum(c, d), jnp.minimum(c, d)
    return jnp.maximum(hi1, hi2) + jnp.maximum(jnp.minimum(hi1, hi2), jnp.maximum(lo1, lo2))


def _router_kernel(x_ref, g_ref, sc_ref, sh_ref, whi_ref, wlo_ref, br_ref,
                   h_ref, e_ref, w_ref, rank_ref, cnt_ref, carry_ref):
    first = (pl.program_id(0) == 0) & (pl.program_id(1) == 0)

    @pl.when(first)
    def _():
        carry_ref[...] = jnp.zeros_like(carry_ref)

    x = x_ref[0]
    h = x * lax.rsqrt(jnp.mean(x * x, axis=-1, keepdims=True) + EPS) * g_ref[...]
    h = h * (1.0 + sc_ref[0]) + sh_ref[0]
    h_ref[0] = h
    h_hi = h.astype(BF16)
    h_lo = (h - h_hi.astype(F32)).astype(BF16)
    whi = whi_ref[...]
    logit = _dot_nt(whi, h_hi) + _dot_nt(whi, h_lo) + _dot_nt(wlo_ref[...], h_hi)
    s = jax.nn.sigmoid(logit)
    sel = s + br_ref[...]
    epg = EXPERTS_PER_GROUP
    row = lambda a, e: a[e:e + 1, :]
    gscore = [_top2_of4(*[row(sel, g * epg + j) for j in range(epg)]) for g in range(N_GROUPS)]
    g_best = jnp.zeros_like(gscore[0], dtype=jnp.int32)
    best = gscore[0]
    for g in range(1, N_GROUPS):
        upd = gscore[g] > best
        g_best = jnp.where(upd, g, g_best)
        best = jnp.where(upd, gscore[g], best)
    vs, ss = [], []
    for j in range(epg):
        v = row(sel, j)
        sv = row(s, j)
        for g in range(1, N_GROUPS):
            v = jnp.where(g_best == g, row(sel, g * epg + j), v)
            sv = jnp.where(g_best == g, row(s, g * epg + j), sv)
        vs.append(v)
        ss.append(sv)
    i1 = jnp.zeros_like(g_best)
    v1 = vs[0]
    for j in range(1, epg):
        upd = vs[j] > v1
        i1 = jnp.where(upd, j, i1)
        v1 = jnp.where(upd, vs[j], v1)
    i2 = jnp.full_like(g_best, -1)
    v2 = jnp.full_like(v1, -jnp.inf)
    for j in range(epg):
        upd = (i1 != j) & ((i2 < 0) | (vs[j] > v2))
        i2 = jnp.where(upd, j, i2)
        v2 = jnp.where(upd, vs[j], v2)
    w1 = sum(jnp.where(i1 == j, ss[j], 0.0) for j in range(epg))
    w2 = sum(jnp.where(i2 == j, ss[j], 0.0) for j in range(epg))
    tot = w1 + w2
    w_ref[0] = jnp.concatenate([w1 / tot, w2 / tot], axis=0)
    e1 = g_best * epg + i1
    e2 = g_best * epg + i2
    e_ref[0] = jnp.concatenate([e1, e2], axis=0)

    tm = x.shape[0]
    eidx = lax.broadcasted_iota(jnp.int32, (N_EXPERTS, tm), 0)
    oh1 = eidx == e1
    oh2 = eidx == e2
    oh = jnp.where(oh1 | oh2, 1.0, 0.0)
    before = lax.broadcasted_iota(jnp.int32, (tm, tm), 0) < lax.broadcasted_iota(jnp.int32, (tm, tm), 1)
    prefix = _dot(oh.astype(BF16), jnp.where(before, 1.0, 0.0).astype(BF16))
    base = carry_ref[...] + prefix
    r1 = jnp.sum(jnp.where(oh1, base, 0.0), axis=0, keepdims=True)
    r2 = jnp.sum(jnp.where(oh2, base, 0.0), axis=0, keepdims=True)
    rank_ref[0] = jnp.concatenate([r1, r2], axis=0).astype(jnp.int32)
    carry = carry_ref[...] + jnp.sum(oh, axis=1, keepdims=True)
    carry_ref[...] = carry
    cnt_ref[...] = jnp.broadcast_to(carry, cnt_ref.shape).astype(jnp.int32)


def _router(x, gain, sc, sh, wr_hi, wr_lo, b_router):
    b, n, d = x.shape
    tm = min(512, n)
    ne = wr_hi.shape[0]
    pair = pl.BlockSpec((1, 2, tm), lambda bb, i: (bb, 0, i))
    return pl.pallas_call(
        _router_kernel,
        grid=(b, n // tm),
        in_specs=[
            pl.BlockSpec((1, tm, d), lambda bb, i: (bb, i, 0)),
            pl.BlockSpec((1, d), lambda bb, i: (0, 0)),
            pl.BlockSpec((1, 1, d), lambda bb, i: (bb, 0, 0)),
            pl.BlockSpec((1, 1, d), lambda bb, i: (bb, 0, 0)),
            pl.BlockSpec((ne, d), lambda bb, i: (0, 0)),
            pl.BlockSpec((ne, d), lambda bb, i: (0, 0)),
            pl.BlockSpec((ne, 1), lambda bb, i: (0, 0)),
        ],
        out_specs=[
            pl.BlockSpec((1, tm, d), lambda bb, i: (bb, i, 0)),
            pair, pair, pair,
            pl.BlockSpec((ne, HEAD_DIM), lambda bb, i: (0, 0)),
        ],
        out_shape=[
            jax.ShapeDtypeStruct((b, n, d), F32),
            jax.ShapeDtypeStruct((b, 2, n), jnp.int32),
            jax.ShapeDtypeStruct((b, 2, n), F32),
            jax.ShapeDtypeStruct((b, 2, n), jnp.int32),
            jax.ShapeDtypeStruct((ne, HEAD_DIM), jnp.int32),
        ],
        scratch_shapes=[pltpu.VMEM((ne, 1), F32)],
        compiler_params=_cparams(("arbitrary", "arbitrary")),
        name="norm_router",
    )(x, gain.reshape(1, d), sc, sh, wr_hi, wr_lo, b_router.reshape(ne, 1))


MOE_TILE = 256
MOE_SCATTER_TILE = 1024
MOE_COMBINE_TILE = 256
MOE_DMA_UNROLL = 8


def _moe_plan(e, rank, cnt):
    b, _, n = e.shape
    t = b * n
    counts = cnt[:, 0]
    ntile_e = (counts + MOE_TILE - 1) // MOE_TILE
    tile_end = jnp.cumsum(ntile_e)
    off = (tile_end - ntile_e) * MOE_TILE
    nt = 2 * t // MOE_TILE + N_EXPERTS
    tile_expert = jnp.sum(jnp.arange(nt)[:, None] >= tile_end[None, :], axis=1)
    tile_expert = jnp.minimum(tile_expert, N_EXPERTS - 1).astype(jnp.int32)
    ef = e.transpose(1, 0, 2).reshape(2, t)
    rf = rank.transpose(1, 0, 2).reshape(2, t)
    pos = rf + jnp.sum(jnp.where(ef[..., None] == jnp.arange(N_EXPERTS), off, 0), axis=-1)
    return pos.reshape(2 * t).astype(jnp.int32), tile_expert, tile_end[-1:].astype(jnp.int32), nt


def _scatter_kernel(pos_ref, h_ref, xs0_hbm, xs_hbm, sem, *, t):
    del xs0_hbm
    tm = h_ref.shape[0]
    base = pl.program_id(0) * tm

    def copies(j):
        return [pltpu.make_async_copy(h_ref.at[pl.ds(j, 1)], xs_hbm.at[pl.ds(pos_ref[k * t + base + j], 1)], sem)
                for k in range(2)]

    def start(j, c):
        for cp in copies(j):
            cp.start()
        return c

    def wait(j, c):
        for cp in copies(j):
            cp.wait()
        return c

    lax.fori_loop(0, tm, start, 0, unroll=MOE_DMA_UNROLL)
    lax.fori_loop(0, tm, wait, 0, unroll=MOE_DMA_UNROLL)


def _scatter_rows(pos, h, xs_buf):
    t, d = h.shape
    tm = min(MOE_SCATTER_TILE, t)
    any_spec = pl.BlockSpec(memory_space=pl.ANY)
    return pl.pallas_call(
        functools.partial(_scatter_kernel, t=t),
        grid_spec=pltpu.PrefetchScalarGridSpec(
            num_scalar_prefetch=1, grid=(t // tm,),
            in_specs=[pl.BlockSpec((tm, d), lambda i, p: (i, 0)), any_spec], out_specs=any_spec,
            scratch_shapes=[pltpu.SemaphoreType.DMA(())]),
        out_shape=jax.ShapeDtypeStruct(xs_buf.shape, F32),
        input_output_aliases={2: 0},
        compiler_params=pltpu.CompilerParams(dimension_semantics=("arbitrary",), has_side_effects=True,
                                             vmem_limit_bytes=V7X_VMEM_LIMIT),
        name="moe_scatter",
    )(pos, h, xs_buf)


def _experts_kernel(te_ref, nv_ref, xs_ref, wgu_ref, wd_ref, ys_ref):
    del te_ref

    @pl.when(pl.program_id(0) < nv_ref[0])
    def _():
        gu = _dot(xs_ref[...].astype(BF16), wgu_ref[0, 0])
        ff = gu.shape[1] // 2
        gate = gu[:, :ff]
        a = (gate * jax.nn.sigmoid(gate) * gu[:, ff:]).astype(BF16)
        ys_ref[...] = _dot(a, wd_ref[0, 0])


def _experts(xs, tile_expert, nvalid, w_gu, w_down, layer):
    nrows, d = xs.shape
    nt = nrows // MOE_TILE
    f2 = w_gu.shape[-1]
    row = lambda i, te, nv: (jnp.minimum(i, nv[0] - 1), 0)
    wmap = lambda i, te, nv: (layer, te[jnp.minimum(i, nv[0] - 1)], 0, 0)
    return pl.pallas_call(
        _experts_kernel,
        grid_spec=pltpu.PrefetchScalarGridSpec(
            num_scalar_prefetch=2, grid=(nt,),
            in_specs=[
                pl.BlockSpec((MOE_TILE, d), row),
                pl.BlockSpec((1, 1, d, f2), wmap),
                pl.BlockSpec((1, 1, f2 // 2, d), wmap),
            ],
            out_specs=pl.BlockSpec((MOE_TILE, d), row)),
        out_shape=jax.ShapeDtypeStruct((nrows, d), F32),
        compiler_params=_cparams(("arbitrary",)),
        name="moe_experts",
    )(tile_expert, nvalid, xs, w_gu, w_down)


def _combine_kernel(pos_ref, x_ref, w_ref, g2_ref, ys_hbm, o_ref, buf, sem, *, t):
    i = pl.program_id(0)
    nsteps = pl.num_programs(0)
    tm = x_ref.shape[0]

    def copies(tile, slot, j):
        tok = tile * tm + j
        return [pltpu.make_async_copy(ys_hbm.at[pl.ds(pos_ref[k * t + tok], 1)],
                                      buf.at[slot, k, pl.ds(j, 1)], sem.at[slot]) for k in range(2)]

    def issue(tile, slot):
        def body(j, c):
            for cp in copies(tile, slot, j):
                cp.start()
            return c
        lax.fori_loop(0, tm, body, 0, unroll=MOE_DMA_UNROLL)

    def wait(tile, slot):
        def body(j, c):
            for cp in copies(tile, slot, j):
                cp.wait()
            return c
        lax.fori_loop(0, tm, body, 0, unroll=MOE_DMA_UNROLL)

    @pl.when(i == 0)
    def _():
        issue(0, 0)

    @pl.when(i + 1 < nsteps)
    def _():
        issue(i + 1, (i + 1) % 2)

    slot = i % 2
    wait(i, slot)
    w = w_ref[...]
    y = w[:, 0:1] * buf[slot, 0] + w[:, 1:2] * buf[slot, 1]
    o_ref[...] = x_ref[...] + g2_ref[0] * y


def _combine(pos, x, w, g2, ys):
    b, n, d = x.shape
    t = b * n
    tm = min(MOE_COMBINE_TILE, n)
    per_b = n // tm
    out = pl.pallas_call(
        functools.partial(_combine_kernel, t=t),
        grid_spec=pltpu.PrefetchScalarGridSpec(
            num_scalar_prefetch=1, grid=(t // tm,),
            in_specs=[
                pl.BlockSpec((tm, d), lambda i, p: (i, 0)),
                pl.BlockSpec((tm, 2), lambda i, p: (i, 0)),
                pl.BlockSpec((1, 1, d), lambda i, p: (i // per_b, 0, 0)),
                pl.BlockSpec(memory_space=pl.ANY),
            ],
            out_specs=pl.BlockSpec((tm, d), lambda i, p: (i, 0)),
            scratch_shapes=[pltpu.VMEM((2, 2, tm, d), F32), pltpu.SemaphoreType.DMA((2,))]),
        out_shape=jax.ShapeDtypeStruct((t, d), F32),
        compiler_params=_cparams(("arbitrary",)),
        name="moe_combine",
    )(pos, x.reshape(t, d), w, g2, ys)
    return out.reshape(b, n, d)


def _moe(x, gain, sc, sh, g2, wr_hi, wr_lo, b_router, w_gu, w_down, layer, xs_buf):
    b, n, d = x.shape
    h, e, w, rank, cnt = _router(x, gain, sc, sh, wr_hi, wr_lo, b_router)
    pos, tile_expert, nvalid, nt = _moe_plan(e, rank, cnt)
    if xs_buf is None:
        xs_buf = jnp.zeros((nt * MOE_TILE, d), F32)
    xs = _scatter_rows(pos, h.reshape(b * n, d), xs_buf)
    ys = _experts(xs, tile_expert, nvalid, w_gu, w_down, layer)
    return _combine(pos, x, w.transpose(0, 2, 1).reshape(b * n, 2), g2, ys), xs


def _rope_tables(n):
    t = jnp.arange(n, dtype=jnp.int32)
    row = (t // GRID_W).astype(F32)
    col = (t % GRID_W).astype(F32)
    axis_dim = HEAD_DIM // 2
    inv = ROPE_THETA ** (-jnp.arange(0, axis_dim, 2, dtype=F32) / axis_dim)
    ang = jnp.concatenate([row[:, None] * inv, col[:, None] * inv], axis=-1)
    cos, sin = jnp.cos(ang), jnp.sin(ang)
    return jnp.concatenate([cos, cos], axis=-1), jnp.concatenate([-sin, sin], axis=-1)


def kernel(x, c, ctx, c_ctx, w_mod, b_mod, norm1, norm2, w_in, qk_gain, pool_w, pool_scale,
           na_rpb, w_br, w_out, w_router, b_router, w_gu, w_down):
    b, n, d = x.shape
    depth = w_mod.shape[0]
    rows = n // GRID_W
    kr = min(NA_ROWS, rows)
    assert n % GRID_W == 0 and rows % kr == 0 and b + 1 <= 8

    cos, sin = _rope_tables(n)
    mods = _modulation(jnp.concatenate([c, c_ctx[None, :]], axis=0), w_mod, b_mod)
    wr_t = w_router.T
    wr_hi = wr_t.astype(BF16)
    wr_lo = (wr_t - wr_hi.astype(F32)).astype(BF16)
    w_br_b, w_out_b = w_br.astype(BF16), w_out.astype(BF16)
    w_gu_b, w_down_b = w_gu.astype(BF16), w_down.astype(BF16)
    xs_x = xs_c = None

    for l in range(depth):
        last = l == depth - 1
        mx = mods[l, :b].reshape(b, 1, N_MOD, d)
        mc = jnp.broadcast_to(mods[l, b].reshape(1, 1, N_MOD, d), (b, 1, N_MOD, d))
        x_sh1, x_sc1, x_g1, x_sh2, x_sc2, x_g2 = [mx[:, :, k] for k in range(N_MOD)]
        c_sh1, c_sc1, c_g1, c_sh2, c_sc2, c_g2 = [mc[:, :, k] for k in range(N_MOD)]
        pool_w_l = pool_w[l].astype(BF16)

        def proj(t, sc, sh, kind, tables=(None, None)):
            return _inproj(t, norm1[l], sc, sh, w_in, l, qk_gain[l], *tables, kind=kind)

        pckv = proj(ctx, c_sc1, c_sh1, "kv")
        pq = proj(x, x_sc1, x_sh1, "q", (cos, sin))
        pkv = proj(x, x_sc1, x_sh1, "kv", (cos, sin))
        pg = proj(x, x_sc1, x_sh1, "gates")
        y_pool = _pool(pq, pool_w_l, pool_scale[l])
        y_na = _na(pq, pkv, pckv, na_rpb[l])
        y_gqa = _gqa(pq, pkv, pckv, qk_gain[l, 2], qk_gain[l, 3])
        x = _merge(x, pg, y_pool, y_na, 0, y_gqa, 0, w_br_b, w_out_b, l, x_g1)

        moe_w = (wr_hi, wr_lo, b_router, w_gu_b, w_down_b, l)
        if not last:
            pcq = proj(ctx, c_sc1, c_sh1, "q")
            pcg = proj(ctx, c_sc1, c_sh1, "gates")
            yc_pool = _pool(pcq, pool_w_l, pool_scale[l])
            yc = _ctx_attn(pcq, pckv)
            ctx = _merge(ctx, pcg, yc_pool, yc, 0, yc, 1, w_br_b, w_out_b, l, c_g1)
            ctx, xs_c = _moe(ctx, norm2[l], c_sc2, c_sh2, c_g2, *moe_w, xs_c)
        x, xs_x = _moe(x, norm2[l], x_sc2, x_sh2, x_g2, *moe_w, xs_x)
    return x
```

```python
import functools

import numpy as np
import jax
import jax.numpy as jnp
from jax import lax
from jax.experimental import pallas as pl
from jax.experimental.pallas import tpu as pltpu

F32 = jnp.float32
BF16 = jnp.bfloat16

GRID_W = 64
HEAD_DIM = 128
ROPE_THETA = 10000.0
EPS = 1e-6
POOL_WINDOWS = (2, 4, 8, 16)
POOL_CH = 128
POOL_WIDTH = len(POOL_WINDOWS) * POOL_CH
NA_HEADS = 6
NA_WIDTH = NA_HEADS * HEAD_DIM
NA_ROWS = 8
NA_COLS = 16
GQA_Q_HEADS = 6
GQA_KV_HEADS = 2
GQA_GROUP = GQA_Q_HEADS // GQA_KV_HEADS
GQA_Q_WIDTH = GQA_Q_HEADS * HEAD_DIM
N_BRANCH = 3
N_EXPERTS = 16
N_GROUPS = 4
EXPERTS_PER_GROUP = N_EXPERTS // N_GROUPS
N_MOD = 6
ATTN_SCALE = HEAD_DIM ** -0.5

COL_BLOCK = 2048
CB_POOL = 0
HB_NA_Q = POOL_WIDTH // HEAD_DIM
HB_GQA_Q = HB_NA_Q + NA_HEADS
HB_NA_K = 0
HB_NA_V = HB_NA_K + NA_HEADS
HB_GQA_K = HB_NA_V + NA_HEADS
HB_GQA_V = HB_GQA_K + GQA_KV_HEADS
PROJ_KINDS = {"gates": (0, 3), "q": (3, 1), "kv": (4, 1)}

V7X_VMEM_LIMIT = 56 * 1024 * 1024
NEG_BIG = -1e30


def _cparams(sem):
    return pltpu.CompilerParams(dimension_semantics=sem, vmem_limit_bytes=V7X_VMEM_LIMIT)


def _dot(a, b):
    return jnp.dot(a, b, preferred_element_type=F32)


def _dot_nt(a, b):
    return lax.dot_general(a, b, (((1,), (1,)), ((), ())), preferred_element_type=F32)


def _mod_kernel(ct_ref, w_ref, b_ref, o_ref, *, n_rows):
    ct = ct_ref[...]
    a = ct * jax.nn.sigmoid(ct)
    w = w_ref[0]
    rows = [jnp.sum(w * a[:, r:r + 1], axis=0, keepdims=True) for r in range(n_rows)]
    rows += [jnp.zeros_like(rows[0])] * (8 - n_rows)
    o_ref[0] = jnp.concatenate(rows, axis=0) + b_ref[0]


def _modulation(c_rows, w_mod, b_mod):
    depth, d, nm = w_mod.shape
    n_rows = c_rows.shape[0]
    ct = jnp.zeros((d, 8), F32).at[:, :n_rows].set(c_rows.T)
    tn = 1024
    return pl.pallas_call(
        functools.partial(_mod_kernel, n_rows=n_rows),
        grid=(depth, nm // tn),
        in_specs=[
            pl.BlockSpec((d, 8), lambda l, j: (0, 0)),
            pl.BlockSpec((1, d, tn), lambda l, j: (l, 0, j)),
            pl.BlockSpec((1, 1, tn), lambda l, j: (l, 0, j)),
        ],
        out_specs=pl.BlockSpec((1, 8, tn), lambda l, j: (l, 0, j)),
        out_shape=jax.ShapeDtypeStruct((depth, 8, nm), F32),
        compiler_params=_cparams(("arbitrary", "arbitrary")),
        name="adaln_mod",
    )(ct, w_mod, b_mod.reshape(depth, 1, nm))


def _head_norm(a, gain):
    return a * lax.rsqrt(jnp.mean(a * a, axis=-1, keepdims=True) + EPS) * gain


def _rope(y, cos, sin):
    return y * cos + pltpu.roll(y, HEAD_DIM // 2, 1) * sin


PROJ_CHUNK = 2 * HEAD_DIM
PROJ_ROW_SPLIT = 4
PROJ_ROW_TILE = 1024
PROJ_MIN_SUB_ROWS = 256


def _proj_epilogue(kind, head, a, qg_ref, rope_tables):
    if kind == "gates":
        return jax.nn.sigmoid(a)
    if kind == "q":
        if head < HB_NA_Q:
            return a
        if head < HB_GQA_Q:
            return _head_norm(a, qg_ref[0:1, :]) * ATTN_SCALE
        y = _head_norm(a, qg_ref[2:3, :])
        return (_rope(y, *rope_tables) if rope_tables else y) * ATTN_SCALE
    if head < HB_NA_V:
        return _head_norm(a, qg_ref[1:2, :])
    if HB_GQA_K <= head < HB_GQA_V:
        y = _head_norm(a, qg_ref[3:4, :])
        return _rope(y, *rope_tables) if rope_tables else y
    return a


def _inproj_kernel(*refs, kind, rope):
    if rope:
        x_ref, g_ref, sc_ref, sh_ref, w_ref, qg_ref, cos_ref, sin_ref, o_ref, wb_s = refs
    else:
        x_ref, g_ref, sc_ref, sh_ref, w_ref, qg_ref, o_ref, wb_s = refs

    @pl.when((pl.program_id(1) == 0) & (pl.program_id(2) == 0))
    def _():
        wb_s[...] = w_ref[0].astype(BF16)

    tm = x_ref.shape[1]
    nsub = PROJ_ROW_SPLIT if tm % (PROJ_ROW_SPLIT * PROJ_MIN_SUB_ROWS) == 0 else 1
    ts = tm // nsub
    per = PROJ_CHUNK // HEAD_DIM
    for r in range(nsub):
        rows = slice(r * ts, (r + 1) * ts)
        x = x_ref[0, rows, :]
        h = x * lax.rsqrt(jnp.mean(x * x, axis=-1, keepdims=True) + EPS) * g_ref[...]
        hb = (h * (1.0 + sc_ref[0]) + sh_ref[0]).astype(BF16)
        tables = (cos_ref[rows, :], sin_ref[rows, :]) if rope else None
        for c in range(COL_BLOCK // PROJ_CHUNK):
            acc = _dot(hb, wb_s[:, c * PROJ_CHUNK:(c + 1) * PROJ_CHUNK])
            for u in range(per):
                head = c * per + u
                y = _proj_epilogue(kind, head, acc[:, u * HEAD_DIM:(u + 1) * HEAD_DIM], qg_ref, tables)
                o_ref[0, rows, head * HEAD_DIM:(head + 1) * HEAD_DIM] = y.astype(BF16)


def _inproj(x, gain, sc, sh, w_in, layer, qgain, cos, sin, *, kind):
    b, n, d = x.shape
    rope = cos is not None and kind != "gates"
    tm = min(PROJ_ROW_TILE, n)
    j0, nj = PROJ_KINDS[kind]
    in_specs = [
        pl.BlockSpec((1, tm, d), lambda j, bb, i: (bb, i, 0)),
        pl.BlockSpec((1, d), lambda j, bb, i: (0, 0)),
        pl.BlockSpec((1, 1, d), lambda j, bb, i: (bb, 0, 0)),
        pl.BlockSpec((1, 1, d), lambda j, bb, i: (bb, 0, 0)),
        pl.BlockSpec((1, d, COL_BLOCK), lambda j, bb, i: (layer, 0, j + j0), pipeline_mode=pl.Buffered(1)),
        pl.BlockSpec((4, HEAD_DIM), lambda j, bb, i: (0, 0)),
    ]
    args = [x, gain.reshape(1, d), sc, sh, w_in, qgain]
    if rope:
        in_specs += [pl.BlockSpec((tm, HEAD_DIM), lambda j, bb, i: (i, 0))] * 2
        args += [cos, sin]
    return pl.pallas_call(
        functools.partial(_inproj_kernel, kind=kind, rope=rope),
        grid=(nj, b, n // tm),
        in_specs=in_specs,
        out_specs=pl.BlockSpec((1, tm, COL_BLOCK), lambda j, bb, i: (bb, i, j)),
        out_shape=jax.ShapeDtypeStruct((b, n, nj * COL_BLOCK), BF16),
        scratch_shapes=[pltpu.VMEM((d, COL_BLOCK), BF16)],
        compiler_params=_cparams(("arbitrary", "arbitrary", "arbitrary")),
        name="inproj_" + kind + ("_rope" if rope else ""),
    )(*args)


POOL_HALO = 16


def _pool_kernel(prev_ref, cur_ref, next_ref, w_ref, s_ref, o_ref, buf_ref, *, tm, n):
    i = pl.program_id(1)
    nt = pl.num_programs(1)
    hl = POOL_HALO
    buf_ref[pl.ds(hl, tm), :] = cur_ref[0].astype(F32)
    buf_ref[pl.ds(0, hl), :] = jnp.where(i > 0, prev_ref[0].astype(F32), 0.0)
    buf_ref[pl.ds(hl + tm, hl), :] = jnp.where(i < nt - 1, next_ref[0].astype(F32), 0.0)
    t = i * tm + lax.broadcasted_iota(jnp.int32, (tm, 1), 0)
    for g, w in enumerate(POOL_WINDOWS):
        sl = slice(g * POOL_CH, (g + 1) * POOL_CH)
        acc = buf_ref[pl.ds(hl - w // 2, tm), sl]
        for off in range(-w // 2 + 1, w // 2):
            acc = acc + buf_ref[pl.ds(hl + off, tm), sl]
        cnt = (jnp.minimum(t + w // 2, n) - jnp.maximum(t - w // 2, 0)).astype(F32)
        dlt = acc / cnt - buf_ref[pl.ds(hl, tm), sl]
        y = _dot(dlt.astype(BF16), w_ref[g]) * s_ref[:, sl]
        o_ref[0, :, sl] = y.astype(BF16)


def _pool(p, pool_w, pool_scale):
    b, n, _ = p.shape
    tm = min(512, n)
    hl = POOL_HALO
    hb = tm // hl
    last = n // hl - 1
    return pl.pallas_call(
        functools.partial(_pool_kernel, tm=tm, n=n),
        grid=(b, n // tm),
        in_specs=[
            pl.BlockSpec((1, hl, POOL_WIDTH), lambda bb, i: (bb, jnp.maximum(i * hb - 1, 0), CB_POOL)),
            pl.BlockSpec((1, tm, POOL_WIDTH), lambda bb, i: (bb, i, CB_POOL)),
            pl.BlockSpec((1, hl, POOL_WIDTH), lambda bb, i: (bb, jnp.minimum((i + 1) * hb, last), CB_POOL)),
            pl.BlockSpec((len(POOL_WINDOWS), POOL_CH, POOL_CH), lambda bb, i: (0, 0, 0)),
            pl.BlockSpec((1, POOL_WIDTH), lambda bb, i: (0, 0)),
        ],
        out_specs=pl.BlockSpec((1, tm, POOL_WIDTH), lambda bb, i: (bb, i, 0)),
        out_shape=jax.ShapeDtypeStruct((b, n, POOL_WIDTH), BF16),
        scratch_shapes=[pltpu.VMEM((tm + 2 * hl, POOL_WIDTH), F32)],
        compiler_params=_cparams(("arbitrary", "arbitrary")),
        name="pool_mixer",
    )(p, p, p, pool_w, pool_scale.reshape(1, POOL_WIDTH))


NA_QROWS = 4
NA_UROWS = 12
NA_BLOCKS_PER_STEP = 8


def _na_plan(rows, kr):
    rq, ku = NA_QROWS, NA_UROWS
    if rows < ku or rows % rq:
        rq, ku = 1, kr
    starts, keys = [], []
    for r0 in range(0, rows, rq):
        rs = [min(max(r - kr // 2, 0), rows - kr) for r in range(r0, r0 + rq)]
        us = min(rs[0], rows - ku)
        starts.append(us)
        keys.append(tuple((us - r, rs_q - us) for r, rs_q in zip(range(r0, r0 + rq), rs)))
    tables = sorted(set(keys))
    table_of = np.array([tables.index(k) for k in keys], np.int32)
    return rq, ku, np.array(starts, np.int32), table_of, tables


def _na_bias(rpb, kr, ku, tables):
    col = np.arange(GRID_W)
    col_start = np.clip(col - NA_COLS // 2, 0, GRID_W - NA_COLS)
    dcol = col[None, :] - col[:, None] + (NA_COLS - 1)
    ok = (col[None, :] >= col_start[:, None]) & (col[None, :] < col_start[:, None] + NA_COLS)
    onehot = ((dcol[:, :, None] == np.arange(2 * NA_COLS - 1)) & ok[:, :, None]).astype(np.float32)
    nt, rq = len(tables), len(tables[0])
    rowsel = np.zeros((nt, rq, ku, 2 * NA_ROWS - 1), np.float32)
    for t, key in enumerate(tables):
        for q, (rel0, first) in enumerate(key):
            for i in range(first, first + kr):
                rowsel[t, q, i, rel0 + i + NA_ROWS - 1] = 1.0
    keep = (rowsel.sum(-1) > 0)[:, None, :, None, :, None] & ok[None, None, None, :, None, :]
    bias = jnp.einsum("hrc,tqir,xyc->thqxiy", rpb.astype(F32), rowsel, onehot,
                      precision=lax.Precision.HIGHEST)
    bias = jnp.where(keep, bias, NEG_BIG)
    return bias.reshape(nt, rpb.shape[0], rq * GRID_W, ku * GRID_W)


def _na_kernel(us_ref, tb_ref, q_ref, k_ref, v_ref, kc_ref, vc_ref, bias_ref, o_ref, sl_s, sc_s,
               *, nb, mq, mk):
    i = pl.program_id(2)
    kc = kc_ref[0]
    vc = vc_ref[0]

    def koff(bb):
        return pl.multiple_of(us_ref[i * nb + bb] * GRID_W, GRID_W)

    def scores(bb, slot):
        q = q_ref[0, bb * mq:(bb + 1) * mq, :]
        sl_s[slot] = _dot_nt(q, k_ref[0, pl.ds(koff(bb), mk), :]) + bias_ref[tb_ref[i * nb + bb], 0]
        sc_s[slot] = _dot_nt(q, kc)

    def finish(bb, slot):
        s_loc = sl_s[slot]
        s_ctx = sc_s[slot]
        m = jnp.maximum(jnp.max(s_loc, axis=-1, keepdims=True), jnp.max(s_ctx, axis=-1, keepdims=True))
        p_loc = jnp.exp(s_loc - m)
        p_ctx = jnp.exp(s_ctx - m)
        den = jnp.sum(p_loc, axis=-1, keepdims=True) + jnp.sum(p_ctx, axis=-1, keepdims=True)
        o = _dot(p_loc.astype(BF16), v_ref[0, pl.ds(koff(bb), mk), :]) + _dot(p_ctx.astype(BF16), vc)
        o_ref[0, bb * mq:(bb + 1) * mq, :] = (o / den).astype(BF16)

    scores(0, 0)
    for bb in range(nb):
        if bb + 1 < nb:
            scores(bb + 1, (bb + 1) % 2)
        finish(bb, bb % 2)


def _na(pq, pkv, pckv, rpb):
    b, n, _ = pq.shape
    nc = pckv.shape[1]
    rows = n // GRID_W
    kr = min(NA_ROWS, rows)
    rq, ku, starts, table_of, tables = _na_plan(rows, kr)
    bias = _na_bias(rpb, kr, ku, tables)
    nblk = rows // rq
    nb = min(NA_BLOCKS_PER_STEP, nblk)
    assert nblk % nb == 0
    mq, mk = rq * GRID_W, ku * GRID_W
    return pl.pallas_call(
        functools.partial(_na_kernel, nb=nb, mq=mq, mk=mk),
        grid_spec=pltpu.PrefetchScalarGridSpec(
            num_scalar_prefetch=2, grid=(b, NA_HEADS, nblk // nb),
            in_specs=[
                pl.BlockSpec((1, nb * mq, HEAD_DIM), lambda bb, h, i, us, tb: (bb, i, HB_NA_Q + h)),
                pl.BlockSpec((1, n, HEAD_DIM), lambda bb, h, i, us, tb: (bb, 0, HB_NA_K + h)),
                pl.BlockSpec((1, n, HEAD_DIM), lambda bb, h, i, us, tb: (bb, 0, HB_NA_V + h)),
                pl.BlockSpec((1, nc, HEAD_DIM), lambda bb, h, i, us, tb: (bb, 0, HB_NA_K + h)),
                pl.BlockSpec((1, nc, HEAD_DIM), lambda bb, h, i, us, tb: (bb, 0, HB_NA_V + h)),
                pl.BlockSpec((len(tables), 1, mq, mk), lambda bb, h, i, us, tb: (0, h, 0, 0)),
            ],
            out_specs=pl.BlockSpec((1, nb * mq, HEAD_DIM), lambda bb, h, i, us, tb: (bb, i, h)),
            scratch_shapes=[pltpu.VMEM((2, mq, mk), F32), pltpu.VMEM((2, mq, nc), F32)]),
        out_shape=jax.ShapeDtypeStruct((b, n, NA_WIDTH), BF16),
        compiler_params=_cparams(("arbitrary", "arbitrary", "arbitrary")),
        name="na_attn",
    )(jnp.asarray(starts), jnp.asarray(table_of), pq, pkv, pkv, pckv, pckv, bias)


GQA_CHUNKS_PER_TRIP = 4
GQA_Q_TILE = 512


def _gqa_kernel(bound_ref, q0_ref, q1_ref, q2_ref, k_ref, v_ref, kc_ref, vc_ref, o_ref,
                qt_s, vt_s, vct_s, s_s, sc_s, m_s, den_s, acc_s, *, tk, n, fixed_shift):
    i = pl.program_id(2)
    tq = q0_ref.shape[1]
    nchunk = n // tk
    hd = HEAD_DIM

    def to_t(a):
        return a.astype(F32).T.astype(BF16)

    @pl.when(i == 0)
    def _():
        def tr(c, carry):
            off = pl.multiple_of(c * tk, tk)
            vt_s[c] = to_t(v_ref[0, pl.ds(off, tk), :])
            return carry
        lax.fori_loop(0, nchunk, tr, 0)
        vct_s[...] = to_t(vc_ref[0])

    for g, qr in enumerate((q0_ref, q1_ref, q2_ref)):
        qt_s[:, g * tq:(g + 1) * tq] = to_t(qr[0])

    def scores(c, slot):
        off = pl.multiple_of(c * tk, tk)
        s_s[slot] = _dot(k_ref[0, pl.ds(off, tk), :], qt_s[...])

    def col_sums(p):
        return jnp.sum(p.reshape(p.shape[0] // 8, 8, p.shape[1]), axis=0)

    def update(s, vt, first=False):
        if fixed_shift:
            p = jnp.exp(s - bound_ref[0])
            pv = _dot(vt, p.astype(BF16))
            den_s[...] = col_sums(p) if first else den_s[...] + col_sums(p)
            acc_s[...] = pv if first else acc_s[...] + pv
            return
        smax = jnp.max(s, axis=0, keepdims=True)
        if first:
            m_new = smax
        else:
            m_prev = m_s[...]
            m_new = jnp.maximum(m_prev, smax)
            alpha = jnp.exp(m_prev - m_new)
        p = jnp.exp(s - m_new)
        pv = _dot(vt, p.astype(BF16))
        den_s[...] = col_sums(p) if first else alpha * den_s[...] + col_sums(p)
        acc_s[...] = pv if first else alpha * acc_s[...] + pv
        m_s[...] = m_new

    scores(0, 0)
    sc_s[...] = _dot(kc_ref[0], qt_s[...])
    update(sc_s[...], vct_s[...], first=True)

    per_trip = GQA_CHUNKS_PER_TRIP if nchunk % GQA_CHUNKS_PER_TRIP == 0 else 2

    def body(ct, carry):
        c = per_trip * ct
        for u in range(per_trip):
            scores(jnp.minimum(c + u + 1, nchunk - 1), (u + 1) % 2)
            update(s_s[u % 2], vt_s[c + u])
        return carry

    lax.fori_loop(0, nchunk // per_trip, body, 0)
    o = (acc_s[...] / jnp.sum(den_s[...], axis=0, keepdims=True)).T
    for g in range(GQA_GROUP):
        o_ref[0, :, g * hd:(g + 1) * hd] = o[g * tq:(g + 1) * tq].astype(BF16)


GQA_FIXED_SHIFT_LIMIT = 40.0


def _gqa(pq, pkv, pckv, gain_q, gain_k):
    bound = 1.01 * HEAD_DIM * ATTN_SCALE * jnp.max(jnp.abs(gain_q)) * jnp.max(jnp.abs(gain_k))
    bound = bound.astype(F32).reshape(1)
    return lax.cond(bound[0] <= GQA_FIXED_SHIFT_LIMIT,
                    functools.partial(_gqa_call, fixed_shift=True),
                    functools.partial(_gqa_call, fixed_shift=False),
                    pq, pkv, pckv, bound)


def _gqa_call(pq, pkv, pckv, bound, *, fixed_shift):
    b, n, _ = pq.shape
    nc = pckv.shape[1]
    tq = min(GQA_Q_TILE, n)
    tk = min(512, n // 2)
    assert n % (2 * tk) == 0
    nq = GQA_GROUP * tq

    def qspec(g):
        return pl.BlockSpec((1, tq, HEAD_DIM), lambda bb, h, i: (bb, i, HB_GQA_Q + h * GQA_GROUP + g))

    return pl.pallas_call(
        functools.partial(_gqa_kernel, tk=tk, n=n, fixed_shift=fixed_shift),
        grid=(b, GQA_KV_HEADS, n // tq),
        in_specs=[
            pl.BlockSpec(memory_space=pltpu.SMEM),
            qspec(0), qspec(1), qspec(2),
            pl.BlockSpec((1, n, HEAD_DIM), lambda bb, h, i: (bb, 0, HB_GQA_K + h)),
            pl.BlockSpec((1, n, HEAD_DIM), lambda bb, h, i: (bb, 0, HB_GQA_V + h)),
            pl.BlockSpec((1, nc, HEAD_DIM), lambda bb, h, i: (bb, 0, HB_GQA_K + h)),
            pl.BlockSpec((1, nc, HEAD_DIM), lambda bb, h, i: (bb, 0, HB_GQA_V + h)),
        ],
        out_specs=pl.BlockSpec((1, tq, GQA_GROUP * HEAD_DIM), lambda bb, h, i: (bb, i, h)),
        out_shape=jax.ShapeDtypeStruct((b, n, GQA_Q_WIDTH), BF16),
        scratch_shapes=[
            pltpu.VMEM((HEAD_DIM, nq), BF16),
            pltpu.VMEM((n // tk, HEAD_DIM, tk), BF16),
            pltpu.VMEM((HEAD_DIM, nc), BF16),
            pltpu.VMEM((2, tk, nq), F32),
            pltpu.VMEM((nc, nq), F32),
            pltpu.VMEM((1, nq), F32),
            pltpu.VMEM((8, nq), F32),
            pltpu.VMEM((HEAD_DIM, nq), F32),
        ],
        compiler_params=_cparams(("arbitrary", "arbitrary", "arbitrary")),
        name="gqa_attn_fixed_shift" if fixed_shift else "gqa_attn_running_max",
    )(bound, pq, pq, pq, pkv, pkv, pckv, pckv)


def _ctx_attn_kernel(q_ref, k_ref, v_ref, o_ref):
    s = _dot_nt(q_ref[0], k_ref[0])
    m = jnp.max(s, axis=-1, keepdims=True)
    p = jnp.exp(s - m)
    den = jnp.sum(p, axis=-1, keepdims=True)
    o_ref[0] = (_dot(p.astype(BF16), v_ref[0]) / den).astype(BF16)


def _ctx_attn(pcq, pckv):
    b, nc, _ = pcq.shape
    nh = NA_HEADS + GQA_Q_HEADS

    def kmap(bb, h):
        g = jnp.maximum(h - NA_HEADS, 0) // GQA_GROUP
        return bb, 0, jnp.where(h < NA_HEADS, HB_NA_K + h, HB_GQA_K + g)

    def vmap_(bb, h):
        g = jnp.maximum(h - NA_HEADS, 0) // GQA_GROUP
        return bb, 0, jnp.where(h < NA_HEADS, HB_NA_V + h, HB_GQA_V + g)

    return pl.pallas_call(
        _ctx_attn_kernel,
        grid=(b, nh),
        in_specs=[
            pl.BlockSpec((1, nc, HEAD_DIM), lambda bb, h: (bb, 0, HB_NA_Q + h)),
            pl.BlockSpec((1, nc, HEAD_DIM), kmap),
            pl.BlockSpec((1, nc, HEAD_DIM), vmap_),
        ],
        out_specs=pl.BlockSpec((1, nc, HEAD_DIM), lambda bb, h: (bb, 0, h)),
        out_shape=jax.ShapeDtypeStruct((b, nc, nh * HEAD_DIM), BF16),
        compiler_params=_cparams(("arbitrary", "arbitrary")),
        name="ctx_attn",
    )(pcq, pckv, pckv)


def _merge_kernel(x_ref, ga_ref, gb_ref, gc_ref, yp_ref, yn_ref, yg_ref, wbr_ref, wout_ref, g1_ref, o_ref):
    r1 = POOL_WIDTH
    r2 = POOL_WIDTH + NA_WIDTH
    z = ga_ref[0].astype(F32) * _dot(yp_ref[0], wbr_ref[0, 0:r1, :])
    z = z + gb_ref[0].astype(F32) * _dot(yn_ref[0], wbr_ref[0, r1:r2, :])
    z = z + gc_ref[0].astype(F32) * _dot(yg_ref[0], wbr_ref[0, r2:, :])
    o_ref[0] = x_ref[0] + g1_ref[0] * _dot(z.astype(BF16), wout_ref[0])


def _merge(x, p, y_pool, y_na, na_cb, y_gqa, gqa_cb, w_br, w_out, layer, g1):
    b, n, d = x.shape
    tm = min(256, n)
    const = lambda bb, i: (layer, 0, 0)
    return pl.pallas_call(
        _merge_kernel,
        grid=(b, n // tm),
        in_specs=[
            pl.BlockSpec((1, tm, d), lambda bb, i: (bb, i, 0)),
            pl.BlockSpec((1, tm, d), lambda bb, i: (bb, i, 0)),
            pl.BlockSpec((1, tm, d), lambda bb, i: (bb, i, 1)),
            pl.BlockSpec((1, tm, d), lambda bb, i: (bb, i, 2)),
            pl.BlockSpec((1, tm, POOL_WIDTH), lambda bb, i: (bb, i, 0)),
            pl.BlockSpec((1, tm, NA_WIDTH), lambda bb, i: (bb, i, na_cb)),
            pl.BlockSpec((1, tm, GQA_Q_WIDTH), lambda bb, i: (bb, i, gqa_cb)),
            pl.BlockSpec((1,) + w_br.shape[1:], const, pipeline_mode=pl.Buffered(1)),
            pl.BlockSpec((1,) + w_out.shape[1:], const, pipeline_mode=pl.Buffered(1)),
            pl.BlockSpec((1, 1, d), lambda bb, i: (bb, 0, 0)),
        ],
        out_specs=pl.BlockSpec((1, tm, d), lambda bb, i: (bb, i, 0)),
        out_shape=jax.ShapeDtypeStruct((b, n, d), F32),
        compiler_params=_cparams(("arbitrary", "arbitrary")),
        name="branch_merge",
    )(x, p, p, p, y_pool, y_na, y_gqa, w_br, w_out, g1)


def _top2_of4(a, b, c, d):
    hi1, lo1 = jnp.maximum(a, b), jnp.minimum(a, b)
    hi2, lo2 = jnp.maximum(c, d), jnp.minimum(c, d)
    return jnp.maximum(hi1, hi2) + jnp.maximum(jnp.minimum(hi1, hi2), jnp.maximum(lo1, lo2))


def _router_kernel(x_ref, g_ref, sc_ref, sh_ref, whi_ref, wlo_ref, br_ref, c0_ref,
                   h_ref, e_ref, w_ref, rank_ref, cnt_ref, carry_ref):
    first = (pl.program_id(0) == 0) & (pl.program_id(1) == 0)

    @pl.when(first)
    def _():
        carry_ref[...] = c0_ref[...]

    x = x_ref[0]
    h = x * lax.rsqrt(jnp.mean(x * x, axis=-1, keepdims=True) + EPS) * g_ref[...]
    h = h * (1.0 + sc_ref[0]) + sh_ref[0]
    h_ref[0] = h
    h_hi = h.astype(BF16)
    h_lo = (h - h_hi.astype(F32)).astype(BF16)
    whi = whi_ref[...]
    logit = _dot_nt(whi, h_hi) + _dot_nt(whi, h_lo) + _dot_nt(wlo_ref[...], h_hi)
    s = jax.nn.sigmoid(logit)
    sel = s + br_ref[...]
    epg = EXPERTS_PER_GROUP
    row = lambda a, e: a[e:e + 1, :]
    gscore = [_top2_of4(*[row(sel, g * epg + j) for j in range(epg)]) for g in range(N_GROUPS)]
    g_best = jnp.zeros_like(gscore[0], dtype=jnp.int32)
    best = gscore[0]
    for g in range(1, N_GROUPS):
        upd = gscore[g] > best
        g_best = jnp.where(upd, g, g_best)
        best = jnp.where(upd, gscore[g], best)
    vs, ss = [], []
    for j in range(epg):
        v = row(sel, j)
        sv = row(s, j)
        for g in range(1, N_GROUPS):
            v = jnp.where(g_best == g, row(sel, g * epg + j), v)
            sv = jnp.where(g_best == g, row(s, g * epg + j), sv)
        vs.append(v)
        ss.append(sv)
    i1 = jnp.zeros_like(g_best)
    v1 = vs[0]
    for j in range(1, epg):
        upd = vs[j] > v1
        i1 = jnp.where(upd, j, i1)
        v1 = jnp.where(upd, vs[j], v1)
    i2 = jnp.full_like(g_best, -1)
    v2 = jnp.full_like(v1, -jnp.inf)
    for j in range(epg):
        upd = (i1 != j) & ((i2 < 0) | (vs[j] > v2))
        i2 = jnp.where(upd, j, i2)
        v2 = jnp.where(upd, vs[j], v2)
    w1 = sum(jnp.where(i1 == j, ss[j], 0.0) for j in range(epg))
    w2 = sum(jnp.where(i2 == j, ss[j], 0.0) for j in range(epg))
    tot = w1 + w2
    w_ref[0] = jnp.concatenate([w1 / tot, w2 / tot], axis=0)
    e1 = g_best * epg + i1
    e2 = g_best * epg + i2
    e_ref[0] = jnp.concatenate([e1, e2], axis=0)

    tm = x.shape[0]
    eidx = lax.broadcasted_iota(jnp.int32, (N_EXPERTS, tm), 0)
    oh1 = eidx == e1
    oh2 = eidx == e2
    oh = jnp.where(oh1 | oh2, 1.0, 0.0)
    before = lax.broadcasted_iota(jnp.int32, (tm, tm), 0) < lax.broadcasted_iota(jnp.int32, (tm, tm), 1)
    prefix = _dot(oh.astype(BF16), jnp.where(before, 1.0, 0.0).astype(BF16))
    base = carry_ref[...] + prefix
    r1 = jnp.sum(jnp.where(oh1, base, 0.0), axis=0, keepdims=True)
    r2 = jnp.sum(jnp.where(oh2, base, 0.0), axis=0, keepdims=True)
    rank_ref[0] = jnp.concatenate([r1, r2], axis=0).astype(jnp.int32)
    carry = carry_ref[...] + jnp.sum(oh, axis=1, keepdims=True)
    carry_ref[...] = carry
    cnt_ref[...] = jnp.broadcast_to(carry, cnt_ref.shape).astype(jnp.int32)


def _router(x, gain, sc, sh, wr_hi, wr_lo, b_router, counts0):
    b, n, d = x.shape
    tm = min(512, n)
    ne = wr_hi.shape[0]
    pair = pl.BlockSpec((1, 2, tm), lambda bb, i: (bb, 0, i))
    return pl.pallas_call(
        _router_kernel,
        grid=(b, n // tm),
        in_specs=[
            pl.BlockSpec((1, tm, d), lambda bb, i: (bb, i, 0)),
            pl.BlockSpec((1, d), lambda bb, i: (0, 0)),
            pl.BlockSpec((1, 1, d), lambda bb, i: (bb, 0, 0)),
            pl.BlockSpec((1, 1, d), lambda bb, i: (bb, 0, 0)),
            pl.BlockSpec((ne, d), lambda bb, i: (0, 0)),
            pl.BlockSpec((ne, d), lambda bb, i: (0, 0)),
            pl.BlockSpec((ne, 1), lambda bb, i: (0, 0)),
            pl.BlockSpec((ne, 1), lambda bb, i: (0, 0)),
        ],
        out_specs=[
            pl.BlockSpec((1, tm, d), lambda bb, i: (bb, i, 0)),
            pair, pair, pair,
            pl.BlockSpec((ne, HEAD_DIM), lambda bb, i: (0, 0)),
        ],
        out_shape=[
            jax.ShapeDtypeStruct((b, n, d), F32),
            jax.ShapeDtypeStruct((b, 2, n), jnp.int32),
            jax.ShapeDtypeStruct((b, 2, n), F32),
            jax.ShapeDtypeStruct((b, 2, n), jnp.int32),
            jax.ShapeDtypeStruct((ne, HEAD_DIM), jnp.int32),
        ],
        scratch_shapes=[pltpu.VMEM((ne, 1), F32)],
        compiler_params=_cparams(("arbitrary", "arbitrary")),
        name="norm_router",
    )(x, gain.reshape(1, d), sc, sh, wr_hi, wr_lo, b_router.reshape(ne, 1), counts0)


MOE_TILE = 256
MOE_SCATTER_TILE = 1024
MOE_COMBINE_TILE = 256
MOE_DMA_UNROLL = 8


def _moe_tiles(counts, nt):
    ntile_e = (counts + MOE_TILE - 1) // MOE_TILE
    tile_end = jnp.cumsum(ntile_e)
    off = (tile_end - ntile_e) * MOE_TILE
    tile_expert = jnp.sum(jnp.arange(nt)[:, None] >= tile_end[None, :], axis=1)
    tile_expert = jnp.minimum(tile_expert, N_EXPERTS - 1).astype(jnp.int32)
    return off, tile_expert, tile_end[-1:].astype(jnp.int32)


def _moe_rows(e, rank, off):
    b, _, n = e.shape
    t = b * n
    ef = e.transpose(1, 0, 2).reshape(2, t)
    rf = rank.transpose(1, 0, 2).reshape(2, t)
    pos = rf + jnp.sum(jnp.where(ef[..., None] == jnp.arange(N_EXPERTS), off, 0), axis=-1)
    return pos.reshape(2 * t).astype(jnp.int32)


def _scatter_kernel(pos_ref, h_ref, xs0_hbm, xs_hbm, sem, *, t):
    del xs0_hbm
    tm = h_ref.shape[0]
    base = pl.program_id(0) * tm

    def copies(j):
        return [pltpu.make_async_copy(h_ref.at[pl.ds(j, 1)], xs_hbm.at[pl.ds(pos_ref[k * t + base + j], 1)], sem)
                for k in range(2)]

    def start(j, c):
        for cp in copies(j):
            cp.start()
        return c

    def wait(j, c):
        for cp in copies(j):
            cp.wait()
        return c

    lax.fori_loop(0, tm, start, 0, unroll=MOE_DMA_UNROLL)
    lax.fori_loop(0, tm, wait, 0, unroll=MOE_DMA_UNROLL)


def _scatter_rows(pos, h, xs_buf):
    t, d = h.shape
    tm = min(MOE_SCATTER_TILE, t)
    any_spec = pl.BlockSpec(memory_space=pl.ANY)
    return pl.pallas_call(
        functools.partial(_scatter_kernel, t=t),
        grid_spec=pltpu.PrefetchScalarGridSpec(
            num_scalar_prefetch=1, grid=(t // tm,),
            in_specs=[pl.BlockSpec((tm, d), lambda i, p: (i, 0)), any_spec], out_specs=any_spec,
            scratch_shapes=[pltpu.SemaphoreType.DMA(())]),
        out_shape=jax.ShapeDtypeStruct(xs_buf.shape, F32),
        input_output_aliases={2: 0},
        compiler_params=pltpu.CompilerParams(dimension_semantics=("arbitrary",), has_side_effects=True,
                                             vmem_limit_bytes=V7X_VMEM_LIMIT),
        name="moe_scatter",
    )(pos, h, xs_buf)


def _experts_kernel(te_ref, nv_ref, xs_ref, wgu_ref, wd_ref, ys_ref):
    del te_ref

    @pl.when(pl.program_id(0) < nv_ref[0])
    def _():
        gu = _dot(xs_ref[...].astype(BF16), wgu_ref[0, 0])
        ff = gu.shape[1] // 2
        gate = gu[:, :ff]
        a = (gate * jax.nn.sigmoid(gate) * gu[:, ff:]).astype(BF16)
        ys_ref[...] = _dot(a, wd_ref[0, 0])


def _experts(xs, tile_expert, nvalid, w_gu, w_down, layer):
    nrows, d = xs.shape
    nt = nrows // MOE_TILE
    f2 = w_gu.shape[-1]
    row = lambda i, te, nv: (jnp.minimum(i, nv[0] - 1), 0)
    wmap = lambda i, te, nv: (layer, te[jnp.minimum(i, nv[0] - 1)], 0, 0)
    return pl.pallas_call(
        _experts_kernel,
        grid_spec=pltpu.PrefetchScalarGridSpec(
            num_scalar_prefetch=2, grid=(nt,),
            in_specs=[
                pl.BlockSpec((MOE_TILE, d), row),
                pl.BlockSpec((1, 1, d, f2), wmap),
                pl.BlockSpec((1, 1, f2 // 2, d), wmap),
            ],
            out_specs=pl.BlockSpec((MOE_TILE, d), row)),
        out_shape=jax.ShapeDtypeStruct((nrows, d), F32),
        compiler_params=_cparams(("arbitrary",)),
        name="moe_experts",
    )(tile_expert, nvalid, xs, w_gu, w_down)


def _combine_kernel(pos_ref, x_ref, w_ref, g2_ref, ys_hbm, o_ref, buf, sem, *, t):
    i = pl.program_id(0)
    nsteps = pl.num_programs(0)
    tm = x_ref.shape[0]

    def copies(tile, slot, j):
        tok = tile * tm + j
        return [pltpu.make_async_copy(ys_hbm.at[pl.ds(pos_ref[k * t + tok], 1)],
                                      buf.at[slot, k, pl.ds(j, 1)], sem.at[slot]) for k in range(2)]

    def issue(tile, slot):
        def body(j, c):
            for cp in copies(tile, slot, j):
                cp.start()
            return c
        lax.fori_loop(0, tm, body, 0, unroll=MOE_DMA_UNROLL)

    def wait(tile, slot):
        def body(j, c):
            for cp in copies(tile, slot, j):
                cp.wait()
            return c
        lax.fori_loop(0, tm, body, 0, unroll=MOE_DMA_UNROLL)

    @pl.when(i == 0)
    def _():
        issue(0, 0)

    @pl.when(i + 1 < nsteps)
    def _():
        issue(i + 1, (i + 1) % 2)

    slot = i % 2
    wait(i, slot)
    w = w_ref[...]
    y = w[:, 0:1] * buf[slot, 0] + w[:, 1:2] * buf[slot, 1]
    o_ref[...] = x_ref[...] + g2_ref[0] * y


def _combine(pos, x, w, g2, ys):
    b, n, d = x.shape
    t = b * n
    tm = min(MOE_COMBINE_TILE, n)
    per_b = n // tm
    out = pl.pallas_call(
        functools.partial(_combine_kernel, t=t),
        grid_spec=pltpu.PrefetchScalarGridSpec(
            num_scalar_prefetch=1, grid=(t // tm,),
            in_specs=[
                pl.BlockSpec((tm, d), lambda i, p: (i, 0)),
                pl.BlockSpec((tm, 2), lambda i, p: (i, 0)),
                pl.BlockSpec((1, 1, d), lambda i, p: (i // per_b, 0, 0)),
                pl.BlockSpec(memory_space=pl.ANY),
            ],
            out_specs=pl.BlockSpec((tm, d), lambda i, p: (i, 0)),
            scratch_shapes=[pltpu.VMEM((2, 2, tm, d), F32), pltpu.SemaphoreType.DMA((2,))]),
        out_shape=jax.ShapeDtypeStruct((t, d), F32),
        compiler_params=_cparams(("arbitrary",)),
        name="moe_combine",
    )(pos, x.reshape(t, d), w, g2, ys)
    return out.reshape(b, n, d)


def _moe(token_sets, gain, wr_hi, wr_lo, b_router, w_gu, w_down, layer, xs_buf, nt):
    d = token_sets[0][0].shape[-1]
    counts = jnp.zeros((N_EXPERTS, 1), F32)
    routed = []
    for t, sc, sh, _ in token_sets:
        h, e, w, rank, cnt = _router(t, gain, sc, sh, wr_hi, wr_lo, b_router, counts)
        counts = cnt[:, :1].astype(F32)
        routed.append((h, e, w, rank))
    off, tile_expert, nvalid = _moe_tiles(cnt[:, 0], nt)
    xs = jnp.zeros((nt * MOE_TILE, d), F32) if xs_buf is None else xs_buf
    rows = []
    for h, e, w, rank in routed:
        rows.append(_moe_rows(e, rank, off))
        xs = _scatter_rows(rows[-1], h.reshape(-1, d), xs)
    ys = _experts(xs, tile_expert, nvalid, w_gu, w_down, layer)
    outs = [_combine(pos, t, w.transpose(0, 2, 1).reshape(-1, 2), g2, ys)
            for pos, (t, _, _, g2), (_, _, w, _) in zip(rows, token_sets, routed)]
    return outs, xs


def _rope_tables(n):
    t = jnp.arange(n, dtype=jnp.int32)
    row = (t // GRID_W).astype(F32)
    col = (t % GRID_W).astype(F32)
    axis_dim = HEAD_DIM // 2
    inv = ROPE_THETA ** (-jnp.arange(0, axis_dim, 2, dtype=F32) / axis_dim)
    ang = jnp.concatenate([row[:, None] * inv, col[:, None] * inv], axis=-1)
    cos, sin = jnp.cos(ang), jnp.sin(ang)
    return jnp.concatenate([cos, cos], axis=-1), jnp.concatenate([-sin, sin], axis=-1)


def kernel(x, c, ctx, c_ctx, w_mod, b_mod, norm1, norm2, w_in, qk_gain, pool_w, pool_scale,
           na_rpb, w_br, w_out, w_router, b_router, w_gu, w_down):
    b, n, d = x.shape
    depth = w_mod.shape[0]
    rows = n // GRID_W
    kr = min(NA_ROWS, rows)
    assert n % GRID_W == 0 and rows % kr == 0 and b + 1 <= 8

    cos, sin = _rope_tables(n)
    mods = _modulation(jnp.concatenate([c, c_ctx[None, :]], axis=0), w_mod, b_mod)
    wr_t = w_router.T
    wr_hi = wr_t.astype(BF16)
    wr_lo = (wr_t - wr_hi.astype(F32)).astype(BF16)
    w_br_b, w_out_b = w_br.astype(BF16), w_out.astype(BF16)
    w_gu_b, w_down_b = w_gu.astype(BF16), w_down.astype(BF16)
    xs_buf = None
    moe_tiles = 2 * b * (n + ctx.shape[1]) // MOE_TILE + N_EXPERTS

    for l in range(depth):
        last = l == depth - 1
        mx = mods[l, :b].reshape(b, 1, N_MOD, d)
        mc = jnp.broadcast_to(mods[l, b].reshape(1, 1, N_MOD, d), (b, 1, N_MOD, d))
        x_sh1, x_sc1, x_g1, x_sh2, x_sc2, x_g2 = [mx[:, :, k] for k in range(N_MOD)]
        c_sh1, c_sc1, c_g1, c_sh2, c_sc2, c_g2 = [mc[:, :, k] for k in range(N_MOD)]
        pool_w_l = pool_w[l].astype(BF16)

        def proj(t, sc, sh, kind, tables=(None, None)):
            return _inproj(t, norm1[l], sc, sh, w_in, l, qk_gain[l], *tables, kind=kind)

        pckv = proj(ctx, c_sc1, c_sh1, "kv")
        pq = proj(x, x_sc1, x_sh1, "q", (cos, sin))
        pkv = proj(x, x_sc1, x_sh1, "kv", (cos, sin))
        pg = proj(x, x_sc1, x_sh1, "gates")
        y_pool = _pool(pq, pool_w_l, pool_scale[l])
        y_na = _na(pq, pkv, pckv, na_rpb[l])
        y_gqa = _gqa(pq, pkv, pckv, qk_gain[l, 2], qk_gain[l, 3])
        x = _merge(x, pg, y_pool, y_na, 0, y_gqa, 0, w_br_b, w_out_b, l, x_g1)

        token_sets = [(x, x_sc2, x_sh2, x_g2)]
        if not last:
            pcq = proj(ctx, c_sc1, c_sh1, "q")
            pcg = proj(ctx, c_sc1, c_sh1, "gates")
            yc_pool = _pool(pcq, pool_w_l, pool_scale[l])
            yc = _ctx_attn(pcq, pckv)
            ctx = _merge(ctx, pcg, yc_pool, yc, 0, yc, 1, w_br_b, w_out_b, l, c_g1)
            token_sets.append((ctx, c_sc2, c_sh2, c_g2))
        outs, xs_buf = _moe(token_sets, norm2[l], wr_hi, wr_lo, b_router, w_gu_b, w_down_b, l,
                            xs_buf, moe_tiles)
        x = outs[0]
        if not last:
            ctx = outs[1]
    return x
```

```python
import functools

import numpy as np
import jax
import jax.numpy as jnp
from jax import lax
from jax.experimental import pallas as pl
from jax.experimental.pallas import tpu as pltpu

F32 = jnp.float32
BF16 = jnp.bfloat16

GRID_W = 64
HEAD_DIM = 128
ROPE_THETA = 10000.0
EPS = 1e-6
POOL_WINDOWS = (2, 4, 8, 16)
POOL_CH = 128
POOL_WIDTH = len(POOL_WINDOWS) * POOL_CH
NA_HEADS = 6
NA_WIDTH = NA_HEADS * HEAD_DIM
NA_ROWS = 8
NA_COLS = 16
GQA_Q_HEADS = 6
GQA_KV_HEADS = 2
GQA_GROUP = GQA_Q_HEADS // GQA_KV_HEADS
GQA_Q_WIDTH = GQA_Q_HEADS * HEAD_DIM
N_BRANCH = 3
N_EXPERTS = 16
N_GROUPS = 4
EXPERTS_PER_GROUP = N_EXPERTS // N_GROUPS
N_MOD = 6
ATTN_SCALE = HEAD_DIM ** -0.5

COL_BLOCK = 2048
CB_POOL = 0
HB_NA_Q = POOL_WIDTH // HEAD_DIM
HB_GQA_Q = HB_NA_Q + NA_HEADS
HB_NA_K = 0
HB_NA_V = HB_NA_K + NA_HEADS
HB_GQA_K = HB_NA_V + NA_HEADS
HB_GQA_V = HB_GQA_K + GQA_KV_HEADS
PROJ_KINDS = {"gates": (0, 3), "q": (3, 1), "kv": (4, 1)}

V7X_VMEM_LIMIT = 56 * 1024 * 1024
NEG_BIG = -1e30


def _cparams(sem):
    return pltpu.CompilerParams(dimension_semantics=sem, vmem_limit_bytes=V7X_VMEM_LIMIT)


def _dot(a, b):
    return jnp.dot(a, b, preferred_element_type=F32)


def _dot_nt(a, b):
    return lax.dot_general(a, b, (((1,), (1,)), ((), ())), preferred_element_type=F32)


def _mod_kernel(ct_ref, w_ref, b_ref, o_ref, *, n_rows):
    ct = ct_ref[...]
    a = ct * jax.nn.sigmoid(ct)
    w = w_ref[0]
    rows = [jnp.sum(w * a[:, r:r + 1], axis=0, keepdims=True) for r in range(n_rows)]
    rows += [jnp.zeros_like(rows[0])] * (8 - n_rows)
    o_ref[0] = jnp.concatenate(rows, axis=0) + b_ref[0]


def _modulation(c_rows, w_mod, b_mod):
    depth, d, nm = w_mod.shape
    n_rows = c_rows.shape[0]
    ct = jnp.zeros((d, 8), F32).at[:, :n_rows].set(c_rows.T)
    tn = 1024
    return pl.pallas_call(
        functools.partial(_mod_kernel, n_rows=n_rows),
        grid=(depth, nm // tn),
        in_specs=[
            pl.BlockSpec((d, 8), lambda l, j: (0, 0)),
            pl.BlockSpec((1, d, tn), lambda l, j: (l, 0, j)),
            pl.BlockSpec((1, 1, tn), lambda l, j: (l, 0, j)),
        ],
        out_specs=pl.BlockSpec((1, 8, tn), lambda l, j: (l, 0, j)),
        out_shape=jax.ShapeDtypeStruct((depth, 8, nm), F32),
        compiler_params=_cparams(("arbitrary", "arbitrary")),
        name="adaln_mod",
    )(ct, w_mod, b_mod.reshape(depth, 1, nm))


def _head_norm(a, gain):
    return a * lax.rsqrt(jnp.mean(a * a, axis=-1, keepdims=True) + EPS) * gain


def _rope(y, cos, sin):
    return y * cos + pltpu.roll(y, HEAD_DIM // 2, 1) * sin


PROJ_CHUNK = 2 * HEAD_DIM
PROJ_ROW_SPLIT = 4
PROJ_ROW_TILE = 1024
PROJ_MIN_SUB_ROWS = 256


def _proj_epilogue(kind, head, a, qg_ref, rope_tables):
    if kind == "gates":
        return jax.nn.sigmoid(a)
    if kind == "q":
        if head < HB_NA_Q:
            return a
        if head < HB_GQA_Q:
            return _head_norm(a, qg_ref[0:1, :]) * ATTN_SCALE
        y = _head_norm(a, qg_ref[2:3, :])
        return (_rope(y, *rope_tables) if rope_tables else y) * ATTN_SCALE
    if head < HB_NA_V:
        return _head_norm(a, qg_ref[1:2, :])
    if HB_GQA_K <= head < HB_GQA_V:
        y = _head_norm(a, qg_ref[3:4, :])
        return _rope(y, *rope_tables) if rope_tables else y
    return a


def _inproj_kernel(*refs, kind, rope):
    if rope:
        x_ref, g_ref, sc_ref, sh_ref, w_ref, qg_ref, cos_ref, sin_ref, o_ref, wb_s = refs
    else:
        x_ref, g_ref, sc_ref, sh_ref, w_ref, qg_ref, o_ref, wb_s = refs

    @pl.when((pl.program_id(1) == 0) & (pl.program_id(2) == 0))
    def _():
        wb_s[...] = w_ref[0].astype(BF16)

    tm = x_ref.shape[1]
    nsub = PROJ_ROW_SPLIT if tm % (PROJ_ROW_SPLIT * PROJ_MIN_SUB_ROWS) == 0 else 1
    ts = tm // nsub
    per = PROJ_CHUNK // HEAD_DIM
    for r in range(nsub):
        rows = slice(r * ts, (r + 1) * ts)
        x = x_ref[0, rows, :]
        h = x * lax.rsqrt(jnp.mean(x * x, axis=-1, keepdims=True) + EPS) * g_ref[...]
        hb = (h * (1.0 + sc_ref[0]) + sh_ref[0]).astype(BF16)
        tables = (cos_ref[rows, :], sin_ref[rows, :]) if rope else None
        for c in range(COL_BLOCK // PROJ_CHUNK):
            acc = _dot(hb, wb_s[:, c * PROJ_CHUNK:(c + 1) * PROJ_CHUNK])
            for u in range(per):
                head = c * per + u
                y = _proj_epilogue(kind, head, acc[:, u * HEAD_DIM:(u + 1) * HEAD_DIM], qg_ref, tables)
                o_ref[0, rows, head * HEAD_DIM:(head + 1) * HEAD_DIM] = y.astype(BF16)


def _inproj(x, gain, sc, sh, w_in, layer, qgain, cos, sin, *, kind):
    b, n, d = x.shape
    rope = cos is not None and kind != "gates"
    tm = min(PROJ_ROW_TILE, n)
    j0, nj = PROJ_KINDS[kind]
    in_specs = [
        pl.BlockSpec((1, tm, d), lambda j, bb, i: (bb, i, 0)),
        pl.BlockSpec((1, d), lambda j, bb, i: (0, 0)),
        pl.BlockSpec((1, 1, d), lambda j, bb, i: (bb, 0, 0)),
        pl.BlockSpec((1, 1, d), lambda j, bb, i: (bb, 0, 0)),
        pl.BlockSpec((1, d, COL_BLOCK), lambda j, bb, i: (layer, 0, j + j0), pipeline_mode=pl.Buffered(1)),
        pl.BlockSpec((4, HEAD_DIM), lambda j, bb, i: (0, 0)),
    ]
    args = [x, gain.reshape(1, d), sc, sh, w_in, qgain]
    if rope:
        in_specs += [pl.BlockSpec((tm, HEAD_DIM), lambda j, bb, i: (i, 0))] * 2
        args += [cos, sin]
    return pl.pallas_call(
        functools.partial(_inproj_kernel, kind=kind, rope=rope),
        grid=(nj, b, n // tm),
        in_specs=in_specs,
        out_specs=pl.BlockSpec((1, tm, COL_BLOCK), lambda j, bb, i: (bb, i, j)),
        out_shape=jax.ShapeDtypeStruct((b, n, nj * COL_BLOCK), BF16),
        scratch_shapes=[pltpu.VMEM((d, COL_BLOCK), BF16)],
        compiler_params=_cparams(("arbitrary", "arbitrary", "arbitrary")),
        name="inproj_" + kind + ("_rope" if rope else ""),
    )(*args)


POOL_HALO = 16


def _pool_kernel(prev_ref, cur_ref, next_ref, w_ref, s_ref, o_ref, buf_ref, *, tm, n):
    i = pl.program_id(1)
    nt = pl.num_programs(1)
    hl = POOL_HALO
    buf_ref[pl.ds(hl, tm), :] = cur_ref[0].astype(F32)
    buf_ref[pl.ds(0, hl), :] = jnp.where(i > 0, prev_ref[0].astype(F32), 0.0)
    buf_ref[pl.ds(hl + tm, hl), :] = jnp.where(i < nt - 1, next_ref[0].astype(F32), 0.0)
    t = i * tm + lax.broadcasted_iota(jnp.int32, (tm, 1), 0)
    for g, w in enumerate(POOL_WINDOWS):
        sl = slice(g * POOL_CH, (g + 1) * POOL_CH)
        acc = buf_ref[pl.ds(hl - w // 2, tm), sl]
        for off in range(-w // 2 + 1, w // 2):
            acc = acc + buf_ref[pl.ds(hl + off, tm), sl]
        cnt = (jnp.minimum(t + w // 2, n) - jnp.maximum(t - w // 2, 0)).astype(F32)
        dlt = acc / cnt - buf_ref[pl.ds(hl, tm), sl]
        y = _dot(dlt.astype(BF16), w_ref[g]) * s_ref[:, sl]
        o_ref[0, :, sl] = y.astype(BF16)


def _pool(p, pool_w, pool_scale):
    b, n, _ = p.shape
    tm = min(512, n)
    hl = POOL_HALO
    hb = tm // hl
    last = n // hl - 1
    return pl.pallas_call(
        functools.partial(_pool_kernel, tm=tm, n=n),
        grid=(b, n // tm),
        in_specs=[
            pl.BlockSpec((1, hl, POOL_WIDTH), lambda bb, i: (bb, jnp.maximum(i * hb - 1, 0), CB_POOL)),
            pl.BlockSpec((1, tm, POOL_WIDTH), lambda bb, i: (bb, i, CB_POOL)),
            pl.BlockSpec((1, hl, POOL_WIDTH), lambda bb, i: (bb, jnp.minimum((i + 1) * hb, last), CB_POOL)),
            pl.BlockSpec((len(POOL_WINDOWS), POOL_CH, POOL_CH), lambda bb, i: (0, 0, 0)),
            pl.BlockSpec((1, POOL_WIDTH), lambda bb, i: (0, 0)),
        ],
        out_specs=pl.BlockSpec((1, tm, POOL_WIDTH), lambda bb, i: (bb, i, 0)),
        out_shape=jax.ShapeDtypeStruct((b, n, POOL_WIDTH), BF16),
        scratch_shapes=[pltpu.VMEM((tm + 2 * hl, POOL_WIDTH), F32)],
        compiler_params=_cparams(("arbitrary", "arbitrary")),
        name="pool_mixer",
    )(p, p, p, pool_w, pool_scale.reshape(1, POOL_WIDTH))


NA_QROWS = 4
NA_UROWS = 12
NA_BLOCKS_PER_STEP = 8


def _na_plan(rows, kr):
    rq, ku = NA_QROWS, NA_UROWS
    if rows < ku or rows % rq:
        rq, ku = 1, kr
    starts, keys = [], []
    for r0 in range(0, rows, rq):
        rs = [min(max(r - kr // 2, 0), rows - kr) for r in range(r0, r0 + rq)]
        us = min(rs[0], rows - ku)
        starts.append(us)
        keys.append(tuple((us - r, rs_q - us) for r, rs_q in zip(range(r0, r0 + rq), rs)))
    tables = sorted(set(keys))
    table_of = np.array([tables.index(k) for k in keys], np.int32)
    return rq, ku, np.array(starts, np.int32), table_of, tables


def _na_bias(rpb, kr, ku, tables):
    col = np.arange(GRID_W)
    col_start = np.clip(col - NA_COLS // 2, 0, GRID_W - NA_COLS)
    dcol = col[None, :] - col[:, None] + (NA_COLS - 1)
    ok = (col[None, :] >= col_start[:, None]) & (col[None, :] < col_start[:, None] + NA_COLS)
    onehot = ((dcol[:, :, None] == np.arange(2 * NA_COLS - 1)) & ok[:, :, None]).astype(np.float32)
    nt, rq = len(tables), len(tables[0])
    rowsel = np.zeros((nt, rq, ku, 2 * NA_ROWS - 1), np.float32)
    for t, key in enumerate(tables):
        for q, (rel0, first) in enumerate(key):
            for i in range(first, first + kr):
                rowsel[t, q, i, rel0 + i + NA_ROWS - 1] = 1.0
    keep = (rowsel.sum(-1) > 0)[:, None, :, None, :, None] & ok[None, None, None, :, None, :]
    bias = jnp.einsum("hrc,tqir,xyc->thqxiy", rpb.astype(F32), rowsel, onehot,
                      precision=lax.Precision.HIGHEST)
    bias = jnp.where(keep, bias, NEG_BIG)
    return bias.reshape(nt, rpb.shape[0], rq * GRID_W, ku * GRID_W)


def _na_kernel(us_ref, tb_ref, q_ref, k_ref, v_ref, kc_ref, vc_ref, bias_ref, o_ref, sl_s, sc_s,
               *, nb, mq, mk):
    i = pl.program_id(2)
    kc = kc_ref[0]
    vc = vc_ref[0]

    def koff(bb):
        return pl.multiple_of(us_ref[i * nb + bb] * GRID_W, GRID_W)

    def scores(bb, slot):
        q = q_ref[0, bb * mq:(bb + 1) * mq, :]
        sl_s[slot] = _dot_nt(q, k_ref[0, pl.ds(koff(bb), mk), :]) + bias_ref[tb_ref[i * nb + bb], 0]
        sc_s[slot] = _dot_nt(q, kc)

    def finish(bb, slot):
        s_loc = sl_s[slot]
        s_ctx = sc_s[slot]
        m = jnp.maximum(jnp.max(s_loc, axis=-1, keepdims=True), jnp.max(s_ctx, axis=-1, keepdims=True))
        p_loc = jnp.exp(s_loc - m)
        p_ctx = jnp.exp(s_ctx - m)
        den = jnp.sum(p_loc, axis=-1, keepdims=True) + jnp.sum(p_ctx, axis=-1, keepdims=True)
        o = _dot(p_loc.astype(BF16), v_ref[0, pl.ds(koff(bb), mk), :]) + _dot(p_ctx.astype(BF16), vc)
        o_ref[0, bb * mq:(bb + 1) * mq, :] = (o / den).astype(BF16)

    scores(0, 0)
    for bb in range(nb):
        if bb + 1 < nb:
            scores(bb + 1, (bb + 1) % 2)
        finish(bb, bb % 2)


def _na(pq, pkv, pckv, rpb):
    b, n, _ = pq.shape
    nc = pckv.shape[1]
    rows = n // GRID_W
    kr = min(NA_ROWS, rows)
    rq, ku, starts, table_of, tables = _na_plan(rows, kr)
    bias = _na_bias(rpb, kr, ku, tables)
    nblk = rows // rq
    nb = min(NA_BLOCKS_PER_STEP, nblk)
    assert nblk % nb == 0
    mq, mk = rq * GRID_W, ku * GRID_W
    return pl.pallas_call(
        functools.partial(_na_kernel, nb=nb, mq=mq, mk=mk),
        grid_spec=pltpu.PrefetchScalarGridSpec(
            num_scalar_prefetch=2, grid=(b, NA_HEADS, nblk // nb),
            in_specs=[
                pl.BlockSpec((1, nb * mq, HEAD_DIM), lambda bb, h, i, us, tb: (bb, i, HB_NA_Q + h)),
                pl.BlockSpec((1, n, HEAD_DIM), lambda bb, h, i, us, tb: (bb, 0, HB_NA_K + h)),
                pl.BlockSpec((1, n, HEAD_DIM), lambda bb, h, i, us, tb: (bb, 0, HB_NA_V + h)),
                pl.BlockSpec((1, nc, HEAD_DIM), lambda bb, h, i, us, tb: (bb, 0, HB_NA_K + h)),
                pl.BlockSpec((1, nc, HEAD_DIM), lambda bb, h, i, us, tb: (bb, 0, HB_NA_V + h)),
                pl.BlockSpec((len(tables), 1, mq, mk), lambda bb, h, i, us, tb: (0, h, 0, 0)),
            ],
            out_specs=pl.BlockSpec((1, nb * mq, HEAD_DIM), lambda bb, h, i, us, tb: (bb, i, h)),
            scratch_shapes=[pltpu.VMEM((2, mq, mk), F32), pltpu.VMEM((2, mq, nc), F32)]),
        out_shape=jax.ShapeDtypeStruct((b, n, NA_WIDTH), BF16),
        compiler_params=_cparams(("arbitrary", "arbitrary", "arbitrary")),
        name="na_attn",
    )(jnp.asarray(starts), jnp.asarray(table_of), pq, pkv, pkv, pckv, pckv, bias)


GQA_CHUNKS_PER_TRIP = 4
GQA_Q_TILE = 512


def _gqa_kernel(bound_ref, q0_ref, q1_ref, q2_ref, k_ref, v_ref, kc_ref, vc_ref, o_ref,
                qt_s, vt_s, vct_s, s_s, sc_s, m_s, den_s, acc_s, *, tk, n, fixed_shift):
    i = pl.program_id(2)
    tq = q0_ref.shape[1]
    nchunk = n // tk
    hd = HEAD_DIM

    def to_t(a):
        return a.astype(F32).T.astype(BF16)

    @pl.when(i == 0)
    def _():
        def tr(c, carry):
            off = pl.multiple_of(c * tk, tk)
            vt_s[c] = to_t(v_ref[0, pl.ds(off, tk), :])
            return carry
        lax.fori_loop(0, nchunk, tr, 0)
        vct_s[...] = to_t(vc_ref[0])

    for g, qr in enumerate((q0_ref, q1_ref, q2_ref)):
        qt_s[:, g * tq:(g + 1) * tq] = to_t(qr[0])

    def scores(c, slot):
        off = pl.multiple_of(c * tk, tk)
        s_s[slot] = _dot(k_ref[0, pl.ds(off, tk), :], qt_s[...])

    def col_sums(p):
        return jnp.sum(p.reshape(p.shape[0] // 8, 8, p.shape[1]), axis=0)

    def update(s, vt, first=False):
        if fixed_shift:
            p = jnp.exp(s - bound_ref[0])
            pv = _dot(vt, p.astype(BF16))
            den_s[...] = col_sums(p) if first else den_s[...] + col_sums(p)
            acc_s[...] = pv if first else acc_s[...] + pv
            return
        smax = jnp.max(s, axis=0, keepdims=True)
        if first:
            m_new = smax
        else:
            m_prev = m_s[...]
            m_new = jnp.maximum(m_prev, smax)
            alpha = jnp.exp(m_prev - m_new)
        p = jnp.exp(s - m_new)
        pv = _dot(vt, p.astype(BF16))
        den_s[...] = col_sums(p) if first else alpha * den_s[...] + col_sums(p)
        acc_s[...] = pv if first else alpha * acc_s[...] + pv
        m_s[...] = m_new

    scores(0, 0)
    sc_s[...] = _dot(kc_ref[0], qt_s[...])
    update(sc_s[...], vct_s[...], first=True)

    per_trip = GQA_CHUNKS_PER_TRIP if nchunk % GQA_CHUNKS_PER_TRIP == 0 else 2

    def body(ct, carry):
        c = per_trip * ct
        for u in range(per_trip):
            scores(jnp.minimum(c + u + 1, nchunk - 1), (u + 1) % 2)
            update(s_s[u % 2], vt_s[c + u])
        return carry

    lax.fori_loop(0, nchunk // per_trip, body, 0)
    o = (acc_s[...] / jnp.sum(den_s[...], axis=0, keepdims=True)).T
    for g in range(GQA_GROUP):
        o_ref[0, :, g * hd:(g + 1) * hd] = o[g * tq:(g + 1) * tq].astype(BF16)


GQA_FIXED_SHIFT_LIMIT = 40.0


def _gqa(pq, pkv, pckv, gain_q, gain_k):
    bound = 1.01 * HEAD_DIM * ATTN_SCALE * jnp.max(jnp.abs(gain_q)) * jnp.max(jnp.abs(gain_k))
    bound = bound.astype(F32).reshape(1)
    return lax.cond(bound[0] <= GQA_FIXED_SHIFT_LIMIT,
                    functools.partial(_gqa_call, fixed_shift=True),
                    functools.partial(_gqa_call, fixed_shift=False),
                    pq, pkv, pckv, bound)


def _gqa_call(pq, pkv, pckv, bound, *, fixed_shift):
    b, n, _ = pq.shape
    nc = pckv.shape[1]
    tq = min(GQA_Q_TILE, n)
    tk = min(512, n // 2)
    assert n % (2 * tk) == 0
    nq = GQA_GROUP * tq

    def qspec(g):
        return pl.BlockSpec((1, tq, HEAD_DIM), lambda bb, h, i: (bb, i, HB_GQA_Q + h * GQA_GROUP + g))

    return pl.pallas_call(
        functools.partial(_gqa_kernel, tk=tk, n=n, fixed_shift=fixed_shift),
        grid=(b, GQA_KV_HEADS, n // tq),
        in_specs=[
            pl.BlockSpec(memory_space=pltpu.SMEM),
            qspec(0), qspec(1), qspec(2),
            pl.BlockSpec((1, n, HEAD_DIM), lambda bb, h, i: (bb, 0, HB_GQA_K + h)),
            pl.BlockSpec((1, n, HEAD_DIM), lambda bb, h, i: (bb, 0, HB_GQA_V + h)),
            pl.BlockSpec((1, nc, HEAD_DIM), lambda bb, h, i: (bb, 0, HB_GQA_K + h)),
            pl.BlockSpec((1, nc, HEAD_DIM), lambda bb, h, i: (bb, 0, HB_GQA_V + h)),
        ],
        out_specs=pl.BlockSpec((1, tq, GQA_GROUP * HEAD_DIM), lambda bb, h, i: (bb, i, h)),
        out_shape=jax.ShapeDtypeStruct((b, n, GQA_Q_WIDTH), BF16),
        scratch_shapes=[
            pltpu.VMEM((HEAD_DIM, nq), BF16),
            pltpu.VMEM((n // tk, HEAD_DIM, tk), BF16),
            pltpu.VMEM((HEAD_DIM, nc), BF16),
            pltpu.VMEM((2, tk, nq), F32),
            pltpu.VMEM((nc, nq), F32),
            pltpu.VMEM((1, nq), F32),
            pltpu.VMEM((8, nq), F32),
            pltpu.VMEM((HEAD_DIM, nq), F32),
        ],
        compiler_params=_cparams(("arbitrary", "arbitrary", "arbitrary")),
        name="gqa_attn_fixed_shift" if fixed_shift else "gqa_attn_running_max",
    )(bound, pq, pq, pq, pkv, pkv, pckv, pckv)


def _ctx_attn_kernel(q_ref, k_ref, v_ref, o_ref):
    s = _dot_nt(q_ref[0], k_ref[0])
    m = jnp.max(s, axis=-1, keepdims=True)
    p = jnp.exp(s - m)
    den = jnp.sum(p, axis=-1, keepdims=True)
    o_ref[0] = (_dot(p.astype(BF16), v_ref[0]) / den).astype(BF16)


def _ctx_attn(pcq, pckv):
    b, nc, _ = pcq.shape
    nh = NA_HEADS + GQA_Q_HEADS

    def kmap(bb, h):
        g = jnp.maximum(h - NA_HEADS, 0) // GQA_GROUP
        return bb, 0, jnp.where(h < NA_HEADS, HB_NA_K + h, HB_GQA_K + g)

    def vmap_(bb, h):
        g = jnp.maximum(h - NA_HEADS, 0) // GQA_GROUP
        return bb, 0, jnp.where(h < NA_HEADS, HB_NA_V + h, HB_GQA_V + g)

    return pl.pallas_call(
        _ctx_attn_kernel,
        grid=(b, nh),
        in_specs=[
            pl.BlockSpec((1, nc, HEAD_DIM), lambda bb, h: (bb, 0, HB_NA_Q + h)),
            pl.BlockSpec((1, nc, HEAD_DIM), kmap),
            pl.BlockSpec((1, nc, HEAD_DIM), vmap_),
        ],
        out_specs=pl.BlockSpec((1, nc, HEAD_DIM), lambda bb, h: (bb, 0, h)),
        out_shape=jax.ShapeDtypeStruct((b, nc, nh * HEAD_DIM), BF16),
        compiler_params=_cparams(("arbitrary", "arbitrary")),
        name="ctx_attn",
    )(pcq, pckv, pckv)


def _merge_kernel(x_ref, ga_ref, gb_ref, gc_ref, yp_ref, yn_ref, yg_ref, wbr_ref, wout_ref, g1_ref, o_ref):
    r1 = POOL_WIDTH
    r2 = POOL_WIDTH + NA_WIDTH
    z = ga_ref[0].astype(F32) * _dot(yp_ref[0], wbr_ref[0, 0:r1, :])
    z = z + gb_ref[0].astype(F32) * _dot(yn_ref[0], wbr_ref[0, r1:r2, :])
    z = z + gc_ref[0].astype(F32) * _dot(yg_ref[0], wbr_ref[0, r2:, :])
    o_ref[0] = x_ref[0] + g1_ref[0] * _dot(z.astype(BF16), wout_ref[0])


def _merge(x, p, y_pool, y_na, na_cb, y_gqa, gqa_cb, w_br, w_out, layer, g1):
    b, n, d = x.shape
    tm = min(256, n)
    const = lambda bb, i: (layer, 0, 0)
    return pl.pallas_call(
        _merge_kernel,
        grid=(b, n // tm),
        in_specs=[
            pl.BlockSpec((1, tm, d), lambda bb, i: (bb, i, 0)),
            pl.BlockSpec((1, tm, d), lambda bb, i: (bb, i, 0)),
            pl.BlockSpec((1, tm, d), lambda bb, i: (bb, i, 1)),
            pl.BlockSpec((1, tm, d), lambda bb, i: (bb, i, 2)),
            pl.BlockSpec((1, tm, POOL_WIDTH), lambda bb, i: (bb, i, 0)),
            pl.BlockSpec((1, tm, NA_WIDTH), lambda bb, i: (bb, i, na_cb)),
            pl.BlockSpec((1, tm, GQA_Q_WIDTH), lambda bb, i: (bb, i, gqa_cb)),
            pl.BlockSpec((1,) + w_br.shape[1:], const, pipeline_mode=pl.Buffered(1)),
            pl.BlockSpec((1,) + w_out.shape[1:], const, pipeline_mode=pl.Buffered(1)),
            pl.BlockSpec((1, 1, d), lambda bb, i: (bb, 0, 0)),
        ],
        out_specs=pl.BlockSpec((1, tm, d), lambda bb, i: (bb, i, 0)),
        out_shape=jax.ShapeDtypeStruct((b, n, d), F32),
        compiler_params=_cparams(("arbitrary", "arbitrary")),
        name="branch_merge",
    )(x, p, p, p, y_pool, y_na, y_gqa, w_br, w_out, g1)


def _top2_of4(a, b, c, d):
    hi1, lo1 = jnp.maximum(a, b), jnp.minimum(a, b)
    hi2, lo2 = jnp.maximum(c, d), jnp.minimum(c, d)
    return jnp.maximum(hi1, hi2) + jnp.maximum(jnp.minimum(hi1, hi2), jnp.maximum(lo1, lo2))


def _router_kernel(x_ref, g_ref, sc_ref, sh_ref, whi_ref, wlo_ref, br_ref, c0_ref,
                   h_ref, e_ref, w_ref, rank_ref, cnt_ref, carry_ref):
    first = (pl.program_id(0) == 0) & (pl.program_id(1) == 0)

    @pl.when(first)
    def _():
        carry_ref[...] = c0_ref[...]

    x = x_ref[0]
    h = x * lax.rsqrt(jnp.mean(x * x, axis=-1, keepdims=True) + EPS) * g_ref[...]
    h = h * (1.0 + sc_ref[0]) + sh_ref[0]
    h_ref[0] = h
    h_hi = h.astype(BF16)
    h_lo = (h - h_hi.astype(F32)).astype(BF16)
    whi = whi_ref[...]
    logit = _dot_nt(whi, h_hi) + _dot_nt(whi, h_lo) + _dot_nt(wlo_ref[...], h_hi)
    s = jax.nn.sigmoid(logit)
    sel = s + br_ref[...]
    epg = EXPERTS_PER_GROUP
    row = lambda a, e: a[e:e + 1, :]
    gscore = [_top2_of4(*[row(sel, g * epg + j) for j in range(epg)]) for g in range(N_GROUPS)]
    g_best = jnp.zeros_like(gscore[0], dtype=jnp.int32)
    best = gscore[0]
    for g in range(1, N_GROUPS):
        upd = gscore[g] > best
        g_best = jnp.where(upd, g, g_best)
        best = jnp.where(upd, gscore[g], best)
    vs, ss = [], []
    for j in range(epg):
        v = row(sel, j)
        sv = row(s, j)
        for g in range(1, N_GROUPS):
            v = jnp.where(g_best == g, row(sel, g * epg + j), v)
            sv = jnp.where(g_best == g, row(s, g * epg + j), sv)
        vs.append(v)
        ss.append(sv)
    i1 = jnp.zeros_like(g_best)
    v1 = vs[0]
    for j in range(1, epg):
        upd = vs[j] > v1
        i1 = jnp.where(upd, j, i1)
        v1 = jnp.where(upd, vs[j], v1)
    i2 = jnp.full_like(g_best, -1)
    v2 = jnp.full_like(v1, -jnp.inf)
    for j in range(epg):
        upd = (i1 != j) & ((i2 < 0) | (vs[j] > v2))
        i2 = jnp.where(upd, j, i2)
        v2 = jnp.where(upd, vs[j], v2)
    w1 = sum(jnp.where(i1 == j, ss[j], 0.0) for j in range(epg))
    w2 = sum(jnp.where(i2 == j, ss[j], 0.0) for j in range(epg))
    tot = w1 + w2
    w_ref[0] = jnp.concatenate([w1 / tot, w2 / tot], axis=0)
    e1 = g_best * epg + i1
    e2 = g_best * epg + i2
    e_ref[0] = jnp.concatenate([e1, e2], axis=0)

    tm = x.shape[0]
    eidx = lax.broadcasted_iota(jnp.int32, (N_EXPERTS, tm), 0)
    oh1 = eidx == e1
    oh2 = eidx == e2
    oh = jnp.where(oh1 | oh2, 1.0, 0.0)
    before = lax.broadcasted_iota(jnp.int32, (tm, tm), 0) < lax.broadcasted_iota(jnp.int32, (tm, tm), 1)
    prefix = _dot(oh.astype(BF16), jnp.where(before, 1.0, 0.0).astype(BF16))
    base = carry_ref[...] + prefix
    r1 = jnp.sum(jnp.where(oh1, base, 0.0), axis=0, keepdims=True)
    r2 = jnp.sum(jnp.where(oh2, base, 0.0), axis=0, keepdims=True)
    rank_ref[0] = jnp.concatenate([r1, r2], axis=0).astype(jnp.int32)
    carry = carry_ref[...] + jnp.sum(oh, axis=1, keepdims=True)
    carry_ref[...] = carry
    cnt_ref[...] = jnp.broadcast_to(carry, cnt_ref.shape).astype(jnp.int32)


def _router(x, gain, sc, sh, wr_hi, wr_lo, b_router, counts0):
    b, n, d = x.shape
    tm = min(512, n)
    ne = wr_hi.shape[0]
    pair = pl.BlockSpec((1, 2, tm), lambda bb, i: (bb, 0, i))
    return pl.pallas_call(
        _router_kernel,
        grid=(b, n // tm),
        in_specs=[
            pl.BlockSpec((1, tm, d), lambda bb, i: (bb, i, 0)),
            pl.BlockSpec((1, d), lambda bb, i: (0, 0)),
            pl.BlockSpec((1, 1, d), lambda bb, i: (bb, 0, 0)),
            pl.BlockSpec((1, 1, d), lambda bb, i: (bb, 0, 0)),
            pl.BlockSpec((ne, d), lambda bb, i: (0, 0)),
            pl.BlockSpec((ne, d), lambda bb, i: (0, 0)),
            pl.BlockSpec((ne, 1), lambda bb, i: (0, 0)),
            pl.BlockSpec((ne, 1), lambda bb, i: (0, 0)),
        ],
        out_specs=[
            pl.BlockSpec((1, tm, d), lambda bb, i: (bb, i, 0)),
            pair, pair, pair,
            pl.BlockSpec((ne, HEAD_DIM), lambda bb, i: (0, 0)),
        ],
        out_shape=[
            jax.ShapeDtypeStruct((b, n, d), F32),
            jax.ShapeDtypeStruct((b, 2, n), jnp.int32),
            jax.ShapeDtypeStruct((b, 2, n), F32),
            jax.ShapeDtypeStruct((b, 2, n), jnp.int32),
            jax.ShapeDtypeStruct((ne, HEAD_DIM), jnp.int32),
        ],
        scratch_shapes=[pltpu.VMEM((ne, 1), F32)],
        compiler_params=_cparams(("arbitrary", "arbitrary")),
        name="norm_router",
    )(x, gain.reshape(1, d), sc, sh, wr_hi, wr_lo, b_router.reshape(ne, 1), counts0)


MOE_TILE = 256
MOE_SCATTER_TILE = 1024
MOE_COMBINE_TILE = 256
MOE_DMA_UNROLL = 8


def _moe_tiles(counts, nt):
    ntile_e = (counts + MOE_TILE - 1) // MOE_TILE
    tile_end = jnp.cumsum(ntile_e)
    off = (tile_end - ntile_e) * MOE_TILE
    tile_expert = jnp.sum(jnp.arange(nt)[:, None] >= tile_end[None, :], axis=1)
    tile_expert = jnp.minimum(tile_expert, N_EXPERTS - 1).astype(jnp.int32)
    return off, tile_expert, tile_end[-1:].astype(jnp.int32)


def _moe_rows(e, rank, off):
    b, _, n = e.shape
    t = b * n
    ef = e.transpose(1, 0, 2).reshape(2, t)
    rf = rank.transpose(1, 0, 2).reshape(2, t)
    pos = rf + jnp.sum(jnp.where(ef[..., None] == jnp.arange(N_EXPERTS), off, 0), axis=-1)
    return pos.reshape(2 * t).astype(jnp.int32)


def _scatter_kernel(pos_ref, h_ref, xs0_hbm, xs_hbm, sem, *, t):
    del xs0_hbm
    tm = h_ref.shape[0]
    base = pl.program_id(0) * tm

    def copies(j):
        return [pltpu.make_async_copy(h_ref.at[pl.ds(j, 1)], xs_hbm.at[pl.ds(pos_ref[k * t + base + j], 1)], sem)
                for k in range(2)]

    def start(j, c):
        for k, cp in enumerate(copies(j)):
            cp.start(priority=k)
        return c

    def wait(j, c):
        for cp in copies(j):
            cp.wait()
        return c

    lax.fori_loop(0, tm, start, 0, unroll=MOE_DMA_UNROLL)
    lax.fori_loop(0, tm, wait, 0, unroll=MOE_DMA_UNROLL)


def _scatter_rows(pos, h, xs_buf):
    t, d = h.shape
    tm = min(MOE_SCATTER_TILE, t)
    any_spec = pl.BlockSpec(memory_space=pl.ANY)
    return pl.pallas_call(
        functools.partial(_scatter_kernel, t=t),
        grid_spec=pltpu.PrefetchScalarGridSpec(
            num_scalar_prefetch=1, grid=(t // tm,),
            in_specs=[pl.BlockSpec((tm, d), lambda i, p: (i, 0)), any_spec], out_specs=any_spec,
            scratch_shapes=[pltpu.SemaphoreType.DMA(())]),
        out_shape=jax.ShapeDtypeStruct(xs_buf.shape, F32),
        input_output_aliases={2: 0},
        compiler_params=pltpu.CompilerParams(dimension_semantics=("arbitrary",), has_side_effects=True,
                                             vmem_limit_bytes=V7X_VMEM_LIMIT),
        name="moe_scatter",
    )(pos, h, xs_buf)


def _experts_kernel(te_ref, nv_ref, xs_ref, wgu_ref, wd_ref, ys_ref):
    del te_ref

    @pl.when(pl.program_id(0) < nv_ref[0])
    def _():
        gu = _dot(xs_ref[...].astype(BF16), wgu_ref[0, 0])
        ff = gu.shape[1] // 2
        gate = gu[:, :ff]
        a = (gate * jax.nn.sigmoid(gate) * gu[:, ff:]).astype(BF16)
        ys_ref[...] = _dot(a, wd_ref[0, 0])


def _experts(xs, tile_expert, nvalid, w_gu, w_down, layer):
    nrows, d = xs.shape
    nt = nrows // MOE_TILE
    f2 = w_gu.shape[-1]
    row = lambda i, te, nv: (jnp.minimum(i, nv[0] - 1), 0)
    wmap = lambda i, te, nv: (layer, te[jnp.minimum(i, nv[0] - 1)], 0, 0)
    return pl.pallas_call(
        _experts_kernel,
        grid_spec=pltpu.PrefetchScalarGridSpec(
            num_scalar_prefetch=2, grid=(nt,),
            in_specs=[
                pl.BlockSpec((MOE_TILE, d), row),
                pl.BlockSpec((1, 1, d, f2), wmap),
                pl.BlockSpec((1, 1, f2 // 2, d), wmap),
            ],
            out_specs=pl.BlockSpec((MOE_TILE, d), row)),
        out_shape=jax.ShapeDtypeStruct((nrows, d), F32),
        compiler_params=_cparams(("arbitrary",)),
        name="moe_experts",
    )(tile_expert, nvalid, xs, w_gu, w_down)


def _combine_kernel(pos_ref, x_ref, w_ref, g2_ref, ys_hbm, o_ref, buf, sem, *, t):
    i = pl.program_id(0)
    nsteps = pl.num_programs(0)
    tm = x_ref.shape[0]

    def copies(tile, slot, j):
        tok = tile * tm + j
        return [pltpu.make_async_copy(ys_hbm.at[pl.ds(pos_ref[k * t + tok], 1)],
                                      buf.at[slot, k, pl.ds(j, 1)], sem.at[slot]) for k in range(2)]

    def issue(tile, slot):
        def body(j, c):
            for k, cp in enumerate(copies(tile, slot, j)):
                cp.start(priority=k)
            return c
        lax.fori_loop(0, tm, body, 0, unroll=MOE_DMA_UNROLL)

    def wait(tile, slot):
        def body(j, c):
            for cp in copies(tile, slot, j):
                cp.wait()
            return c
        lax.fori_loop(0, tm, body, 0, unroll=MOE_DMA_UNROLL)

    @pl.when(i == 0)
    def _():
        issue(0, 0)

    @pl.when(i + 1 < nsteps)
    def _():
        issue(i + 1, (i + 1) % 2)

    slot = i % 2
    wait(i, slot)
    w = w_ref[...]
    y = w[:, 0:1] * buf[slot, 0] + w[:, 1:2] * buf[slot, 1]
    o_ref[...] = x_ref[...] + g2_ref[0] * y


def _combine(pos, x, w, g2, ys):
    b, n, d = x.shape
    t = b * n
    tm = min(MOE_COMBINE_TILE, n)
    per_b = n // tm
    out = pl.pallas_call(
        functools.partial(_combine_kernel, t=t),
        grid_spec=pltpu.PrefetchScalarGridSpec(
            num_scalar_prefetch=1, grid=(t // tm,),
            in_specs=[
                pl.BlockSpec((tm, d), lambda i, p: (i, 0)),
                pl.BlockSpec((tm, 2), lambda i, p: (i, 0)),
                pl.BlockSpec((1, 1, d), lambda i, p: (i // per_b, 0, 0)),
                pl.BlockSpec(memory_space=pl.ANY),
            ],
            out_specs=pl.BlockSpec((tm, d), lambda i, p: (i, 0)),
            scratch_shapes=[pltpu.VMEM((2, 2, tm, d), F32), pltpu.SemaphoreType.DMA((2,))]),
        out_shape=jax.ShapeDtypeStruct((t, d), F32),
        compiler_params=_cparams(("arbitrary",)),
        name="moe_combine",
    )(pos, x.reshape(t, d), w, g2, ys)
    return out.reshape(b, n, d)


def _moe(token_sets, gain, wr_hi, wr_lo, b_router, w_gu, w_down, layer, xs_buf, nt):
    d = token_sets[0][0].shape[-1]
    counts = jnp.zeros((N_EXPERTS, 1), F32)
    routed = []
    for t, sc, sh, _ in token_sets:
        h, e, w, rank, cnt = _router(t, gain, sc, sh, wr_hi, wr_lo, b_router, counts)
        counts = cnt[:, :1].astype(F32)
        routed.append((h, e, w, rank))
    off, tile_expert, nvalid = _moe_tiles(cnt[:, 0], nt)
    xs = jnp.zeros((nt * MOE_TILE, d), F32) if xs_buf is None else xs_buf
    rows = []
    for h, e, w, rank in routed:
        rows.append(_moe_rows(e, rank, off))
        xs = _scatter_rows(rows[-1], h.reshape(-1, d), xs)
    ys = _experts(xs, tile_expert, nvalid, w_gu, w_down, layer)
    outs = [_combine(pos, t, w.transpose(0, 2, 1).reshape(-1, 2), g2, ys)
            for pos, (t, _, _, g2), (_, _, w, _) in zip(rows, token_sets, routed)]
    return outs, xs


def _rope_tables(n):
    t = jnp.arange(n, dtype=jnp.int32)
    row = (t // GRID_W).astype(F32)
    col = (t % GRID_W).astype(F32)
    axis_dim = HEAD_DIM // 2
    inv = ROPE_THETA ** (-jnp.arange(0, axis_dim, 2, dtype=F32) / axis_dim)
    ang = jnp.concatenate([row[:, None] * inv, col[:, None] * inv], axis=-1)
    cos, sin = jnp.cos(ang), jnp.sin(ang)
    return jnp.concatenate([cos, cos], axis=-1), jnp.concatenate([-sin, sin], axis=-1)


def kernel(x, c, ctx, c_ctx, w_mod, b_mod, norm1, norm2, w_in, qk_gain, pool_w, pool_scale,
           na_rpb, w_br, w_out, w_router, b_router, w_gu, w_down):
    b, n, d = x.shape
    depth = w_mod.shape[0]
    rows = n // GRID_W
    kr = min(NA_ROWS, rows)
    assert n % GRID_W == 0 and rows % kr == 0 and b + 1 <= 8

    cos, sin = _rope_tables(n)
    mods = _modulation(jnp.concatenate([c, c_ctx[None, :]], axis=0), w_mod, b_mod)
    wr_t = w_router.T
    wr_hi = wr_t.astype(BF16)
    wr_lo = (wr_t - wr_hi.astype(F32)).astype(BF16)
    w_br_b, w_out_b = w_br.astype(BF16), w_out.astype(BF16)
    w_gu_b, w_down_b = w_gu.astype(BF16), w_down.astype(BF16)
    xs_buf = None
    moe_tiles = 2 * b * (n + ctx.shape[1]) // MOE_TILE + N_EXPERTS

    for l in range(depth):
        last = l == depth - 1
        mx = mods[l, :b].reshape(b, 1, N_MOD, d)
        mc = jnp.broadcast_to(mods[l, b].reshape(1, 1, N_MOD, d), (b, 1, N_MOD, d))
        x_sh1, x_sc1, x_g1, x_sh2, x_sc2, x_g2 = [mx[:, :, k] for k in range(N_MOD)]
        c_sh1, c_sc1, c_g1, c_sh2, c_sc2, c_g2 = [mc[:, :, k] for k in range(N_MOD)]
        pool_w_l = pool_w[l].astype(BF16)

        def proj(t, sc, sh, kind, tables=(None, None)):
            return _inproj(t, norm1[l], sc, sh, w_in, l, qk_gain[l], *tables, kind=kind)

        pckv = proj(ctx, c_sc1, c_sh1, "kv")
        pq = proj(x, x_sc1, x_sh1, "q", (cos, sin))
        pkv = proj(x, x_sc1, x_sh1, "kv", (cos, sin))
        pg = proj(x, x_sc1, x_sh1, "gates")
        y_pool = _pool(pq, pool_w_l, pool_scale[l])
        y_na = _na(pq, pkv, pckv, na_rpb[l])
        y_gqa = _gqa(pq, pkv, pckv, qk_gain[l, 2], qk_gain[l, 3])
        x = _merge(x, pg, y_pool, y_na, 0, y_gqa, 0, w_br_b, w_out_b, l, x_g1)

        token_sets = [(x, x_sc2, x_sh2, x_g2)]
        if not last:
            pcq = proj(ctx, c_sc1, c_sh1, "q")
            pcg = proj(ctx, c_sc1, c_sh1, "gates")
            yc_pool = _pool(pcq, pool_w_l, pool_scale[l])
            yc = _ctx_attn(pcq, pckv)
            ctx = _merge(ctx, pcg, yc_pool, yc, 0, yc, 1, w_br_b, w_out_b, l, c_g1)
            token_sets.append((ctx, c_sc2, c_sh2, c_g2))
        outs, xs_buf = _moe(token_sets, norm2[l], wr_hi, wr_lo, b_router, w_gu_b, w_down_b, l,
                            xs_buf, moe_tiles)
        x = outs[0]
        if not last:
            ctx = outs[1]
    return x
```
